```python
import math
import jax
import jax.numpy as jnp
from jax import lax
import numpy as np

D_MODEL = 1024
BATCH = 8
SEQ = 4096
DEPTH = 4

HEAD_DIM = 64
MIX_WIDTH = D_MODEL
N_HEADS = MIX_WIDTH // HEAD_DIM
N_DIL_HEADS = N_HEADS // 4
N_DIFF_HEADS = (N_HEADS - N_DIL_HEADS) // 2
N_FOX_HEADS = N_HEADS - N_DIL_HEADS - N_DIFF_HEADS
DIFF_QK_DIM = HEAD_DIM // 2
D_FF = ((8 * D_MODEL // 3 + 255) // 256) * 256
DILATED_PATTERNS = ((128, 1), (512, 4), (2048, 16))
Q_BLOCK = 128
ROPE_THETA = 10000.0
NORM_EPS = 1e-6
SUBLN_EPS = 1e-5
MACARON_SCALE = 0.5
NEG_INF = -1e30

SPLIT_SIZES = (
    N_DIFF_HEADS * 2 * DIFF_QK_DIM,
    N_DIFF_HEADS * 2 * DIFF_QK_DIM,
    N_DIFF_HEADS * HEAD_DIM,
    N_FOX_HEADS * HEAD_DIM,
    N_FOX_HEADS * HEAD_DIM,
    N_FOX_HEADS * HEAD_DIM,
    N_FOX_HEADS,
    N_DIL_HEADS * HEAD_DIM,
    N_DIL_HEADS * HEAD_DIM,
    N_DIL_HEADS * HEAD_DIM,
)
IN_WIDTH = sum(SPLIT_SIZES)
SPLIT_POINTS = tuple(int(v) for v in np.cumsum(SPLIT_SIZES)[:-1])

kernel_name = 'hybrid_diff_fox_dilated_macaron'


def rms_norm(x, g, eps=NORM_EPS):
    xf = x.astype(jnp.float32)
    y = xf * lax.rsqrt(jnp.mean(xf * xf, axis=-1, keepdims=True) + eps)
    return (y * g.astype(jnp.float32)).astype(x.dtype)


def rope(x, pos):
    half = x.shape[-1] // 2
    inv_freq = ROPE_THETA ** (-jnp.arange(half, dtype=jnp.float32) / half)
    ang = pos.astype(jnp.float32)[:, None] * inv_freq[None, :]
    cos = jnp.cos(ang)[None, :, None, :]
    sin = jnp.sin(ang)[None, :, None, :]
    xf = x.astype(jnp.float32)
    x1, x2 = xf[..., :half], xf[..., half:]
    return jnp.concatenate([x1 * cos - x2 * sin, x2 * cos + x1 * sin], axis=-1).astype(x.dtype)


def to_blocks(x):
    b, h, s = x.shape[:3]
    x = x.reshape((b, h, s // Q_BLOCK, Q_BLOCK) + x.shape[3:])
    return jnp.moveaxis(x, 2, 0)


def from_blocks(x):
    nb, b, h, blk, d = x.shape
    return jnp.moveaxis(x, 0, 2).reshape(b, h, nb * blk, d).transpose(0, 2, 1, 3)


def causal_mask(qpos, s):
    return qpos[:, None] >= jnp.arange(s)[None, :]


def differential_attention(q, k, v, lam_q1, lam_k1, lam_q2, lam_k2, g_sub, lam_init, pos):
    b, s, h, _ = v.shape
    q = rope(q.reshape(b, s, 2 * h, DIFF_QK_DIM), pos).reshape(b, s, h, 2, DIFF_QK_DIM)
    k = rope(k.reshape(b, s, 2 * h, DIFF_QK_DIM), pos).reshape(b, s, h, 2, DIFF_QK_DIM)
    q = q.transpose(3, 0, 2, 1, 4)
    k = k.transpose(3, 0, 2, 1, 4)
    vt = v.transpose(0, 2, 1, 3)
    f32 = jnp.float32
    lam = (jnp.exp(jnp.sum(lam_q1.astype(f32) * lam_k1.astype(f32)))
           - jnp.exp(jnp.sum(lam_q2.astype(f32) * lam_k2.astype(f32))) + lam_init)
    scale = DIFF_QK_DIM ** -0.5
    k1, k2 = k[0], k[1]

    def block(args):
        q1b, q2b, qpos = args
        mask = causal_mask(qpos, s)
        s1 = jnp.einsum('bhqe,bhke->bhqk', q1b, k1).astype(f32) * scale
        s2 = jnp.einsum('bhqe,bhke->bhqk', q2b, k2).astype(f32) * scale
        p1 = jax.nn.softmax(jnp.where(mask, s1, NEG_INF), axis=-1)
        p2 = jax.nn.softmax(jnp.where(mask, s2, NEG_INF), axis=-1)
        return jnp.einsum('bhqk,bhkd->bhqd', (p1 - lam * p2).astype(vt.dtype), vt)

    o = lax.map(block, (to_blocks(q[0]), to_blocks(q[1]), pos.reshape(-1, Q_BLOCK)))
    o = from_blocks(o)
    return rms_norm(o, g_sub, SUBLN_EPS) * (1.0 - lam_init)


def forgetting_attention(q, k, v, f_logit, b_f, pos):
    b, s, h, d = v.shape
    f32 = jnp.float32
    log_f = jax.nn.log_sigmoid(f_logit.astype(f32) + b_f.astype(f32))
    c = jnp.cumsum(log_f, axis=1).transpose(0, 2, 1)
    qt, kt, vt = (t.transpose(0, 2, 1, 3) for t in (q, k, v))
    scale = d ** -0.5

    def block(args):
        qb, cb, qpos = args
        mask = causal_mask(qpos, s)
        sc = (jnp.einsum('bhqe,bhke->bhqk', qb, kt).astype(f32) * scale
              + (cb[..., :, None] - c[..., None, :]))
        p = jax.nn.softmax(jnp.where(mask, sc, NEG_INF), axis=-1)
        return jnp.einsum('bhqk,bhkd->bhqd', p.astype(vt.dtype), vt)

    o = lax.map(block, (to_blocks(qt), to_blocks(c), pos.reshape(-1, Q_BLOCK)))
    return from_blocks(o)


def dilated_branch(q, k, v, window, dilation):
    b, s, h, d = q.shape
    n = window // dilation
    L = s // dilation
    nb = -(-L // Q_BLOCK)
    lp = nb * Q_BLOCK
    f32 = jnp.float32

    def to_sub(x):
        x = x.reshape(b, L, dilation, h, d).transpose(0, 2, 3, 1, 4)
        x = jnp.pad(x, ((0, 0), (0, 0), (0, 0), (0, lp - L), (0, 0)))
        return x.reshape(b, dilation, h, nb, Q_BLOCK, d)

    def with_prev(x):
        prev = jnp.pad(x, ((0, 0), (0, 0), (0, 0), (1, 0), (0, 0), (0, 0)))[:, :, :, :nb]
        return jnp.concatenate([prev, x], axis=4)

    qs = to_sub(q)
    kw, vw = with_prev(to_sub(k)), with_prev(to_sub(v))
    qi = Q_BLOCK + jnp.arange(Q_BLOCK)
    kj = jnp.arange(2 * Q_BLOCK)
    dist = qi[:, None] - kj[None, :]
    band = (dist >= 0) & (dist <= n)
    first = (jnp.arange(nb) == 0)[:, None, None] & (kj < Q_BLOCK)[None, None, :]
    mask = band[None] & ~first
    sc = jnp.einsum('brhnqe,brhnke->brhnqk', qs, kw).astype(f32) * (d ** -0.5)
    sc = jnp.where(mask, sc, NEG_INF)
    m = jnp.max(sc, axis=-1, keepdims=True)
    p = jnp.exp(sc - m)
    l = jnp.sum(p, axis=-1)
    o = jnp.einsum('brhnqk,brhnkd->brhnqd', p.astype(vw.dtype), vw).astype(f32) / l[..., None]
    lse = m[..., 0] + jnp.log(l)

    def from_sub(x):
        x = x.reshape((b, dilation, h, lp) + x.shape[5:])[:, :, :, :L]
        x = jnp.moveaxis(x, 3, 1)
        return x.reshape((b, s, h) + x.shape[4:])

    return from_sub(o), from_sub(lse)


def dilated_attention(q, k, v, pos):
    q = rope(q, pos)
    k = rope(k, pos)
    outs, lses = zip(*[dilated_branch(q, k, v, w, r) for (w, r) in DILATED_PATTERNS])
    wts = jax.nn.softmax(jnp.stack(lses, axis=0), axis=0)
    o = jnp.sum(wts[..., None] * jnp.stack(outs, axis=0), axis=0)
    return o.astype(v.dtype)


def swiglu(x, w_gate, w_up, w_down):
    return (jax.nn.silu(x @ w_gate) * (x @ w_up)) @ w_down


def hybrid_mixer(xn, w_in, b_f, lam_q1, lam_k1, lam_q2, lam_k2, g_sub, w_o, lam_init):
    b, s, _ = xn.shape
    pos = jnp.arange(s)
    proj = xn @ w_in
    aq, ak, av, fq, fk, fv, fg, cq, ck, cv = jnp.split(proj, SPLIT_POINTS, axis=-1)
    o_a = differential_attention(
        aq.reshape(b, s, N_DIFF_HEADS, 2 * DIFF_QK_DIM), ak.reshape(b, s, N_DIFF_HEADS, 2 * DIFF_QK_DIM),
        av.reshape(b, s, N_DIFF_HEADS, HEAD_DIM), lam_q1, lam_k1, lam_q2, lam_k2, g_sub, lam_init, pos)
    o_b = forgetting_attention(
        fq.reshape(b, s, N_FOX_HEADS, HEAD_DIM), fk.reshape(b, s, N_FOX_HEADS, HEAD_DIM),
        fv.reshape(b, s, N_FOX_HEADS, HEAD_DIM), fg, b_f, pos)
    o_c = dilated_attention(
        cq.reshape(b, s, N_DIL_HEADS, HEAD_DIM), ck.reshape(b, s, N_DIL_HEADS, HEAD_DIM),
        cv.reshape(b, s, N_DIL_HEADS, HEAD_DIM), pos)
    o = jnp.concatenate([o_a.reshape(b, s, -1), o_b.reshape(b, s, -1), o_c.reshape(b, s, -1)], axis=-1)
    return o @ w_o


def setup_inputs(seed: int = 0) -> dict:
    key = jax.random.key(seed)
    ks = jax.random.split(key, 20)
    nrm = jax.random.normal
    f32 = jnp.float32
    return {
        'x': nrm(ks[0], (BATCH, SEQ, D_MODEL), f32),
        'w_in': nrm(ks[1], (DEPTH, D_MODEL, IN_WIDTH), f32) * D_MODEL ** -0.5,
        'b_f': 3.0 + 0.5 * nrm(ks[2], (DEPTH, N_FOX_HEADS), f32),
        'lam_q1': 0.1 * nrm(ks[3], (DEPTH, DIFF_QK_DIM), f32),
        'lam_k1': 0.1 * nrm(ks[4], (DEPTH, DIFF_QK_DIM), f32),
        'lam_q2': 0.1 * nrm(ks[5], (DEPTH, DIFF_QK_DIM), f32),
        'lam_k2': 0.1 * nrm(ks[6], (DEPTH, DIFF_QK_DIM), f32),
        'g_sub': 1.0 + 0.1 * nrm(ks[7], (DEPTH, HEAD_DIM), f32),
        'w_o': nrm(ks[8], (DEPTH, MIX_WIDTH, D_MODEL), f32) * MIX_WIDTH ** -0.5,
        'g_ffn1': 1.0 + 0.1 * nrm(ks[9], (DEPTH, D_MODEL), f32),
        'w1_gate': nrm(ks[10], (DEPTH, D_MODEL, D_FF), f32) * D_MODEL ** -0.5,
        'w1_up': nrm(ks[11], (DEPTH, D_MODEL, D_FF), f32) * D_MODEL ** -0.5,
        'w1_down': nrm(ks[12], (DEPTH, D_FF, D_MODEL), f32) * D_FF ** -0.5,
        'g_mix': 1.0 + 0.1 * nrm(ks[13], (DEPTH, D_MODEL), f32),
        'g_ffn2': 1.0 + 0.1 * nrm(ks[14], (DEPTH, D_MODEL), f32),
        'w2_gate': nrm(ks[15], (DEPTH, D_MODEL, D_FF), f32) * D_MODEL ** -0.5,
        'w2_up': nrm(ks[16], (DEPTH, D_MODEL, D_FF), f32) * D_MODEL ** -0.5,
        'w2_down': nrm(ks[17], (DEPTH, D_FF, D_MODEL), f32) * D_FF ** -0.5,
        'g_final': 1.0 + 0.1 * nrm(ks[18], (D_MODEL,), f32),
    }


def reference(x, w_in, b_f, lam_q1, lam_k1, lam_q2, lam_k2, g_sub, w_o, g_ffn1, w1_gate, w1_up, w1_down,
              g_mix, g_ffn2, w2_gate, w2_up, w2_down, g_final):
    for l in range(DEPTH):
        lam_init = 0.8 - 0.6 * math.exp(-0.3 * l)
        x = x + MACARON_SCALE * swiglu(rms_norm(x, g_ffn1[l]), w1_gate[l], w1_up[l], w1_down[l])
        x = x + hybrid_mixer(rms_norm(x, g_mix[l]), w_in[l], b_f[l], lam_q1[l], lam_k1[l], lam_q2[l],
                             lam_k2[l], g_sub[l], w_o[l], lam_init)
        x = x + MACARON_SCALE * swiglu(rms_norm(x, g_ffn2[l]), w2_gate[l], w2_up[l], w2_down[l])
    return rms_norm(x, g_final)
```

```python
import functools
import math

import numpy as np
import jax
import jax.numpy as jnp
from jax import lax
from jax.experimental import pallas as pl
from jax.experimental.pallas import tpu as pltpu

F32 = jnp.float32
BF16 = jnp.bfloat16

HEAD_DIM = 64
N_DIFF_HEADS = 6
N_FOX_HEADS = 6
N_DIL_HEADS = 4
DIFF_QK_DIM = 32
DILATED_PATTERNS = ((128, 1), (512, 4), (2048, 16))
ROPE_THETA = 10000.0
NORM_EPS = 1e-6
SUBLN_EPS = 1e-5
MACARON_SCALE = 0.5
NEG_BIG = -1e30
LOG2E = 1.4426950408889634

LANES = 128
BF16_ROWS = 16
V_ROWS = HEAD_DIM + BF16_ROWS
GATE_STRIDE = 8
VMEM_LIMIT_BYTES = 56 * 1024 * 1024


def _dot(a, b):
    return jnp.dot(a, b, preferred_element_type=F32)


def _dot_nt(a, b):
    return lax.dot_general(a, b, (((1,), (1,)), ((), ())), preferred_element_type=F32)


def _rms(x, g, eps):
    ms = jnp.mean(x * x, axis=-1, keepdims=True)
    return x * lax.rsqrt(ms + eps) * g


def _split3(x):
    hi = x.astype(BF16)
    r1 = x - hi.astype(F32)
    mid = r1.astype(BF16)
    lo = (r1 - mid.astype(F32)).astype(BF16)
    return hi, mid, lo


def _ffn_body(*refs, has_attn, has_final, n_chunks):
    refs = list(refs)
    x_ref = refs.pop(0)
    if has_attn:
        oa_ref, ob_ref, oc_ref, woa_ref, wob_ref, woc_ref = refs[:6]
        refs = refs[6:]
    g_ref, wg_ref, wu_ref, wd_ref = refs[:4]
    refs = refs[4:]
    if has_final:
        gf_ref = refs.pop(0)
    out_ref = refs[0]

    x = x_ref[...]
    if has_attn:
        x = x + (_dot(oa_ref[...], woa_ref[...]) + _dot(ob_ref[...], wob_ref[...])
                 + _dot(oc_ref[...], woc_ref[...]))
    xn = _rms(x, g_ref[...], NORM_EPS).astype(BF16)
    ck = wg_ref.shape[1] // n_chunks
    acc = None
    for c in range(n_chunks):
        sl = slice(c * ck, (c + 1) * ck)
        gate = _dot(xn, wg_ref[:, sl])
        up = _dot(xn, wu_ref[:, sl])
        h = (gate * jax.nn.sigmoid(gate) * up).astype(BF16)
        d = _dot(h, wd_ref[sl, :])
        acc = d if acc is None else acc + d
    y = x + MACARON_SCALE * acc
    if has_final:
        y = _rms(y, gf_ref[...], NORM_EPS)
    out_ref[...] = y


def _const_spec(shape):
    nd = len(shape)
    return pl.BlockSpec(shape, lambda *_: (0,) * nd, pipeline_mode=pl.Buffered(1))


def _ffn_call(x2d, g, wg, wu, wd, attn=None, g_final=None, tm=512):
    n, d = x2d.shape
    dff = wg.shape[1]
    args = [x2d]
    specs = [pl.BlockSpec((tm, d), lambda i: (i, 0))]
    if attn is not None:
        o_list, w_list = attn
        for o in o_list:
            args.append(o)
            specs.append(pl.BlockSpec((tm, o.shape[1]), lambda i: (i, 0)))
        for w in w_list:
            args.append(w)
            specs.append(_const_spec(w.shape))
    args += [g.reshape(1, d), wg, wu, wd]
    specs += [_const_spec((1, d)), _const_spec((d, dff)), _const_spec((d, dff)), _const_spec((dff, d))]
    if g_final is not None:
        args.append(g_final.reshape(1, d))
        specs.append(_const_spec((1, d)))
    n_chunks = 2 if dff % (2 * LANES) == 0 else 1
    body = functools.partial(_ffn_body, has_attn=attn is not None, has_final=g_final is not None,
                             n_chunks=n_chunks)
    return pl.pallas_call(
        body,
        grid=(n // tm,),
        in_specs=specs,
        out_specs=pl.BlockSpec((tm, d), lambda i: (i, 0)),
        out_shape=jax.ShapeDtypeStruct((n, d), F32),
        compiler_params=pltpu.CompilerParams(dimension_semantics=("arbitrary",),
                                             vmem_limit_bytes=VMEM_LIMIT_BYTES),
        name="ffn",
    )(*args)


N_NAT = 384 + 384 + 256 + LANES
N_TR = 2 * (384 + 384 + 256)


def _proj_body(x_ref, g_ref, wn_ref, wt_ref, bf_ref, tri_ref, ehi_ref, emid_ref, elo_ref, aux1_ref,
               cnd_ref, snd_ref, cnc_ref, snc_ref, ctd_ref, std_ref, ctc_ref, stc_ref,
               qtd_ref, kd_ref, vtd_ref, qtf_ref, kf_ref, vtf_ref, cq_ref, qtc_ref, kc_ref, vtc_ref,
               carry_ref):
    tm = x_ref.shape[0]

    @pl.when(pl.program_id(1) == 0)
    def _():
        carry_ref[...] = jnp.zeros_like(carry_ref)

    xn = _rms(x_ref[...], g_ref[...], NORM_EPS).astype(BF16)
    rn = _dot(xn, wn_ref[...])
    rt = _dot_nt(wt_ref[...], xn)

    lane = lax.broadcasted_iota(jnp.int32, (tm, LANES), 1)

    def rope_nat(blk, half, cos, sin_signed):
        first = (lane & half) == 0
        rot = jnp.where(first, pltpu.roll(blk, LANES - half, 1), pltpu.roll(blk, half, 1))
        return (blk * cos + rot * sin_signed).astype(BF16)

    for p in range(3):
        kd_ref[p] = rope_nat(rn[:, LANES * p:LANES * (p + 1)], DIFF_QK_DIM // 2, cnd_ref[...], snd_ref[...])
    for p in range(2):
        c0 = 768 + LANES * p
        kc_ref[p] = rope_nat(rn[:, c0:c0 + LANES], HEAD_DIM // 2, cnc_ref[...], snc_ref[...])

    def rope_tr(ref, p, r0, x1, x2, cos, sin, scale):
        half = x1.shape[0]
        ref[p, r0:r0 + half, :] = ((x1 * cos - x2 * sin) * scale).astype(BF16)
        ref[p, r0 + half:r0 + 2 * half, :] = ((x2 * cos + x1 * sin) * scale).astype(BF16)

    sd = DIFF_QK_DIM ** -0.5 * LOG2E
    hd = DIFF_QK_DIM // 2
    for g in range(2 * N_DIFF_HEADS):
        b0 = DIFF_QK_DIM * g
        rope_tr(qtd_ref, g // 4, DIFF_QK_DIM * (g % 4), rt[b0:b0 + hd], rt[b0 + hd:b0 + 2 * hd],
                ctd_ref[...], std_ref[...], sd)
    sf = HEAD_DIM ** -0.5 * LOG2E
    for p in range(3):
        b0 = 384 + LANES * p
        qtf_ref[p] = (rt[b0:b0 + LANES] * sf).astype(BF16)
    hc = HEAD_DIM // 2
    for h in range(N_DIL_HEADS):
        b0 = 768 + HEAD_DIM * h
        rope_tr(qtc_ref, h // 2, HEAD_DIM * (h % 2), rt[b0:b0 + hc], rt[b0 + hc:b0 + 2 * hc],
                ctc_ref[...], stc_ref[...], sf)

    ones_rows = jnp.ones((BF16_ROWS, tm), BF16)

    def put_vt(ref, base, n_pairs):
        for p in range(n_pairs):
            for h in range(2):
                b0 = base + LANES * p + HEAD_DIM * h
                ref[p, V_ROWS * h:V_ROWS * h + HEAD_DIM, :] = rt[b0:b0 + HEAD_DIM].astype(BF16)
                ref[p, V_ROWS * h + HEAD_DIM:V_ROWS * (h + 1), :] = ones_rows

    put_vt(vtd_ref, 1024, 3)
    put_vt(vtf_ref, 1408, 3)
    put_vt(vtc_ref, 1792, 2)

    z = rn[:, 1024:1024 + LANES] + bf_ref[...]
    logf = (jnp.minimum(z, 0.0) - jnp.log1p(jnp.exp(-jnp.abs(z)))) * LOG2E
    hi, mid, lo = _split3(logf)
    tri = tri_ref[...]
    c = _dot(tri, hi) + _dot(tri, mid) + _dot(tri, lo) + carry_ref[...]
    carry_ref[...] = c[tm - 1:tm, :]
    cq_ref[...] = c.T[0:3 * GATE_STRIDE, :]
    chi, cmid, clo = _split3(c)
    aux = _dot(chi, ehi_ref[...]) + _dot(cmid, emid_ref[...]) + _dot(clo, elo_ref[...]) + aux1_ref[...]
    for p in range(3):
        kf_ref[p, :, 0:LANES] = rn[:, 384 + LANES * p:384 + LANES * (p + 1)].astype(BF16)
        kf_ref[p, :, LANES:2 * LANES] = aux[:, LANES * p:LANES * (p + 1)].astype(BF16)


def _proj_call(x, g_mix, wn, wt, bf_pad, consts, T):
    b, s, d = x.shape
    nt = s // T
    tile5 = lambda npair, r, c: pl.BlockSpec((None, npair, None, r, c), lambda bi, ti: (bi, 0, ti, 0, 0))
    nat_tab = pl.BlockSpec((T, LANES), lambda bi, ti: (ti, 0))
    tr_tab = lambda r: pl.BlockSpec((r, T), lambda bi, ti: (0, ti))
    in_specs = [
        pl.BlockSpec((None, T, d), lambda bi, ti: (bi, ti, 0)),
        _const_spec((1, d)), _const_spec(wn.shape), _const_spec(wt.shape), _const_spec((1, LANES)),
        _const_spec((T, T)), _const_spec((LANES, 384)), _const_spec((LANES, 384)), _const_spec((LANES, 384)),
        _const_spec((1, 384)),
        nat_tab, nat_tab, nat_tab, nat_tab,
        tr_tab(DIFF_QK_DIM // 2), tr_tab(DIFF_QK_DIM // 2), tr_tab(HEAD_DIM // 2), tr_tab(HEAD_DIM // 2),
    ]
    out_shape = [
        jax.ShapeDtypeStruct((b, 3, nt, LANES, T), BF16),
        jax.ShapeDtypeStruct((b, 3, nt, T, LANES), BF16),
        jax.ShapeDtypeStruct((b, 3, nt, 2 * V_ROWS, T), BF16),
        jax.ShapeDtypeStruct((b, 3, nt, LANES, T), BF16),
        jax.ShapeDtypeStruct((b, 3, nt, T, 2 * LANES), BF16),
        jax.ShapeDtypeStruct((b, 3, nt, 2 * V_ROWS, T), BF16),
        jax.ShapeDtypeStruct((b, nt, 3 * GATE_STRIDE, T), F32),
        jax.ShapeDtypeStruct((b, 2, nt, LANES, T), BF16),
        jax.ShapeDtypeStruct((b, 2, nt, T, LANES), BF16),
        jax.ShapeDtypeStruct((b, 2, nt, 2 * V_ROWS, T), BF16),
    ]
    out_specs = [
        tile5(3, LANES, T), tile5(3, T, LANES), tile5(3, 2 * V_ROWS, T),
        tile5(3, LANES, T), tile5(3, T, 2 * LANES), tile5(3, 2 * V_ROWS, T),
        pl.BlockSpec((None, None, 3 * GATE_STRIDE, T), lambda bi, ti: (bi, ti, 0, 0)),
        tile5(2, LANES, T), tile5(2, T, LANES), tile5(2, 2 * V_ROWS, T),
    ]
    return pl.pallas_call(
        _proj_body,
        grid=(b, nt),
        in_specs=in_specs,
        out_specs=out_specs,
        out_shape=out_shape,
        scratch_shapes=[pltpu.VMEM((1, LANES), F32)],
        compiler_params=pltpu.CompilerParams(dimension_semantics=("arbitrary", "arbitrary"),
                                             vmem_limit_bytes=VMEM_LIMIT_BYTES),
        name="proj",
    )(x, g_mix.reshape(1, d), wn, wt, bf_pad, consts["tri"], consts["ehi"], consts["emid"], consts["elo"],
      consts["aux1"], consts["cnd"], consts["snd"], consts["cnc"], consts["snc"],
      consts["ctd"], consts["std"], consts["ctc"], consts["stc"])


def _attn_body(*refs, mode, n_back):
    if mode == "fox":
        qt_ref, k_ref, vt_ref, cq_ref, o_ref, qe_ref, m_ref, acc_ref = refs
    elif mode == "diff":
        qt_ref, k_ref, vt_ref, lam_ref, gcol_ref, o_ref, qe_ref, m_ref, acc_ref = refs
    else:
        qt_ref, k_ref, vt_ref, bias_ref, o_ref, qe_ref, m_ref, acc_ref = refs
    T = qt_ref.shape[1]
    n_sc = qe_ref.shape[0]
    i = pl.program_id(2)

    rows = LANES // n_sc
    zeros_q = jnp.zeros((LANES, T), BF16)
    for sc in range(n_sc):
        qe_ref[sc, 0:LANES, :] = zeros_q
        qe_ref[sc, rows * sc:rows * (sc + 1), :] = qt_ref[rows * sc:rows * (sc + 1), :]
    if mode == "fox":
        cq = cq_ref[...]
        for h in range(2):
            terms = _split3(cq[h:h + 1, :])
            blocks = [jnp.broadcast_to(t.astype(F32), (BF16_ROWS, T)) for t in terms]
            blocks += [jnp.full((BF16_ROWS, T), 1.0 if h == hh else 0.0, F32) for hh in range(2)]
            blocks += [jnp.zeros((LANES - 5 * BF16_ROWS, T), F32)]
            qe_ref[h, LANES:2 * LANES, :] = jnp.concatenate(blocks, axis=0).astype(BF16)

    def scores(j, sc, diag):
        s = _dot(k_ref[j], qe_ref[sc])
        if mode == "dil":
            s = s + bias_ref[i - j]
        elif diag:
            kk = lax.broadcasted_iota(jnp.int32, (T, T), 0)
            qq = lax.broadcasted_iota(jnp.int32, (T, T), 1)
            s = jnp.where(kk <= qq, s, NEG_BIG)
        return s

    def v_tile(j, sc):
        h = sc * 2 // n_sc
        return vt_ref[j, V_ROWS * h:V_ROWS * (h + 1), :]

    for sc in range(n_sc):
        s = scores(i, sc, True)
        m = jnp.max(s, axis=0, keepdims=True)
        p = jnp.exp2(s - m).astype(BF16)
        acc_ref[sc] = _dot(v_tile(i, sc), p)
        m_ref[sc] = m

    def step(j, carry):
        for sc in range(n_sc):
            s = scores(j, sc, False)
            m_old = m_ref[sc]
            m_new = jnp.maximum(m_old, jnp.max(s, axis=0, keepdims=True))
            alpha = jnp.exp2(m_old - m_new)
            p = jnp.exp2(s - m_new).astype(BF16)
            acc_ref[sc] = acc_ref[sc] * alpha + _dot(v_tile(j, sc), p)
            m_ref[sc] = m_new
        return carry

    lo = jnp.maximum(i - n_back, 0) if mode == "dil" else 0
    lax.fori_loop(lo, i, step, 0)

    def normalised(sc):
        a = acc_ref[sc]
        return a[0:HEAD_DIM] / a[HEAD_DIM:HEAD_DIM + 1]

    if mode == "diff":
        lam_init = lam_ref[4:5, 0:1]
        lam = (jnp.exp(jnp.sum(lam_ref[0:1] * lam_ref[1:2], axis=1, keepdims=True))
               - jnp.exp(jnp.sum(lam_ref[2:3] * lam_ref[3:4], axis=1, keepdims=True)) + lam_init)
        outs = []
        for h in range(2):
            d = normalised(2 * h) - lam * normalised(2 * h + 1)
            ms = jnp.mean(d * d, axis=0, keepdims=True)
            outs.append(d * lax.rsqrt(ms + SUBLN_EPS) * gcol_ref[...] * (1.0 - lam_init))
    else:
        outs = [normalised(h) for h in range(2)]
    o_ref[...] = jnp.concatenate(outs, axis=0).T.astype(BF16)


def _attn_call(mode, qt, k, vt, extra, T, n_back=0):
    b, npair, nt = qt.shape[:3]
    r = k.shape[-1]
    n_sc = 4 if mode == "diff" else 2
    in_specs = [
        pl.BlockSpec((None, None, None, LANES, T), lambda bi, p, i: (bi, p, i, 0, 0)),
        pl.BlockSpec((None, None, nt, T, r), lambda bi, p, i: (bi, p, 0, 0, 0)),
        pl.BlockSpec((None, None, nt, 2 * V_ROWS, T), lambda bi, p, i: (bi, p, 0, 0, 0)),
    ]
    if mode == "fox":
        in_specs.append(pl.BlockSpec((None, None, GATE_STRIDE, T), lambda bi, p, i: (bi, i, p, 0)))
    elif mode == "diff":
        in_specs += [_const_spec((8, LANES)), _const_spec((HEAD_DIM, 1))]
    else:
        in_specs.append(_const_spec(extra[0].shape))
    body = functools.partial(_attn_body, mode=mode, n_back=n_back)
    return pl.pallas_call(
        body,
        grid=(b, npair, nt),
        in_specs=in_specs,
        out_specs=pl.BlockSpec((None, T, LANES), lambda bi, p, i: (bi, i, p)),
        out_shape=jax.ShapeDtypeStruct((b, nt * T, npair * LANES), BF16),
        scratch_shapes=[pltpu.VMEM((n_sc, r, T), BF16),
                        pltpu.VMEM((n_sc, 1, T), F32),
                        pltpu.VMEM((n_sc, V_ROWS, T), F32)],
        compiler_params=pltpu.CompilerParams(dimension_semantics=("arbitrary", "arbitrary", "arbitrary"),
                                             vmem_limit_bytes=VMEM_LIMIT_BYTES),
        name="attn_" + mode,
    )(qt, k, vt, *extra)


def _constants(s, T):
    pos = np.arange(s, dtype=np.float64)[:, None]

    def angles(half):
        inv = ROPE_THETA ** (-np.arange(half, dtype=np.float64) / half)
        return pos * inv[None, :]

    ad, ac = angles(DIFF_QK_DIM // 2), angles(HEAD_DIM // 2)

    def nat(a):
        reps = LANES // (2 * a.shape[1])
        cos = np.tile(np.concatenate([np.cos(a), np.cos(a)], axis=1), (1, reps))
        sin = np.tile(np.concatenate([-np.sin(a), np.sin(a)], axis=1), (1, reps))
        return cos.astype(np.float32), sin.astype(np.float32)

    cnd, snd = nat(ad)
    cnc, snc = nat(ac)
    consts = dict(cnd=cnd, snd=snd, cnc=cnc, snc=snc,
                  ctd=np.cos(ad).T.astype(np.float32), std=np.sin(ad).T.astype(np.float32),
                  ctc=np.cos(ac).T.astype(np.float32), stc=np.sin(ac).T.astype(np.float32))
    consts["tri"] = np.tril(np.ones((T, T), np.float32))
    e = np.zeros((3, LANES, 384), np.float32)
    for h in range(N_FOX_HEADS):
        src = GATE_STRIDE * (h // 2) + h % 2
        for t in range(3):
            e[t, src, LANES * (h // 2) + 3 * BF16_ROWS + BF16_ROWS * (h % 2) + t] = -1.0
    consts["ehi"], consts["emid"], consts["elo"] = e[0], e[1], e[2]
    aux1 = np.zeros((1, 384), np.float32)
    for p in range(3):
        for t in range(3):
            aux1[0, LANES * p + BF16_ROWS * t] = 1.0
    consts["aux1"] = aux1
    n_back = -(-max(w for w, _ in DILATED_PATTERNS) // T)
    kk = np.arange(T)[:, None]
    qq = np.arange(T)[None, :]
    bias = np.zeros((n_back + 1, T, T), np.float32)
    for o in range(n_back + 1):
        delta = o * T + qq - kk
        mult = np.zeros((T, T), np.float64)
        for w, dil in DILATED_PATTERNS:
            mult += (delta >= 0) & (delta <= w) & (delta % dil == 0)
        bias[o] = np.where(mult > 0, np.log2(np.maximum(mult, 1.0)), NEG_BIG)
    out = {}
    for name, v in consts.items():
        out[name] = jnp.asarray(v, BF16 if name in ("tri", "ehi", "emid", "elo") else F32)
    return out, jnp.asarray(bias), n_back


def _layer_weights(w_in_l, b_f_l):
    d = w_in_l.shape[0]
    sizes = [384] * 6 + [N_FOX_HEADS, 256, 256, 256]
    aq, ak, av, fq, fk, fv, fg, cq, ck, cv = jnp.split(w_in_l, np.cumsum(sizes)[:-1].tolist(), axis=1)
    lanes = np.array([GATE_STRIDE * (h // 2) + h % 2 for h in range(N_FOX_HEADS)])
    fg_pad = jnp.zeros((d, LANES), F32).at[:, lanes].set(fg)
    bf_pad = jnp.zeros((1, LANES), F32).at[0, lanes].set(b_f_l)
    wn = jnp.concatenate([ak, fk, ck, fg_pad], axis=1).astype(BF16)
    wt = jnp.concatenate([aq, fq, cq, av, fv, cv], axis=1).T.astype(BF16)
    return wn, wt, bf_pad


def _forward(x, w_in, b_f, lam_q1, lam_k1, lam_q2, lam_k2, g_sub, w_o, g_ffn1, w1_gate, w1_up, w1_down,
             g_mix, g_ffn2, w2_gate, w2_up, w2_down, g_final, T=512, tm=512):
    b, s, d = x.shape
    depth = w_in.shape[0]
    assert s % T == 0 and (b * s) % tm == 0 and d == HEAD_DIM * (N_DIFF_HEADS + N_FOX_HEADS + N_DIL_HEADS)
    consts, bias, n_back = _constants(s, T)
    x2d = x.reshape(b * s, d)
    for l in range(depth):
        lam_init = 0.8 - 0.6 * math.exp(-0.3 * l)
        if l == 0:
            x2d = _ffn_call(x2d, g_ffn1[l], w1_gate[l].astype(BF16), w1_up[l].astype(BF16),
                            w1_down[l].astype(BF16), tm=tm)
        wn, wt, bf_pad = _layer_weights(w_in[l], b_f[l])
        (qtd, kd, vtd, qtf, kf, vtf, cqf, qtc, kc, vtc) = _proj_call(
            x2d.reshape(b, s, d), g_mix[l], wn, wt, bf_pad, consts, T)
        lam_rows = jnp.zeros((8, LANES), F32)
        lam_rows = lam_rows.at[0:4, 0:DIFF_QK_DIM].set(jnp.stack([lam_q1[l], lam_k1[l], lam_q2[l], lam_k2[l]]))
        lam_rows = lam_rows.at[4, :].set(lam_init)
        o_a = _attn_call("diff", qtd, kd, vtd, (lam_rows, g_sub[l].reshape(HEAD_DIM, 1)), T)
        o_b = _attn_call("fox", qtf, kf, vtf, (cqf,), T)
        o_c = _attn_call("dil", qtc, kc, vtc, (bias,), T, n_back=n_back)
        wo = w_o[l].astype(BF16)
        attn = ([o_a.reshape(b * s, -1), o_b.reshape(b * s, -1), o_c.reshape(b * s, -1)],
                [wo[0:384], wo[384:768], wo[768:1024]])
        last = l == depth - 1
        x2d = _ffn_call(x2d, g_ffn2[l], w2_gate[l].astype(BF16), w2_up[l].astype(BF16), w2_down[l].astype(BF16),
                        attn=attn, g_final=g_final if last else None, tm=tm)
        if not last:
            x2d = _ffn_call(x2d, g_ffn1[l + 1], w1_gate[l + 1].astype(BF16), w1_up[l + 1].astype(BF16),
                            w1_down[l + 1].astype(BF16), tm=tm)
    return x2d.reshape(b, s, d)


def kernel(x, w_in, b_f, lam_q1, lam_k1, lam_q2, lam_k2, g_sub, w_o, g_ffn1, w1_gate, w1_up, w1_down,
           g_mix, g_ffn2, w2_gate, w2_up, w2_down, g_final):
    return _forward(x, w_in, b_f, lam_q1, lam_k1, lam_q2, lam_k2, g_sub, w_o, g_ffn1, w1_gate, w1_up,
                    w1_down, g_mix, g_ffn2, w2_gate, w2_up, w2_down, g_final)
```

```python
import functools
import math

import numpy as np
import jax
import jax.numpy as jnp
from jax import lax
from jax.experimental import pallas as pl
from jax.experimental.pallas import tpu as pltpu

F32 = jnp.float32
BF16 = jnp.bfloat16

HEAD_DIM = 64
N_DIFF_HEADS = 6
N_FOX_HEADS = 6
N_DIL_HEADS = 4
DIFF_QK_DIM = 32
DILATED_PATTERNS = ((128, 1), (512, 4), (2048, 16))
ROPE_THETA = 10000.0
NORM_EPS = 1e-6
SUBLN_EPS = 1e-5
MACARON_SCALE = 0.5
NEG_BIG = -1e30
LOG2E = 1.4426950408889634

LANES = 128
BF16_ROWS = 16
V_ROWS = HEAD_DIM + BF16_ROWS
GATE_STRIDE = 8
VMEM_LIMIT_BYTES = 56 * 1024 * 1024


def _dot(a, b):
    return jnp.dot(a, b, preferred_element_type=F32)


def _dot_nt(a, b):
    return lax.dot_general(a, b, (((1,), (1,)), ((), ())), preferred_element_type=F32)


def _rms(x, g, eps):
    ms = jnp.mean(x * x, axis=-1, keepdims=True)
    return x * lax.rsqrt(ms + eps) * g


def _split3(x):
    hi = x.astype(BF16)
    r1 = x - hi.astype(F32)
    mid = r1.astype(BF16)
    lo = (r1 - mid.astype(F32)).astype(BF16)
    return hi, mid, lo


def _ffn_body(*refs, has_attn, has_final, n_chunks):
    refs = list(refs)
    x_ref = refs.pop(0)
    if has_attn:
        oa_ref, ob_ref, oc_ref, woa_ref, wob_ref, woc_ref = refs[:6]
        refs = refs[6:]
    g_ref, wg_ref, wu_ref, wd_ref = refs[:4]
    refs = refs[4:]
    if has_final:
        gf_ref = refs.pop(0)
    out_ref = refs[0]

    x = x_ref[...]
    if has_attn:
        x = x + (_dot(oa_ref[...], woa_ref[...]) + _dot(ob_ref[...], wob_ref[...])
                 + _dot(oc_ref[...], woc_ref[...]))
    xn = _rms(x, g_ref[...], NORM_EPS).astype(BF16)
    ck = wg_ref.shape[1] // n_chunks
    acc = None
    for c in range(n_chunks):
        sl = slice(c * ck, (c + 1) * ck)
        gate = _dot(xn, wg_ref[:, sl])
        up = _dot(xn, wu_ref[:, sl])
        h = (gate * jax.nn.sigmoid(gate) * up).astype(BF16)
        d = _dot(h, wd_ref[sl, :])
        acc = d if acc is None else acc + d
    y = x + MACARON_SCALE * acc
    if has_final:
        y = _rms(y, gf_ref[...], NORM_EPS)
    out_ref[...] = y


def _const_spec(shape):
    nd = len(shape)
    return pl.BlockSpec(shape, lambda *_: (0,) * nd, pipeline_mode=pl.Buffered(1))


def _ffn_call(x2d, g, wg, wu, wd, attn=None, g_final=None, tm=512):
    n, d = x2d.shape
    dff = wg.shape[1]
    args = [x2d]
    specs = [pl.BlockSpec((tm, d), lambda i: (i, 0))]
    if attn is not None:
        o_list, w_list = attn
        for o in o_list:
            args.append(o)
            specs.append(pl.BlockSpec((tm, o.shape[1]), lambda i: (i, 0)))
        for w in w_list:
            args.append(w)
            specs.append(_const_spec(w.shape))
    args += [g.reshape(1, d), wg, wu, wd]
    specs += [_const_spec((1, d)), _const_spec((d, dff)), _const_spec((d, dff)), _const_spec((dff, d))]
    if g_final is not None:
        args.append(g_final.reshape(1, d))
        specs.append(_const_spec((1, d)))
    n_chunks = 2 if dff % (2 * LANES) == 0 else 1
    body = functools.partial(_ffn_body, has_attn=attn is not None, has_final=g_final is not None,
                             n_chunks=n_chunks)
    return pl.pallas_call(
        body,
        grid=(n // tm,),
        in_specs=specs,
        out_specs=pl.BlockSpec((tm, d), lambda i: (i, 0)),
        out_shape=jax.ShapeDtypeStruct((n, d), F32),
        compiler_params=pltpu.CompilerParams(dimension_semantics=("arbitrary",),
                                             vmem_limit_bytes=VMEM_LIMIT_BYTES),
        name="ffn",
    )(*args)


N_NAT = 384 + 384 + 256 + LANES
N_TR = 2 * (384 + 384 + 256)


def _proj_body(x_ref, g_ref, wn_ref, wt_ref, bf_ref, tri_ref, ehi_ref, emid_ref, elo_ref, aux1_ref,
               cnd_ref, snd_ref, cnc_ref, snc_ref, ctd_ref, std_ref, ctc_ref, stc_ref,
               qtd_ref, kd_ref, vtd_ref, qtf_ref, kf_ref, vtf_ref, cq_ref, qtc_ref, kc_ref, vtc_ref,
               carry_ref):
    tm = x_ref.shape[0]

    @pl.when(pl.program_id(1) == 0)
    def _():
        carry_ref[...] = jnp.zeros_like(carry_ref)

    xn = _rms(x_ref[...], g_ref[...], NORM_EPS).astype(BF16)
    rn = _dot(xn, wn_ref[...])
    rt = _dot_nt(wt_ref[...], xn)

    lane = lax.broadcasted_iota(jnp.int32, (tm, LANES), 1)

    def rope_nat(blk, half, cos, sin_signed):
        first = (lane & half) == 0
        rot = jnp.where(first, pltpu.roll(blk, LANES - half, 1), pltpu.roll(blk, half, 1))
        return (blk * cos + rot * sin_signed).astype(BF16)

    for p in range(3):
        kd_ref[p] = rope_nat(rn[:, LANES * p:LANES * (p + 1)], DIFF_QK_DIM // 2, cnd_ref[...], snd_ref[...])
    for p in range(2):
        c0 = 768 + LANES * p
        kc_ref[p] = rope_nat(rn[:, c0:c0 + LANES], HEAD_DIM // 2, cnc_ref[...], snc_ref[...])

    def rope_tr(ref, p, r0, x1, x2, cos, sin, scale):
        half = x1.shape[0]
        ref[p, r0:r0 + half, :] = ((x1 * cos - x2 * sin) * scale).astype(BF16)
        ref[p, r0 + half:r0 + 2 * half, :] = ((x2 * cos + x1 * sin) * scale).astype(BF16)

    sd = DIFF_QK_DIM ** -0.5 * LOG2E
    hd = DIFF_QK_DIM // 2
    for g in range(2 * N_DIFF_HEADS):
        b0 = DIFF_QK_DIM * g
        rope_tr(qtd_ref, g // 4, DIFF_QK_DIM * (g % 4), rt[b0:b0 + hd], rt[b0 + hd:b0 + 2 * hd],
                ctd_ref[...], std_ref[...], sd)
    sf = HEAD_DIM ** -0.5 * LOG2E
    for p in range(3):
        b0 = 384 + LANES * p
        qtf_ref[p] = (rt[b0:b0 + LANES] * sf).astype(BF16)
    hc = HEAD_DIM // 2
    for h in range(N_DIL_HEADS):
        b0 = 768 + HEAD_DIM * h
        rope_tr(qtc_ref, h // 2, HEAD_DIM * (h % 2), rt[b0:b0 + hc], rt[b0 + hc:b0 + 2 * hc],
                ctc_ref[...], stc_ref[...], sf)

    ones_rows = jnp.ones((BF16_ROWS, tm), BF16)

    def put_vt(ref, base, n_pairs):
        for p in range(n_pairs):
            for h in range(2):
                b0 = base + LANES * p + HEAD_DIM * h
                ref[p, V_ROWS * h:V_ROWS * h + HEAD_DIM, :] = rt[b0:b0 + HEAD_DIM].astype(BF16)
                ref[p, V_ROWS * h + HEAD_DIM:V_ROWS * (h + 1), :] = ones_rows

    put_vt(vtd_ref, 1024, 3)
    put_vt(vtf_ref, 1408, 3)
    put_vt(vtc_ref, 1792, 2)

    z = rn[:, 1024:1024 + LANES] + bf_ref[...]
    logf = (jnp.minimum(z, 0.0) - jnp.log1p(jnp.exp(-jnp.abs(z)))) * LOG2E
    hi, mid, lo = _split3(logf)
    tri = tri_ref[...]
    c = _dot(tri, hi) + _dot(tri, mid) + _dot(tri, lo) + carry_ref[...]
    carry_ref[...] = c[tm - 1:tm, :]
    cq_ref[...] = c.T[0:3 * GATE_STRIDE, :]
    chi, cmid, clo = _split3(c)
    aux = _dot(chi, ehi_ref[...]) + _dot(cmid, emid_ref[...]) + _dot(clo, elo_ref[...]) + aux1_ref[...]
    for p in range(3):
        kf_ref[p, :, 0:LANES] = rn[:, 384 + LANES * p:384 + LANES * (p + 1)].astype(BF16)
        kf_ref[p, :, LANES:2 * LANES] = aux[:, LANES * p:LANES * (p + 1)].astype(BF16)


def _proj_call(x, g_mix, wn, wt, bf_pad, consts, T):
    b, s, d = x.shape
    nt = s // T
    tile5 = lambda npair, r, c: pl.BlockSpec((None, npair, None, r, c), lambda bi, ti: (bi, 0, ti, 0, 0))
    nat_tab = pl.BlockSpec((T, LANES), lambda bi, ti: (ti, 0))
    tr_tab = lambda r: pl.BlockSpec((r, T), lambda bi, ti: (0, ti))
    in_specs = [
        pl.BlockSpec((None, T, d), lambda bi, ti: (bi, ti, 0)),
        _const_spec((1, d)), _const_spec(wn.shape), _const_spec(wt.shape), _const_spec((1, LANES)),
        _const_spec((T, T)), _const_spec((LANES, 384)), _const_spec((LANES, 384)), _const_spec((LANES, 384)),
        _const_spec((1, 384)),
        nat_tab, nat_tab, nat_tab, nat_tab,
        tr_tab(DIFF_QK_DIM // 2), tr_tab(DIFF_QK_DIM // 2), tr_tab(HEAD_DIM // 2), tr_tab(HEAD_DIM // 2),
    ]
    out_shape = [
        jax.ShapeDtypeStruct((b, 3, nt, LANES, T), BF16),
        jax.ShapeDtypeStruct((b, 3, nt, T, LANES), BF16),
        jax.ShapeDtypeStruct((b, 3, nt, 2 * V_ROWS, T), BF16),
        jax.ShapeDtypeStruct((b, 3, nt, LANES, T), BF16),
        jax.ShapeDtypeStruct((b, 3, nt, T, 2 * LANES), BF16),
        jax.ShapeDtypeStruct((b, 3, nt, 2 * V_ROWS, T), BF16),
        jax.ShapeDtypeStruct((b, nt, 3 * GATE_STRIDE, T), F32),
        jax.ShapeDtypeStruct((b, 2, nt, LANES, T), BF16),
        jax.ShapeDtypeStruct((b, 2, nt, T, LANES), BF16),
        jax.ShapeDtypeStruct((b, 2, nt, 2 * V_ROWS, T), BF16),
    ]
    out_specs = [
        tile5(3, LANES, T), tile5(3, T, LANES), tile5(3, 2 * V_ROWS, T),
        tile5(3, LANES, T), tile5(3, T, 2 * LANES), tile5(3, 2 * V_ROWS, T),
        pl.BlockSpec((None, None, 3 * GATE_STRIDE, T), lambda bi, ti: (bi, ti, 0, 0)),
        tile5(2, LANES, T), tile5(2, T, LANES), tile5(2, 2 * V_ROWS, T),
    ]
    return pl.pallas_call(
        _proj_body,
        grid=(b, nt),
        in_specs=in_specs,
        out_specs=out_specs,
        out_shape=out_shape,
        scratch_shapes=[pltpu.VMEM((1, LANES), F32)],
        compiler_params=pltpu.CompilerParams(dimension_semantics=("arbitrary", "arbitrary"),
                                             vmem_limit_bytes=VMEM_LIMIT_BYTES),
        name="proj",
    )(x, g_mix.reshape(1, d), wn, wt, bf_pad, consts["tri"], consts["ehi"], consts["emid"], consts["elo"],
      consts["aux1"], consts["cnd"], consts["snd"], consts["cnc"], consts["snc"],
      consts["ctd"], consts["std"], consts["ctc"], consts["stc"])


def _attn_body(*refs, mode, n_back):
    if mode == "fox":
        qt_ref, k_ref, vt_ref, cq_ref, o_ref, qe_ref, m_ref, acc_ref, s_ref, tmax_ref = refs
    elif mode == "diff":
        qt_ref, k_ref, vt_ref, lam_ref, gcol_ref, o_ref, qe_ref, m_ref, acc_ref, s_ref, tmax_ref = refs
    else:
        qt_ref, k_ref, vt_ref, bias_ref, o_ref, qe_ref, m_ref, acc_ref, s_ref, tmax_ref = refs
    T = qt_ref.shape[1]
    n_sc = qe_ref.shape[0]
    i = pl.program_id(2)

    rows = LANES // n_sc
    zeros_q = jnp.zeros((LANES, T), BF16)
    for sc in range(n_sc):
        qe_ref[sc, 0:LANES, :] = zeros_q
        qe_ref[sc, rows * sc:rows * (sc + 1), :] = qt_ref[rows * sc:rows * (sc + 1), :]
    if mode == "fox":
        cq = cq_ref[...]
        for h in range(2):
            terms = _split3(cq[h:h + 1, :])
            blocks = [jnp.broadcast_to(t.astype(F32), (BF16_ROWS, T)) for t in terms]
            blocks += [jnp.full((BF16_ROWS, T), 1.0 if h == hh else 0.0, F32) for hh in range(2)]
            blocks += [jnp.zeros((LANES - 5 * BF16_ROWS, T), F32)]
            qe_ref[h, LANES:2 * LANES, :] = jnp.concatenate(blocks, axis=0).astype(BF16)

    def produce(sc, j):
        s = _dot(k_ref[j], qe_ref[sc])
        if mode == "dil":
            s = s + bias_ref[i - j]
        s_ref[sc] = s
        tmax_ref[sc] = jnp.max(s, axis=0, keepdims=True)

    def consume(sc, j, diag):
        s = s_ref[sc]
        if diag and mode != "dil":
            kk = lax.broadcasted_iota(jnp.int32, (T, T), 0)
            qq = lax.broadcasted_iota(jnp.int32, (T, T), 1)
            s = jnp.where(kk <= qq, s, NEG_BIG)
            tmax = jnp.max(s, axis=0, keepdims=True)
        else:
            tmax = tmax_ref[sc]
        h = sc * 2 // n_sc
        m_old = m_ref[sc]
        m_new = jnp.maximum(m_old, tmax)
        alpha = jnp.exp2(m_old - m_new)
        p = jnp.exp2(s - m_new).astype(BF16)
        acc_ref[sc] = acc_ref[sc] * alpha + _dot(vt_ref[j, V_ROWS * h:V_ROWS * (h + 1), :], p)
        m_ref[sc] = m_new

    m_ref[...] = jnp.full(m_ref.shape, NEG_BIG, F32)
    acc_ref[...] = jnp.zeros(acc_ref.shape, F32)

    lo = jnp.maximum(i - n_back, 0) if mode == "dil" else 0
    produce(0, lo)

    def step(j, carry):
        for sc in range(n_sc):
            if sc + 1 < n_sc:
                produce(sc + 1, j)
            else:
                produce(0, j + 1)
            consume(sc, j, False)
        return carry

    lax.fori_loop(lo, i, step, 0)
    for sc in range(n_sc):
        if sc + 1 < n_sc:
            produce(sc + 1, i)
        consume(sc, i, True)

    def normalised(sc):
        a = acc_ref[sc]
        return a[0:HEAD_DIM] / a[HEAD_DIM:HEAD_DIM + 1]

    if mode == "diff":
        lam_init = lam_ref[4:5, 0:1]
        lam = (jnp.exp(jnp.sum(lam_ref[0:1] * lam_ref[1:2], axis=1, keepdims=True))
               - jnp.exp(jnp.sum(lam_ref[2:3] * lam_ref[3:4], axis=1, keepdims=True)) + lam_init)
        outs = []
        for h in range(2):
            d = normalised(2 * h) - lam * normalised(2 * h + 1)
            ms = jnp.mean(d * d, axis=0, keepdims=True)
            outs.append(d * lax.rsqrt(ms + SUBLN_EPS) * gcol_ref[...] * (1.0 - lam_init))
    else:
        outs = [normalised(h) for h in range(2)]
    o_ref[...] = jnp.concatenate(outs, axis=0).T.astype(BF16)


def _attn_call(mode, qt, k, vt, extra, T, n_back=0):
    b, npair, nt = qt.shape[:3]
    r = k.shape[-1]
    n_sc = 4 if mode == "diff" else 2
    in_specs = [
        pl.BlockSpec((None, None, None, LANES, T), lambda bi, p, i: (bi, p, i, 0, 0)),
        pl.BlockSpec((None, None, nt, T, r), lambda bi, p, i: (bi, p, 0, 0, 0)),
        pl.BlockSpec((None, None, nt, 2 * V_ROWS, T), lambda bi, p, i: (bi, p, 0, 0, 0)),
    ]
    if mode == "fox":
        in_specs.append(pl.BlockSpec((None, None, GATE_STRIDE, T), lambda bi, p, i: (bi, i, p, 0)))
    elif mode == "diff":
        in_specs += [_const_spec((8, LANES)), _const_spec((HEAD_DIM, 1))]
    else:
        in_specs.append(_const_spec(extra[0].shape))
    body = functools.partial(_attn_body, mode=mode, n_back=n_back)
    return pl.pallas_call(
        body,
        grid=(b, npair, nt),
        in_specs=in_specs,
        out_specs=pl.BlockSpec((None, T, LANES), lambda bi, p, i: (bi, i, p)),
        out_shape=jax.ShapeDtypeStruct((b, nt * T, npair * LANES), BF16),
        scratch_shapes=[pltpu.VMEM((n_sc, r, T), BF16),
                        pltpu.VMEM((n_sc, 1, T), F32),
                        pltpu.VMEM((n_sc, V_ROWS, T), F32),
                        pltpu.VMEM((n_sc, T, T), F32),
                        pltpu.VMEM((n_sc, 1, T), F32)],
        compiler_params=pltpu.CompilerParams(dimension_semantics=("arbitrary", "arbitrary", "arbitrary"),
                                             vmem_limit_bytes=VMEM_LIMIT_BYTES),
        name="attn_" + mode,
    )(qt, k, vt, *extra)


def _constants(s, T):
    pos = np.arange(s, dtype=np.float64)[:, None]

    def angles(half):
        inv = ROPE_THETA ** (-np.arange(half, dtype=np.float64) / half)
        return pos * inv[None, :]

    ad, ac = angles(DIFF_QK_DIM // 2), angles(HEAD_DIM // 2)

    def nat(a):
        reps = LANES // (2 * a.shape[1])
        cos = np.tile(np.concatenate([np.cos(a), np.cos(a)], axis=1), (1, reps))
        sin = np.tile(np.concatenate([-np.sin(a), np.sin(a)], axis=1), (1, reps))
        return cos.astype(np.float32), sin.astype(np.float32)

    cnd, snd = nat(ad)
    cnc, snc = nat(ac)
    consts = dict(cnd=cnd, snd=snd, cnc=cnc, snc=snc,
                  ctd=np.cos(ad).T.astype(np.float32), std=np.sin(ad).T.astype(np.float32),
                  ctc=np.cos(ac).T.astype(np.float32), stc=np.sin(ac).T.astype(np.float32))
    consts["tri"] = np.tril(np.ones((T, T), np.float32))
    e = np.zeros((3, LANES, 384), np.float32)
    for h in range(N_FOX_HEADS):
        src = GATE_STRIDE * (h // 2) + h % 2
        for t in range(3):
            e[t, src, LANES * (h // 2) + 3 * BF16_ROWS + BF16_ROWS * (h % 2) + t] = -1.0
    consts["ehi"], consts["emid"], consts["elo"] = e[0], e[1], e[2]
    aux1 = np.zeros((1, 384), np.float32)
    for p in range(3):
        for t in range(3):
            aux1[0, LANES * p + BF16_ROWS * t] = 1.0
    consts["aux1"] = aux1
    n_back = -(-max(w for w, _ in DILATED_PATTERNS) // T)
    kk = np.arange(T)[:, None]
    qq = np.arange(T)[None, :]
    bias = np.zeros((n_back + 1, T, T), np.float32)
    for o in range(n_back + 1):
        delta = o * T + qq - kk
        mult = np.zeros((T, T), np.float64)
        for w, dil in DILATED_PATTERNS:
            mult += (delta >= 0) & (delta <= w) & (delta % dil == 0)
        bias[o] = np.where(mult > 0, np.log2(np.maximum(mult, 1.0)), NEG_BIG)
    out = {}
    for name, v in consts.items():
        out[name] = jnp.asarray(v, BF16 if name in ("tri", "ehi", "emid", "elo") else F32)
    return out, jnp.asarray(bias), n_back


def _layer_weights(w_in_l, b_f_l):
    d = w_in_l.shape[0]
    sizes = [384] * 6 + [N_FOX_HEADS, 256, 256, 256]
    aq, ak, av, fq, fk, fv, fg, cq, ck, cv = jnp.split(w_in_l, np.cumsum(sizes)[:-1].tolist(), axis=1)
    lanes = np.array([GATE_STRIDE * (h // 2) + h % 2 for h in range(N_FOX_HEADS)])
    fg_pad = jnp.zeros((d, LANES), F32).at[:, lanes].set(fg)
    bf_pad = jnp.zeros((1, LANES), F32).at[0, lanes].set(b_f_l)
    wn = jnp.concatenate([ak, fk, ck, fg_pad], axis=1).astype(BF16)
    wt = jnp.concatenate([aq, fq, cq, av, fv, cv], axis=1).T.astype(BF16)
    return wn, wt, bf_pad


def _forward(x, w_in, b_f, lam_q1, lam_k1, lam_q2, lam_k2, g_sub, w_o, g_ffn1, w1_gate, w1_up, w1_down,
             g_mix, g_ffn2, w2_gate, w2_up, w2_down, g_final, T=512, tm=512):
    b, s, d = x.shape
    depth = w_in.shape[0]
    assert s % T == 0 and (b * s) % tm == 0 and d == HEAD_DIM * (N_DIFF_HEADS + N_FOX_HEADS + N_DIL_HEADS)
    consts, bias, n_back = _constants(s, T)
    x2d = x.reshape(b * s, d)
    for l in range(depth):
        lam_init = 0.8 - 0.6 * math.exp(-0.3 * l)
        if l == 0:
            x2d = _ffn_call(x2d, g_ffn1[l], w1_gate[l].astype(BF16), w1_up[l].astype(BF16),
                            w1_down[l].astype(BF16), tm=tm)
        wn, wt, bf_pad = _layer_weights(w_in[l], b_f[l])
        (qtd, kd, vtd, qtf, kf, vtf, cqf, qtc, kc, vtc) = _proj_call(
            x2d.reshape(b, s, d), g_mix[l], wn, wt, bf_pad, consts, T)
        lam_rows = jnp.zeros((8, LANES), F32)
        lam_rows = lam_rows.at[0:4, 0:DIFF_QK_DIM].set(jnp.stack([lam_q1[l], lam_k1[l], lam_q2[l], lam_k2[l]]))
        lam_rows = lam_rows.at[4, :].set(lam_init)
        o_a = _attn_call("diff", qtd, kd, vtd, (lam_rows, g_sub[l].reshape(HEAD_DIM, 1)), T)
        o_b = _attn_call("fox", qtf, kf, vtf, (cqf,), T)
        o_c = _attn_call("dil", qtc, kc, vtc, (bias,), T, n_back=n_back)
        wo = w_o[l].astype(BF16)
        attn = ([o_a.reshape(b * s, -1), o_b.reshape(b * s, -1), o_c.reshape(b * s, -1)],
                [wo[0:384], wo[384:768], wo[768:1024]])
        last = l == depth - 1
        x2d = _ffn_call(x2d, g_ffn2[l], w2_gate[l].astype(BF16), w2_up[l].astype(BF16), w2_down[l].astype(BF16),
                        attn=attn, g_final=g_final if last else None, tm=tm)
        if not last:
            x2d = _ffn_call(x2d, g_ffn1[l + 1], w1_gate[l + 1].astype(BF16), w1_up[l + 1].astype(BF16),
                            w1_down[l + 1].astype(BF16), tm=tm)
    return x2d.reshape(b, s, d)


def kernel(x, w_in, b_f, lam_q1, lam_k1, lam_q2, lam_k2, g_sub, w_o, g_ffn1, w1_gate, w1_up, w1_down,
           g_mix, g_ffn2, w2_gate, w2_up, w2_down, g_final):
    return _forward(x, w_in, b_f, lam_q1, lam_k1, lam_q2, lam_k2, g_sub, w_o, g_ffn1, w1_gate, w1_up,
                    w1_down, g_mix, g_ffn2, w2_gate, w2_up, w2_down, g_final)
```

```python
import functools
import math

import numpy as np
import jax
import jax.numpy as jnp
from jax import lax
from jax.experimental import pallas as pl
from jax.experimental.pallas import tpu as pltpu

F32 = jnp.float32
BF16 = jnp.bfloat16

HEAD_DIM = 64
N_DIFF_HEADS = 6
N_FOX_HEADS = 6
N_DIL_HEADS = 4
DIFF_QK_DIM = 32
DILATED_PATTERNS = ((128, 1), (512, 4), (2048, 16))
ROPE_THETA = 10000.0
NORM_EPS = 1e-6
SUBLN_EPS = 1e-5
MACARON_SCALE = 0.5
NEG_BIG = -1e30
LOG2E = 1.4426950408889634

LANES = 128
BF16_ROWS = 16
V_ROWS = HEAD_DIM + BF16_ROWS
GATE_STRIDE = 8
VMEM_LIMIT_BYTES = 56 * 1024 * 1024


def _dot(a, b):
    return jnp.dot(a, b, preferred_element_type=F32)


def _dot_nt(a, b):
    return lax.dot_general(a, b, (((1,), (1,)), ((), ())), preferred_element_type=F32)


def _rms(x, g, eps):
    ms = jnp.mean(x * x, axis=-1, keepdims=True)
    return x * lax.rsqrt(ms + eps) * g


def _split3(x):
    hi = x.astype(BF16)
    r1 = x - hi.astype(F32)
    mid = r1.astype(BF16)
    lo = (r1 - mid.astype(F32)).astype(BF16)
    return hi, mid, lo


def _ffn_body(*refs, has_attn, has_final, n_chunks):
    refs = list(refs)
    x_ref = refs.pop(0)
    if has_attn:
        oa_ref, ob_ref, oc_ref, woa_ref, wob_ref, woc_ref = refs[:6]
        refs = refs[6:]
    g_ref, wg_ref, wu_ref, wd_ref = refs[:4]
    refs = refs[4:]
    if has_final:
        gf_ref = refs.pop(0)
    out_ref = refs[0]

    x = x_ref[...]
    if has_attn:
        x = x + (_dot(oa_ref[...], woa_ref[...]) + _dot(ob_ref[...], wob_ref[...])
                 + _dot(oc_ref[...], woc_ref[...]))
    xn = _rms(x, g_ref[...], NORM_EPS).astype(BF16)
    ck = wg_ref.shape[1] // n_chunks
    acc = None
    for c in range(n_chunks):
        sl = slice(c * ck, (c + 1) * ck)
        gate = _dot(xn, wg_ref[:, sl])
        up = _dot(xn, wu_ref[:, sl])
        h = (gate * jax.nn.sigmoid(gate) * up).astype(BF16)
        d = _dot(h, wd_ref[sl, :])
        acc = d if acc is None else acc + d
    y = x + MACARON_SCALE * acc
    if has_final:
        y = _rms(y, gf_ref[...], NORM_EPS)
    out_ref[...] = y


def _const_spec(shape):
    nd = len(shape)
    return pl.BlockSpec(shape, lambda *_: (0,) * nd, pipeline_mode=pl.Buffered(1))


def _ffn_call(x2d, g, wg, wu, wd, attn=None, g_final=None, tm=512):
    n, d = x2d.shape
    dff = wg.shape[1]
    args = [x2d]
    specs = [pl.BlockSpec((tm, d), lambda i: (i, 0))]
    if attn is not None:
        o_list, w_list = attn
        for o in o_list:
            args.append(o)
            specs.append(pl.BlockSpec((tm, o.shape[1]), lambda i: (i, 0)))
        for w in w_list:
            args.append(w)
            specs.append(_const_spec(w.shape))
    args += [g.reshape(1, d), wg, wu, wd]
    specs += [_const_spec((1, d)), _const_spec((d, dff)), _const_spec((d, dff)), _const_spec((dff, d))]
    if g_final is not None:
        args.append(g_final.reshape(1, d))
        specs.append(_const_spec((1, d)))
    n_chunks = 2 if dff % (2 * LANES) == 0 else 1
    body = functools.partial(_ffn_body, has_attn=attn is not None, has_final=g_final is not None,
                             n_chunks=n_chunks)
    return pl.pallas_call(
        body,
        grid=(n // tm,),
        in_specs=specs,
        out_specs=pl.BlockSpec((tm, d), lambda i: (i, 0)),
        out_shape=jax.ShapeDtypeStruct((n, d), F32),
        compiler_params=pltpu.CompilerParams(dimension_semantics=("arbitrary",),
                                             vmem_limit_bytes=VMEM_LIMIT_BYTES),
        name="ffn",
    )(*args)


N_NAT = 384 + 384 + 256 + LANES
N_TR = 2 * (384 + 384 + 256)


def _proj_body(x_ref, g_ref, wn_ref, wt_ref, bf_ref, tri_ref, ehi_ref, emid_ref, elo_ref, aux1_ref,
               cnd_ref, snd_ref, cnc_ref, snc_ref, ctd_ref, std_ref, ctc_ref, stc_ref,
               qtd_ref, kd_ref, vtd_ref, qtf_ref, kf_ref, vtf_ref, cq_ref, qtc_ref, kc_ref, vtc_ref,
               carry_ref):
    tm = x_ref.shape[0]

    @pl.when(pl.program_id(1) == 0)
    def _():
        carry_ref[...] = jnp.zeros_like(carry_ref)

    xn = _rms(x_ref[...], g_ref[...], NORM_EPS).astype(BF16)
    rn = _dot(xn, wn_ref[...])
    rt = _dot_nt(wt_ref[...], xn)

    lane = lax.broadcasted_iota(jnp.int32, (tm, LANES), 1)

    def rope_nat(blk, half, cos, sin_signed):
        first = (lane & half) == 0
        rot = jnp.where(first, pltpu.roll(blk, LANES - half, 1), pltpu.roll(blk, half, 1))
        return (blk * cos + rot * sin_signed).astype(BF16)

    for p in range(3):
        kd_ref[p] = rope_nat(rn[:, LANES * p:LANES * (p + 1)], DIFF_QK_DIM // 2, cnd_ref[...], snd_ref[...])
    for p in range(2):
        c0 = 768 + LANES * p
        kc_ref[p] = rope_nat(rn[:, c0:c0 + LANES], HEAD_DIM // 2, cnc_ref[...], snc_ref[...])

    def rope_tr(ref, p, r0, x1, x2, cos, sin, scale):
        half = x1.shape[0]
        ref[p, r0:r0 + half, :] = ((x1 * cos - x2 * sin) * scale).astype(BF16)
        ref[p, r0 + half:r0 + 2 * half, :] = ((x2 * cos + x1 * sin) * scale).astype(BF16)

    sd = DIFF_QK_DIM ** -0.5 * LOG2E
    hd = DIFF_QK_DIM // 2
    for g in range(2 * N_DIFF_HEADS):
        b0 = DIFF_QK_DIM * g
        rope_tr(qtd_ref, g // 4, DIFF_QK_DIM * (g % 4), rt[b0:b0 + hd], rt[b0 + hd:b0 + 2 * hd],
                ctd_ref[...], std_ref[...], sd)
    sf = HEAD_DIM ** -0.5 * LOG2E
    for p in range(3):
        b0 = 384 + LANES * p
        qtf_ref[p] = (rt[b0:b0 + LANES] * sf).astype(BF16)
    hc = HEAD_DIM // 2
    for h in range(N_DIL_HEADS):
        b0 = 768 + HEAD_DIM * h
        rope_tr(qtc_ref, h // 2, HEAD_DIM * (h % 2), rt[b0:b0 + hc], rt[b0 + hc:b0 + 2 * hc],
                ctc_ref[...], stc_ref[...], sf)

    ones_rows = jnp.ones((BF16_ROWS, tm), BF16)

    def put_vt(ref, base, n_pairs):
        for p in range(n_pairs):
            for h in range(2):
                b0 = base + LANES * p + HEAD_DIM * h
                ref[p, V_ROWS * h:V_ROWS * h + HEAD_DIM, :] = rt[b0:b0 + HEAD_DIM].astype(BF16)
                ref[p, V_ROWS * h + HEAD_DIM:V_ROWS * (h + 1), :] = ones_rows

    put_vt(vtd_ref, 1024, 3)
    put_vt(vtf_ref, 1408, 3)
    put_vt(vtc_ref, 1792, 2)

    z = rn[:, 1024:1024 + LANES] + bf_ref[...]
    logf = (jnp.minimum(z, 0.0) - jnp.log1p(jnp.exp(-jnp.abs(z)))) * LOG2E
    hi, mid, lo = _split3(logf)
    tri = tri_ref[...]
    c = _dot(tri, hi) + _dot(tri, mid) + _dot(tri, lo) + carry_ref[...]
    carry_ref[...] = c[tm - 1:tm, :]
    cq_ref[...] = c.T[0:3 * GATE_STRIDE, :]
    chi, cmid, clo = _split3(c)
    aux = _dot(chi, ehi_ref[...]) + _dot(cmid, emid_ref[...]) + _dot(clo, elo_ref[...]) + aux1_ref[...]
    for p in range(3):
        kf_ref[p, :, 0:LANES] = rn[:, 384 + LANES * p:384 + LANES * (p + 1)].astype(BF16)
        kf_ref[p, :, LANES:2 * LANES] = aux[:, LANES * p:LANES * (p + 1)].astype(BF16)


def _proj_call(x, g_mix, wn, wt, bf_pad, consts, T):
    b, s, d = x.shape
    nt = s // T
    tile5 = lambda npair, r, c: pl.BlockSpec((None, npair, None, r, c), lambda bi, ti: (bi, 0, ti, 0, 0))
    nat_tab = pl.BlockSpec((T, LANES), lambda bi, ti: (ti, 0))
    tr_tab = lambda r: pl.BlockSpec((r, T), lambda bi, ti: (0, ti))
    in_specs = [
        pl.BlockSpec((None, T, d), lambda bi, ti: (bi, ti, 0)),
        _const_spec((1, d)), _const_spec(wn.shape), _const_spec(wt.shape), _const_spec((1, LANES)),
        _const_spec((T, T)), _const_spec((LANES, 384)), _const_spec((LANES, 384)), _const_spec((LANES, 384)),
        _const_spec((1, 384)),
        nat_tab, nat_tab, nat_tab, nat_tab,
        tr_tab(DIFF_QK_DIM // 2), tr_tab(DIFF_QK_DIM // 2), tr_tab(HEAD_DIM // 2), tr_tab(HEAD_DIM // 2),
    ]
    out_shape = [
        jax.ShapeDtypeStruct((b, 3, nt, LANES, T), BF16),
        jax.ShapeDtypeStruct((b, 3, nt, T, LANES), BF16),
        jax.ShapeDtypeStruct((b, 3, nt, 2 * V_ROWS, T), BF16),
        jax.ShapeDtypeStruct((b, 3, nt, LANES, T), BF16),
        jax.ShapeDtypeStruct((b, 3, nt, T, 2 * LANES), BF16),
        jax.ShapeDtypeStruct((b, 3, nt, 2 * V_ROWS, T), BF16),
        jax.ShapeDtypeStruct((b, nt, 3 * GATE_STRIDE, T), F32),
        jax.ShapeDtypeStruct((b, 2, nt, LANES, T), BF16),
        jax.ShapeDtypeStruct((b, 2, nt, T, LANES), BF16),
        jax.ShapeDtypeStruct((b, 2, nt, 2 * V_ROWS, T), BF16),
    ]
    out_specs = [
        tile5(3, LANES, T), tile5(3, T, LANES), tile5(3, 2 * V_ROWS, T),
        tile5(3, LANES, T), tile5(3, T, 2 * LANES), tile5(3, 2 * V_ROWS, T),
        pl.BlockSpec((None, None, 3 * GATE_STRIDE, T), lambda bi, ti: (bi, ti, 0, 0)),
        tile5(2, LANES, T), tile5(2, T, LANES), tile5(2, 2 * V_ROWS, T),
    ]
    return pl.pallas_call(
        _proj_body,
        grid=(b, nt),
        in_specs=in_specs,
        out_specs=out_specs,
        out_shape=out_shape,
        scratch_shapes=[pltpu.VMEM((1, LANES), F32)],
        compiler_params=pltpu.CompilerParams(dimension_semantics=("arbitrary", "arbitrary"),
                                             vmem_limit_bytes=VMEM_LIMIT_BYTES),
        name="proj",
    )(x, g_mix.reshape(1, d), wn, wt, bf_pad, consts["tri"], consts["ehi"], consts["emid"], consts["elo"],
      consts["aux1"], consts["cnd"], consts["snd"], consts["cnc"], consts["snc"],
      consts["ctd"], consts["std"], consts["ctc"], consts["stc"])


def _attn_body(*refs, mode, n_back):
    if mode == "fox":
        qt_ref, k_ref, vt_ref, cq_ref, o_ref, qe_ref, m_ref, acc_ref, s_ref, tmax_ref = refs
    elif mode == "diff":
        qt_ref, k_ref, vt_ref, lam_ref, gcol_ref, o_ref, qe_ref, m_ref, acc_ref, s_ref, tmax_ref = refs
    else:
        qt_ref, k_ref, vt_ref, bias_ref, o_ref, qe_ref, m_ref, acc_ref, s_ref, tmax_ref = refs
    n_pairs, _, T = qt_ref.shape
    n_sc = qe_ref.shape[0]
    per_pair = n_sc // n_pairs
    i = pl.program_id(2)

    rows = LANES // per_pair
    zeros_q = jnp.zeros((LANES, T), BF16)
    for sc in range(n_sc):
        p, r = divmod(sc, per_pair)
        qe_ref[sc, 0:LANES, :] = zeros_q
        qe_ref[sc, rows * r:rows * (r + 1), :] = qt_ref[p, rows * r:rows * (r + 1), :]
    if mode == "fox":
        for sc in range(n_sc):
            p, h = divmod(sc, 2)
            terms = _split3(cq_ref[p, h:h + 1, :])
            blocks = [jnp.broadcast_to(t.astype(F32), (BF16_ROWS, T)) for t in terms]
            blocks += [jnp.full((BF16_ROWS, T), 1.0 if h == hh else 0.0, F32) for hh in range(2)]
            blocks += [jnp.zeros((LANES - 5 * BF16_ROWS, T), F32)]
            qe_ref[sc, LANES:2 * LANES, :] = jnp.concatenate(blocks, axis=0).astype(BF16)

    def produce(sc, j):
        s = _dot(k_ref[sc // per_pair, j], qe_ref[sc])
        if mode == "dil":
            s = s + bias_ref[i - j]
        s_ref[sc] = s
        tmax_ref[sc] = jnp.max(s, axis=0, keepdims=True)

    def consume(sc, j, diag):
        s = s_ref[sc]
        if diag and mode != "dil":
            kk = lax.broadcasted_iota(jnp.int32, (T, T), 0)
            qq = lax.broadcasted_iota(jnp.int32, (T, T), 1)
            s = jnp.where(kk <= qq, s, NEG_BIG)
            tmax = jnp.max(s, axis=0, keepdims=True)
        else:
            tmax = tmax_ref[sc]
        p, r = divmod(sc, per_pair)
        h = r * 2 // per_pair
        m_old = m_ref[sc]
        m_new = jnp.maximum(m_old, tmax)
        alpha = jnp.exp2(m_old - m_new)
        pr = jnp.exp2(s - m_new).astype(BF16)
        acc_ref[sc] = acc_ref[sc] * alpha + _dot(vt_ref[p, j, V_ROWS * h:V_ROWS * (h + 1), :], pr)
        m_ref[sc] = m_new

    m_ref[...] = jnp.full(m_ref.shape, NEG_BIG, F32)
    acc_ref[...] = jnp.zeros(acc_ref.shape, F32)

    lo = jnp.maximum(i - n_back, 0) if mode == "dil" else 0
    produce(0, lo)

    def step(j, carry):
        for sc in range(n_sc):
            if sc + 1 < n_sc:
                produce(sc + 1, j)
            else:
                produce(0, j + 1)
            consume(sc, j, False)
        return carry

    lax.fori_loop(lo, i, step, 0)
    for sc in range(n_sc):
        if sc + 1 < n_sc:
            produce(sc + 1, i)
        consume(sc, i, True)

    def normalised(sc):
        a = acc_ref[sc]
        return a[0:HEAD_DIM] / a[HEAD_DIM:HEAD_DIM + 1]

    if mode == "diff":
        lam_init = lam_ref[4:5, 0:1]
        lam = (jnp.exp(jnp.sum(lam_ref[0:1] * lam_ref[1:2], axis=1, keepdims=True))
               - jnp.exp(jnp.sum(lam_ref[2:3] * lam_ref[3:4], axis=1, keepdims=True)) + lam_init)
    for p in range(n_pairs):
        outs = []
        for h in range(2):
            if mode == "diff":
                sc = per_pair * p + 2 * h
                d = normalised(sc) - lam * normalised(sc + 1)
                ms = jnp.mean(d * d, axis=0, keepdims=True)
                outs.append(d * lax.rsqrt(ms + SUBLN_EPS) * gcol_ref[...] * (1.0 - lam_init))
            else:
                outs.append(normalised(per_pair * p + h))
        o_ref[:, LANES * p:LANES * (p + 1)] = jnp.concatenate(outs, axis=0).T.astype(BF16)


def _attn_call(mode, qt, k, vt, extra, T, n_back=0, pairs_per_step=None):
    b, npair, nt = qt.shape[:3]
    pps = npair if pairs_per_step is None else pairs_per_step
    r = k.shape[-1]
    n_sc = (4 if mode == "diff" else 2) * pps
    in_specs = [
        pl.BlockSpec((None, pps, None, LANES, T), lambda bi, p, i: (bi, p, i, 0, 0)),
        pl.BlockSpec((None, pps, nt, T, r), lambda bi, p, i: (bi, p, 0, 0, 0)),
        pl.BlockSpec((None, pps, nt, 2 * V_ROWS, T), lambda bi, p, i: (bi, p, 0, 0, 0)),
    ]
    if mode == "fox":
        in_specs.append(pl.BlockSpec((None, None, pps, GATE_STRIDE, T), lambda bi, p, i: (bi, i, p, 0, 0)))
    elif mode == "diff":
        in_specs += [_const_spec((8, LANES)), _const_spec((HEAD_DIM, 1))]
    else:
        in_specs.append(_const_spec(extra[0].shape))
    body = functools.partial(_attn_body, mode=mode, n_back=n_back)
    return pl.pallas_call(
        body,
        grid=(b, npair // pps, nt),
        in_specs=in_specs,
        out_specs=pl.BlockSpec((None, T, pps * LANES), lambda bi, p, i: (bi, i, p)),
        out_shape=jax.ShapeDtypeStruct((b, nt * T, npair * LANES), BF16),
        scratch_shapes=[pltpu.VMEM((n_sc, r, T), BF16),
                        pltpu.VMEM((n_sc, 1, T), F32),
                        pltpu.VMEM((n_sc, V_ROWS, T), F32),
                        pltpu.VMEM((n_sc, T, T), F32),
                        pltpu.VMEM((n_sc, 1, T), F32)],
        compiler_params=pltpu.CompilerParams(dimension_semantics=("arbitrary", "arbitrary", "arbitrary"),
                                             vmem_limit_bytes=VMEM_LIMIT_BYTES),
        name="attn_" + mode,
    )(qt, k, vt, *extra)


def _constants(s, T):
    pos = np.arange(s, dtype=np.float64)[:, None]

    def angles(half):
        inv = ROPE_THETA ** (-np.arange(half, dtype=np.float64) / half)
        return pos * inv[None, :]

    ad, ac = angles(DIFF_QK_DIM // 2), angles(HEAD_DIM // 2)

    def nat(a):
        reps = LANES // (2 * a.shape[1])
        cos = np.tile(np.concatenate([np.cos(a), np.cos(a)], axis=1), (1, reps))
        sin = np.tile(np.concatenate([-np.sin(a), np.sin(a)], axis=1), (1, reps))
        return cos.astype(np.float32), sin.astype(np.float32)

    cnd, snd = nat(ad)
    cnc, snc = nat(ac)
    consts = dict(cnd=cnd, snd=snd, cnc=cnc, snc=snc,
                  ctd=np.cos(ad).T.astype(np.float32), std=np.sin(ad).T.astype(np.float32),
                  ctc=np.cos(ac).T.astype(np.float32), stc=np.sin(ac).T.astype(np.float32))
    consts["tri"] = np.tril(np.ones((T, T), np.float32))
    e = np.zeros((3, LANES, 384), np.float32)
    for h in range(N_FOX_HEADS):
        src = GATE_STRIDE * (h // 2) + h % 2
        for t in range(3):
            e[t, src, LANES * (h // 2) + 3 * BF16_ROWS + BF16_ROWS * (h % 2) + t] = -1.0
    consts["ehi"], consts["emid"], consts["elo"] = e[0], e[1], e[2]
    aux1 = np.zeros((1, 384), np.float32)
    for p in range(3):
        for t in range(3):
            aux1[0, LANES * p + BF16_ROWS * t] = 1.0
    consts["aux1"] = aux1
    n_back = -(-max(w for w, _ in DILATED_PATTERNS) // T)
    kk = np.arange(T)[:, None]
    qq = np.arange(T)[None, :]
    bias = np.zeros((n_back + 1, T, T), np.float32)
    for o in range(n_back + 1):
        delta = o * T + qq - kk
        mult = np.zeros((T, T), np.float64)
        for w, dil in DILATED_PATTERNS:
            mult += (delta >= 0) & (delta <= w) & (delta % dil == 0)
        bias[o] = np.where(mult > 0, np.log2(np.maximum(mult, 1.0)), NEG_BIG)
    out = {}
    for name, v in consts.items():
        out[name] = jnp.asarray(v, BF16 if name in ("tri", "ehi", "emid", "elo") else F32)
    return out, jnp.asarray(bias), n_back


def _layer_weights(w_in_l, b_f_l):
    d = w_in_l.shape[0]
    sizes = [384] * 6 + [N_FOX_HEADS, 256, 256, 256]
    aq, ak, av, fq, fk, fv, fg, cq, ck, cv = jnp.split(w_in_l, np.cumsum(sizes)[:-1].tolist(), axis=1)
    lanes = np.array([GATE_STRIDE * (h // 2) + h % 2 for h in range(N_FOX_HEADS)])
    fg_pad = jnp.zeros((d, LANES), F32).at[:, lanes].set(fg)
    bf_pad = jnp.zeros((1, LANES), F32).at[0, lanes].set(b_f_l)
    wn = jnp.concatenate([ak, fk, ck, fg_pad], axis=1).astype(BF16)
    wt = jnp.concatenate([aq, fq, cq, av, fv, cv], axis=1).T.astype(BF16)
    return wn, wt, bf_pad


def _forward(x, w_in, b_f, lam_q1, lam_k1, lam_q2, lam_k2, g_sub, w_o, g_ffn1, w1_gate, w1_up, w1_down,
             g_mix, g_ffn2, w2_gate, w2_up, w2_down, g_final, T=512, tm=512):
    b, s, d = x.shape
    depth = w_in.shape[0]
    assert s % T == 0 and (b * s) % tm == 0 and d == HEAD_DIM * (N_DIFF_HEADS + N_FOX_HEADS + N_DIL_HEADS)
    consts, bias, n_back = _constants(s, T)
    x2d = x.reshape(b * s, d)
    for l in range(depth):
        lam_init = 0.8 - 0.6 * math.exp(-0.3 * l)
        if l == 0:
            x2d = _ffn_call(x2d, g_ffn1[l], w1_gate[l].astype(BF16), w1_up[l].astype(BF16),
                            w1_down[l].astype(BF16), tm=tm)
        wn, wt, bf_pad = _layer_weights(w_in[l], b_f[l])
        (qtd, kd, vtd, qtf, kf, vtf, cqf, qtc, kc, vtc) = _proj_call(
            x2d.reshape(b, s, d), g_mix[l], wn, wt, bf_pad, consts, T)
        lam_rows = jnp.zeros((8, LANES), F32)
        lam_rows = lam_rows.at[0:4, 0:DIFF_QK_DIM].set(jnp.stack([lam_q1[l], lam_k1[l], lam_q2[l], lam_k2[l]]))
        lam_rows = lam_rows.at[4, :].set(lam_init)
        o_a = _attn_call("diff", qtd, kd, vtd, (lam_rows, g_sub[l].reshape(HEAD_DIM, 1)), T)
        o_b = _attn_call("fox", qtf, kf, vtf, (cqf.reshape(b, s // T, 3, GATE_STRIDE, T),), T)
        o_c = _attn_call("dil", qtc, kc, vtc, (bias,), T, n_back=n_back)
        wo = w_o[l].astype(BF16)
        attn = ([o_a.reshape(b * s, -1), o_b.reshape(b * s, -1), o_c.reshape(b * s, -1)],
                [wo[0:384], wo[384:768], wo[768:1024]])
        last = l == depth - 1
        x2d = _ffn_call(x2d, g_ffn2[l], w2_gate[l].astype(BF16), w2_up[l].astype(BF16), w2_down[l].astype(BF16),
                        attn=attn, g_final=g_final if last else None, tm=tm)
        if not last:
            x2d = _ffn_call(x2d, g_ffn1[l + 1], w1_gate[l + 1].astype(BF16), w1_up[l + 1].astype(BF16),
                            w1_down[l + 1].astype(BF16), tm=tm)
    return x2d.reshape(b, s, d)


def kernel(x, w_in, b_f, lam_q1, lam_k1, lam_q2, lam_k2, g_sub, w_o, g_ffn1, w1_gate, w1_up, w1_down,
           g_mix, g_ffn2, w2_gate, w2_up, w2_down, g_final):
    return _forward(x, w_in, b_f, lam_q1, lam_k1, lam_q2, lam_k2, g_sub, w_o, g_ffn1, w1_gate, w1_up,
                    w1_down, g_mix, g_ffn2, w2_gate, w2_up, w2_down, g_final)
```

```python
import functools
import math

import numpy as np
import jax
import jax.numpy as jnp
from jax import lax
from jax.experimental import pallas as pl
from jax.experimental.pallas import tpu as pltpu

F32 = jnp.float32
BF16 = jnp.bfloat16

HEAD_DIM = 64
N_DIFF_HEADS = 6
N_FOX_HEADS = 6
N_DIL_HEADS = 4
DIFF_QK_DIM = 32
DILATED_PATTERNS = ((128, 1), (512, 4), (2048, 16))
ROPE_THETA = 10000.0
NORM_EPS = 1e-6
SUBLN_EPS = 1e-5
MACARON_SCALE = 0.5
NEG_BIG = -1e30
LOG2E = 1.4426950408889634

LANES = 128
MXU_WIDTH = 256
BF16_ROWS = 16
V_ROWS = HEAD_DIM + BF16_ROWS
GATE_STRIDE = 8
VMEM_LIMIT_BYTES = 56 * 1024 * 1024


def _dot(a, b):
    return jnp.dot(a, b, preferred_element_type=F32)


def _dot_nt(a, b):
    return lax.dot_general(a, b, (((1,), (1,)), ((), ())), preferred_element_type=F32)


def _rms(x, g, eps):
    ms = jnp.mean(x * x, axis=-1, keepdims=True)
    return x * lax.rsqrt(ms + eps) * g


def _split3(x):
    hi = x.astype(BF16)
    r1 = x - hi.astype(F32)
    mid = r1.astype(BF16)
    lo = (r1 - mid.astype(F32)).astype(BF16)
    return hi, mid, lo


def _ffn_body(*refs, has_attn, has_final, n_chunks):
    refs = list(refs)
    x_ref = refs.pop(0)
    if has_attn:
        oa_ref, ob_ref, oc_ref, woa_ref, wob_ref, woc_ref = refs[:6]
        refs = refs[6:]
    g_ref, wg_ref, wu_ref, wd_ref = refs[:4]
    refs = refs[4:]
    if has_final:
        gf_ref = refs.pop(0)
    out_ref = refs[0]

    x = x_ref[...]
    if has_attn:
        x = x + (_dot(oa_ref[...], woa_ref[...]) + _dot(ob_ref[...], wob_ref[...])
                 + _dot(oc_ref[...], woc_ref[...]))
    xn = _rms(x, g_ref[...], NORM_EPS).astype(BF16)
    n_wide = wg_ref.shape[1] // MXU_WIDTH
    bounds = [MXU_WIDTH * ((n_wide * c + n_chunks - 1) // n_chunks) for c in range(n_chunks + 1)]
    acc = None
    for c in range(n_chunks):
        sl = slice(bounds[c], bounds[c + 1])
        gate = _dot(xn, wg_ref[:, sl])
        up = _dot(xn, wu_ref[:, sl])
        h = (gate * jax.nn.sigmoid(gate) * up).astype(BF16)
        d = _dot(h, wd_ref[sl, :])
        acc = d if acc is None else acc + d
    y = x + MACARON_SCALE * acc
    if has_final:
        y = _rms(y, gf_ref[...], NORM_EPS)
    out_ref[...] = y


def _const_spec(shape):
    nd = len(shape)
    return pl.BlockSpec(shape, lambda *_: (0,) * nd, pipeline_mode=pl.Buffered(1))


def _ffn_call(x2d, g, wg, wu, wd, attn=None, g_final=None, tm=512):
    n, d = x2d.shape
    dff = wg.shape[1]
    args = [x2d]
    specs = [pl.BlockSpec((tm, d), lambda i: (i, 0))]
    if attn is not None:
        o_list, w_list = attn
        for o in o_list:
            args.append(o)
            specs.append(pl.BlockSpec((tm, o.shape[1]), lambda i: (i, 0)))
        for w in w_list:
            args.append(w)
            specs.append(_const_spec(w.shape))
    args += [g.reshape(1, d), wg, wu, wd]
    specs += [_const_spec((1, d)), _const_spec((d, dff)), _const_spec((d, dff)), _const_spec((dff, d))]
    if g_final is not None:
        args.append(g_final.reshape(1, d))
        specs.append(_const_spec((1, d)))
    assert dff % MXU_WIDTH == 0
    n_chunks = 2
    body = functools.partial(_ffn_body, has_attn=attn is not None, has_final=g_final is not None,
                             n_chunks=n_chunks)
    return pl.pallas_call(
        body,
        grid=(n // tm,),
        in_specs=specs,
        out_specs=pl.BlockSpec((tm, d), lambda i: (i, 0)),
        out_shape=jax.ShapeDtypeStruct((n, d), F32),
        compiler_params=pltpu.CompilerParams(dimension_semantics=("arbitrary",),
                                             vmem_limit_bytes=VMEM_LIMIT_BYTES),
        name="ffn",
    )(*args)


N_NAT = 384 + 384 + 256
N_TR = 2 * (384 + 384 + 256)
GATE_ROWS = 32


def _proj_body(x_ref, g_ref, wn_ref, wt_ref, bf_ref, tri_ref,
               cnd_ref, snd_ref, cnc_ref, snc_ref, ctd_ref, std_ref, ctc_ref, stc_ref,
               qtd_ref, kd_ref, vtd_ref, qtf_ref, kf_ref, vtf_ref, cq_ref, qtc_ref, kc_ref, vtc_ref,
               carry_ref):
    tm = x_ref.shape[0]

    @pl.when(pl.program_id(1) == 0)
    def _():
        carry_ref[...] = jnp.zeros_like(carry_ref)

    xn = _rms(x_ref[...], g_ref[...], NORM_EPS).astype(BF16)
    rn = _dot(xn, wn_ref[...])
    rt = _dot_nt(wt_ref[...], xn)

    lane = lax.broadcasted_iota(jnp.int32, (tm, LANES), 1)

    def rope_nat(blk, half, cos, sin_signed):
        first = (lane & half) == 0
        rot = jnp.where(first, pltpu.roll(blk, LANES - half, 1), pltpu.roll(blk, half, 1))
        return (blk * cos + rot * sin_signed).astype(BF16)

    for p in range(3):
        kd_ref[p] = rope_nat(rn[:, LANES * p:LANES * (p + 1)], DIFF_QK_DIM // 2, cnd_ref[...], snd_ref[...])
    for p in range(2):
        c0 = 768 + LANES * p
        kc_ref[p] = rope_nat(rn[:, c0:c0 + LANES], HEAD_DIM // 2, cnc_ref[...], snc_ref[...])

    def rope_tr(ref, p, r0, x1, x2, cos, sin, scale):
        half = x1.shape[0]
        ref[p, r0:r0 + half, :] = ((x1 * cos - x2 * sin) * scale).astype(BF16)
        ref[p, r0 + half:r0 + 2 * half, :] = ((x2 * cos + x1 * sin) * scale).astype(BF16)

    sd = DIFF_QK_DIM ** -0.5 * LOG2E
    hd = DIFF_QK_DIM // 2
    for g in range(2 * N_DIFF_HEADS):
        b0 = DIFF_QK_DIM * g
        rope_tr(qtd_ref, g // 4, DIFF_QK_DIM * (g % 4), rt[b0:b0 + hd], rt[b0 + hd:b0 + 2 * hd],
                ctd_ref[...], std_ref[...], sd)
    sf = HEAD_DIM ** -0.5 * LOG2E
    for p in range(3):
        b0 = 384 + LANES * p
        qtf_ref[p] = (rt[b0:b0 + LANES] * sf).astype(BF16)
    hc = HEAD_DIM // 2
    for h in range(N_DIL_HEADS):
        b0 = 768 + HEAD_DIM * h
        rope_tr(qtc_ref, h // 2, HEAD_DIM * (h % 2), rt[b0:b0 + hc], rt[b0 + hc:b0 + 2 * hc],
                ctc_ref[...], stc_ref[...], sf)

    ones_rows = jnp.ones((BF16_ROWS, tm), BF16)

    def put_vt(ref, base, n_pairs):
        for p in range(n_pairs):
            for h in range(2):
                b0 = base + LANES * p + HEAD_DIM * h
                ref[p, V_ROWS * h:V_ROWS * h + HEAD_DIM, :] = rt[b0:b0 + HEAD_DIM].astype(BF16)
                ref[p, V_ROWS * h + HEAD_DIM:V_ROWS * (h + 1), :] = ones_rows

    put_vt(vtd_ref, 1024, 3)
    put_vt(vtf_ref, 1408, 3)
    put_vt(vtc_ref, 1792, 2)

    z = rt[N_TR:N_TR + GATE_ROWS] + bf_ref[...]
    logf = (jnp.minimum(z, 0.0) - jnp.log1p(jnp.exp(-jnp.abs(z)))) * LOG2E
    part = _dot(jnp.concatenate(_split3(logf), axis=0), tri_ref[...])
    c = part[0:GATE_ROWS] + part[GATE_ROWS:2 * GATE_ROWS] + part[2 * GATE_ROWS:3 * GATE_ROWS] + carry_ref[...]
    carry_ref[...] = c[:, tm - 1:tm]
    cq_ref[...] = c[0:3 * GATE_STRIDE]
    row = lax.broadcasted_iota(jnp.int32, (LANES, tm), 0)
    ones_pattern = jnp.where((row < 3 * BF16_ROWS) & ((row & (BF16_ROWS - 1)) == 0), 1.0, 0.0)
    for p in range(3):
        kf_ref[p, :, 0:LANES] = rn[:, 384 + LANES * p:384 + LANES * (p + 1)].astype(BF16)
        auxt = ones_pattern
        for h in range(2):
            ch = c[GATE_STRIDE * p + h:GATE_STRIDE * p + h + 1]
            for t, term in enumerate(_split3(ch)):
                auxt = jnp.where(row == (3 + h) * BF16_ROWS + t, -term.astype(F32), auxt)
        kf_ref[p, :, LANES:2 * LANES] = auxt.T.astype(BF16)


def _proj_call(x, g_mix, wn, wt, bf_pad, consts, T):
    b, s, d = x.shape
    nt = s // T
    tile5 = lambda npair, r, c: pl.BlockSpec((None, npair, None, r, c), lambda bi, ti: (bi, 0, ti, 0, 0))
    nat_tab = pl.BlockSpec((T, LANES), lambda bi, ti: (ti, 0))
    tr_tab = lambda r: pl.BlockSpec((r, T), lambda bi, ti: (0, ti))
    in_specs = [
        pl.BlockSpec((None, T, d), lambda bi, ti: (bi, ti, 0)),
        _const_spec((1, d)), _const_spec(wn.shape), _const_spec(wt.shape), _const_spec((GATE_ROWS, 1)),
        _const_spec((T, T)),
        nat_tab, nat_tab, nat_tab, nat_tab,
        tr_tab(DIFF_QK_DIM // 2), tr_tab(DIFF_QK_DIM // 2), tr_tab(HEAD_DIM // 2), tr_tab(HEAD_DIM // 2),
    ]
    out_shape = [
        jax.ShapeDtypeStruct((b, 3, nt, LANES, T), BF16),
        jax.ShapeDtypeStruct((b, 3, nt, T, LANES), BF16),
        jax.ShapeDtypeStruct((b, 3, nt, 2 * V_ROWS, T), BF16),
        jax.ShapeDtypeStruct((b, 3, nt, LANES, T), BF16),
        jax.ShapeDtypeStruct((b, 3, nt, T, 2 * LANES), BF16),
        jax.ShapeDtypeStruct((b, 3, nt, 2 * V_ROWS, T), BF16),
        jax.ShapeDtypeStruct((b, nt, 3 * GATE_STRIDE, T), F32),
        jax.ShapeDtypeStruct((b, 2, nt, LANES, T), BF16),
        jax.ShapeDtypeStruct((b, 2, nt, T, LANES), BF16),
        jax.ShapeDtypeStruct((b, 2, nt, 2 * V_ROWS, T), BF16),
    ]
    out_specs = [
        tile5(3, LANES, T), tile5(3, T, LANES), tile5(3, 2 * V_ROWS, T),
        tile5(3, LANES, T), tile5(3, T, 2 * LANES), tile5(3, 2 * V_ROWS, T),
        pl.BlockSpec((None, None, 3 * GATE_STRIDE, T), lambda bi, ti: (bi, ti, 0, 0)),
        tile5(2, LANES, T), tile5(2, T, LANES), tile5(2, 2 * V_ROWS, T),
    ]
    return pl.pallas_call(
        _proj_body,
        grid=(b, nt),
        in_specs=in_specs,
        out_specs=out_specs,
        out_shape=out_shape,
        scratch_shapes=[pltpu.VMEM((GATE_ROWS, 1), F32)],
        compiler_params=pltpu.CompilerParams(dimension_semantics=("arbitrary", "arbitrary"),
                                             vmem_limit_bytes=VMEM_LIMIT_BYTES),
        name="proj",
    )(x, g_mix.reshape(1, d), wn, wt, bf_pad, consts["tri"], consts["cnd"], consts["snd"], consts["cnc"], consts["snc"],
      consts["ctd"], consts["std"], consts["ctc"], consts["stc"])


def _attn_body(*refs, mode, n_back):
    if mode == "fox":
        qt_ref, k_ref, vt_ref, cq_ref, o_ref, qe_ref, m_ref, acc_ref, s_ref, tmax_ref = refs
    elif mode == "diff":
        qt_ref, k_ref, vt_ref, lam_ref, gcol_ref, o_ref, qe_ref, m_ref, acc_ref, s_ref, tmax_ref = refs
    else:
        qt_ref, k_ref, vt_ref, bias_ref, o_ref, qe_ref, m_ref, acc_ref, s_ref, tmax_ref = refs
    n_pairs, _, T = qt_ref.shape
    n_sc = qe_ref.shape[0]
    per_pair = n_sc // n_pairs
    i = pl.program_id(2)

    rows = LANES // per_pair
    zeros_q = jnp.zeros((LANES, T), BF16)
    for sc in range(n_sc):
        p, r = divmod(sc, per_pair)
        qe_ref[sc, 0:LANES, :] = zeros_q
        qe_ref[sc, rows * r:rows * (r + 1), :] = qt_ref[p, rows * r:rows * (r + 1), :]
    if mode == "fox":
        for sc in range(n_sc):
            p, h = divmod(sc, 2)
            terms = _split3(cq_ref[p, h:h + 1, :])
            blocks = [jnp.broadcast_to(t.astype(F32), (BF16_ROWS, T)) for t in terms]
            blocks += [jnp.full((BF16_ROWS, T), 1.0 if h == hh else 0.0, F32) for hh in range(2)]
            blocks += [jnp.zeros((LANES - 5 * BF16_ROWS, T), F32)]
            qe_ref[sc, LANES:2 * LANES, :] = jnp.concatenate(blocks, axis=0).astype(BF16)

    KT = s_ref.shape[1]
    n_sub = T // KT

    def produce(sc, j, u, qlo=0):
        rows_k = slice(KT * u, KT * (u + 1))
        s = _dot(k_ref[sc // per_pair, j, rows_k, :], qe_ref[sc, :, qlo:T])
        if mode == "dil":
            s = s + bias_ref[i - j, rows_k, qlo:T]
        s_ref[sc, :, qlo:T] = s
        tmax_ref[sc, :, qlo:T] = jnp.max(s, axis=0, keepdims=True)

    def consume(sc, j, u, diag, qlo=0):
        s = s_ref[sc, :, qlo:T]
        if diag and mode != "dil":
            kk = lax.broadcasted_iota(jnp.int32, s.shape, 0) + KT * u
            qq = lax.broadcasted_iota(jnp.int32, s.shape, 1) + qlo
            s = jnp.where(kk <= qq, s, NEG_BIG)
            tmax = jnp.max(s, axis=0, keepdims=True)
        else:
            tmax = tmax_ref[sc, :, qlo:T]
        p, r = divmod(sc, per_pair)
        h = r * 2 // per_pair
        m_old = m_ref[sc, :, qlo:T]
        m_new = jnp.maximum(m_old, tmax)
        alpha = jnp.exp2(m_old - m_new)
        pr = jnp.exp2(s - m_new).astype(BF16)
        pv = _dot(vt_ref[p, j, V_ROWS * h:V_ROWS * (h + 1), KT * u:KT * (u + 1)], pr)
        acc_ref[sc, :, qlo:T] = acc_ref[sc, :, qlo:T] * alpha + pv
        m_ref[sc, :, qlo:T] = m_new

    m_ref[...] = jnp.full(m_ref.shape, NEG_BIG, F32)
    acc_ref[...] = jnp.zeros(acc_ref.shape, F32)

    units = [(u, sc) for u in range(n_sub) for sc in range(n_sc)]

    def run_tile(j, diag):
        for w, (u, sc) in enumerate(units):
            if w + 1 < len(units):
                un, scn = units[w + 1]
                produce(scn, j, un, KT * un if diag else 0)
            elif not diag:
                produce(0, j + 1, 0)
            consume(sc, j, u, diag, KT * u if diag else 0)

    lo = jnp.maximum(i - n_back, 0) if mode == "dil" else 0
    produce(0, lo, 0)

    def step(j, carry):
        run_tile(j, False)
        return carry

    lax.fori_loop(lo, i, step, 0)
    run_tile(i, True)

    def normalised(sc):
        a = acc_ref[sc]
        return a[0:HEAD_DIM] / a[HEAD_DIM:HEAD_DIM + 1]

    if mode == "diff":
        lam_init = lam_ref[4:5, 0:1]
        lam = (jnp.exp(jnp.sum(lam_ref[0:1] * lam_ref[1:2], axis=1, keepdims=True))
               - jnp.exp(jnp.sum(lam_ref[2:3] * lam_ref[3:4], axis=1, keepdims=True)) + lam_init)
    for p in range(n_pairs):
        outs = []
        for h in range(2):
            if mode == "diff":
                sc = per_pair * p + 2 * h
                d = normalised(sc) - lam * normalised(sc + 1)
                ms = jnp.mean(d * d, axis=0, keepdims=True)
                outs.append(d * lax.rsqrt(ms + SUBLN_EPS) * gcol_ref[...] * (1.0 - lam_init))
            else:
                outs.append(normalised(per_pair * p + h))
        o_ref[:, LANES * p:LANES * (p + 1)] = jnp.concatenate(outs, axis=0).T.astype(BF16)


def _attn_call(mode, qt, k, vt, extra, T, n_back=0, pairs_per_step=None, n_sub=1):
    b, npair, nt = qt.shape[:3]
    pps = npair if pairs_per_step is None else pairs_per_step
    r = k.shape[-1]
    n_sc = (4 if mode == "diff" else 2) * pps
    in_specs = [
        pl.BlockSpec((None, pps, None, LANES, T), lambda bi, p, i: (bi, p, i, 0, 0)),
        pl.BlockSpec((None, pps, nt, T, r), lambda bi, p, i: (bi, p, 0, 0, 0)),
        pl.BlockSpec((None, pps, nt, 2 * V_ROWS, T), lambda bi, p, i: (bi, p, 0, 0, 0)),
    ]
    if mode == "fox":
        in_specs.append(pl.BlockSpec((None, None, pps, GATE_STRIDE, T), lambda bi, p, i: (bi, i, p, 0, 0)))
    elif mode == "diff":
        in_specs += [_const_spec((8, LANES)), _const_spec((HEAD_DIM, 1))]
    else:
        in_specs.append(_const_spec(extra[0].shape))
    body = functools.partial(_attn_body, mode=mode, n_back=n_back)
    return pl.pallas_call(
        body,
        grid=(b, npair // pps, nt),
        in_specs=in_specs,
        out_specs=pl.BlockSpec((None, T, pps * LANES), lambda bi, p, i: (bi, i, p)),
        out_shape=jax.ShapeDtypeStruct((b, nt * T, npair * LANES), BF16),
        scratch_shapes=[pltpu.VMEM((n_sc, r, T), BF16),
                        pltpu.VMEM((n_sc, 1, T), F32),
                        pltpu.VMEM((n_sc, V_ROWS, T), F32),
                        pltpu.VMEM((n_sc, T // n_sub, T), F32),
                        pltpu.VMEM((n_sc, 1, T), F32)],
        compiler_params=pltpu.CompilerParams(dimension_semantics=("arbitrary", "arbitrary", "arbitrary"),
                                             vmem_limit_bytes=VMEM_LIMIT_BYTES),
        name="attn_" + mode,
    )(qt, k, vt, *extra)


def _constants(s, T):
    pos = np.arange(s, dtype=np.float64)[:, None]

    def angles(half):
        inv = ROPE_THETA ** (-np.arange(half, dtype=np.float64) / half)
        return pos * inv[None, :]

    ad, ac = angles(DIFF_QK_DIM // 2), angles(HEAD_DIM // 2)

    def nat(a):
        reps = LANES // (2 * a.shape[1])
        cos = np.tile(np.concatenate([np.cos(a), np.cos(a)], axis=1), (1, reps))
        sin = np.tile(np.concatenate([-np.sin(a), np.sin(a)], axis=1), (1, reps))
        return cos.astype(np.float32), sin.astype(np.float32)

    cnd, snd = nat(ad)
    cnc, snc = nat(ac)
    consts = dict(cnd=cnd, snd=snd, cnc=cnc, snc=snc,
                  ctd=np.cos(ad).T.astype(np.float32), std=np.sin(ad).T.astype(np.float32),
                  ctc=np.cos(ac).T.astype(np.float32), stc=np.sin(ac).T.astype(np.float32))
    consts["tri"] = np.triu(np.ones((T, T), np.float32))
    n_back = -(-max(w for w, _ in DILATED_PATTERNS) // T)
    kk = np.arange(T)[:, None]
    qq = np.arange(T)[None, :]
    bias = np.zeros((n_back + 1, T, T), np.float32)
    for o in range(n_back + 1):
        delta = o * T + qq - kk
        mult = np.zeros((T, T), np.float64)
        for w, dil in DILATED_PATTERNS:
            mult += (delta >= 0) & (delta <= w) & (delta % dil == 0)
        bias[o] = np.where(mult > 0, np.log2(np.maximum(mult, 1.0)), NEG_BIG)
    out = {}
    for name, v in consts.items():
        out[name] = jnp.asarray(v, BF16 if name == "tri" else F32)
    return out, jnp.asarray(bias), n_back


def _layer_weights(w_in_l, b_f_l):
    d = w_in_l.shape[0]
    sizes = [384] * 6 + [N_FOX_HEADS, 256, 256, 256]
    aq, ak, av, fq, fk, fv, fg, cq, ck, cv = jnp.split(w_in_l, np.cumsum(sizes)[:-1].tolist(), axis=1)
    rows = np.array([GATE_STRIDE * (h // 2) + h % 2 for h in range(N_FOX_HEADS)])
    fg_rows = jnp.zeros((GATE_ROWS, d), F32).at[rows].set(fg.T)
    bf_pad = jnp.zeros((GATE_ROWS, 1), F32).at[rows, 0].set(b_f_l)
    wn = jnp.concatenate([ak, fk, ck], axis=1).astype(BF16)
    wt = jnp.concatenate([jnp.concatenate([aq, fq, cq, av, fv, cv], axis=1).T, fg_rows], axis=0).astype(BF16)
    return wn, wt, bf_pad


def _forward(x, w_in, b_f, lam_q1, lam_k1, lam_q2, lam_k2, g_sub, w_o, g_ffn1, w1_gate, w1_up, w1_down,
             g_mix, g_ffn2, w2_gate, w2_up, w2_down, g_final, T=512, tm=512):
    b, s, d = x.shape
    depth = w_in.shape[0]
    assert s % T == 0 and (b * s) % tm == 0 and d == HEAD_DIM * (N_DIFF_HEADS + N_FOX_HEADS + N_DIL_HEADS)
    consts, bias, n_back = _constants(s, T)
    x2d = x.reshape(b * s, d)
    for l in range(depth):
        lam_init = 0.8 - 0.6 * math.exp(-0.3 * l)
        if l == 0:
            x2d = _ffn_call(x2d, g_ffn1[l], w1_gate[l].astype(BF16), w1_up[l].astype(BF16),
                            w1_down[l].astype(BF16), tm=tm)
        wn, wt, bf_pad = _layer_weights(w_in[l], b_f[l])
        (qtd, kd, vtd, qtf, kf, vtf, cqf, qtc, kc, vtc) = _proj_call(
            x2d.reshape(b, s, d), g_mix[l], wn, wt, bf_pad, consts, T)
        lam_rows = jnp.zeros((8, LANES), F32)
        lam_rows = lam_rows.at[0:4, 0:DIFF_QK_DIM].set(jnp.stack([lam_q1[l], lam_k1[l], lam_q2[l], lam_k2[l]]))
        lam_rows = lam_rows.at[4, :].set(lam_init)
        o_a = _attn_call("diff", qtd, kd, vtd, (lam_rows, g_sub[l].reshape(HEAD_DIM, 1)), T)
        o_b = _attn_call("fox", qtf, kf, vtf, (cqf.reshape(b, s // T, 3, GATE_STRIDE, T),), T)
        o_c = _attn_call("dil", qtc, kc, vtc, (bias,), T, n_back=n_back)
        wo = w_o[l].astype(BF16)
        attn = ([o_a.reshape(b * s, -1), o_b.reshape(b * s, -1), o_c.reshape(b * s, -1)],
                [wo[0:384], wo[384:768], wo[768:1024]])
        last = l == depth - 1
        x2d = _ffn_call(x2d, g_ffn2[l], w2_gate[l].astype(BF16), w2_up[l].astype(BF16), w2_down[l].astype(BF16),
                        attn=attn, g_final=g_final if last else None, tm=tm)
        if not last:
            x2d = _ffn_call(x2d, g_ffn1[l + 1], w1_gate[l + 1].astype(BF16), w1_up[l + 1].astype(BF16),
                            w1_down[l + 1].astype(BF16), tm=tm)
    return x2d.reshape(b, s, d)


def kernel(x, w_in, b_f, lam_q1, lam_k1, lam_q2, lam_k2, g_sub, w_o, g_ffn1, w1_gate, w1_up, w1_down,
           g_mix, g_ffn2, w2_gate, w2_up, w2_down, g_final):
    return _forward(x, w_in, b_f, lam_q1, lam_k1, lam_q2, lam_k2, g_sub, w_o, g_ffn1, w1_gate, w1_up,
                    w1_down, g_mix, g_ffn2, w2_gate, w2_up, w2_down, g_final)
```

```python
import functools
import math

import numpy as np
import jax
import jax.numpy as jnp
from jax import lax
from jax.experimental import pallas as pl
from jax.experimental.pallas import tpu as pltpu

F32 = jnp.float32
BF16 = jnp.bfloat16

HEAD_DIM = 64
N_DIFF_HEADS = 6
N_FOX_HEADS = 6
N_DIL_HEADS = 4
DIFF_QK_DIM = 32
DILATED_PATTERNS = ((128, 1), (512, 4), (2048, 16))
ROPE_THETA = 10000.0
NORM_EPS = 1e-6
SUBLN_EPS = 1e-5
MACARON_SCALE = 0.5
NEG_BIG = -1e30
LOG2E = 1.4426950408889634

LANES = 128
SUBLANES = 8
MXU_WIDTH = 256
BF16_ROWS = 16
V_ROWS = HEAD_DIM + BF16_ROWS
GATE_STRIDE = 8
VMEM_LIMIT_BYTES = 56 * 1024 * 1024


def _dot(a, b):
    return jnp.dot(a, b, preferred_element_type=F32)


def _dot_nt(a, b):
    return lax.dot_general(a, b, (((1,), (1,)), ((), ())), preferred_element_type=F32)


def _rms(x, g, eps):
    ms = jnp.mean(x * x, axis=-1, keepdims=True)
    return x * lax.rsqrt(ms + eps) * g


def _split3(x):
    hi = x.astype(BF16)
    r1 = x - hi.astype(F32)
    mid = r1.astype(BF16)
    lo = (r1 - mid.astype(F32)).astype(BF16)
    return hi, mid, lo


def _ffn_body(*refs, has_attn, has_final, n_chunks):
    refs = list(refs)
    x_ref = refs.pop(0)
    if has_attn:
        oa_ref, ob_ref, oc_ref, woa_ref, wob_ref, woc_ref = refs[:6]
        refs = refs[6:]
    g_ref, wg_ref, wu_ref, wd_ref = refs[:4]
    refs = refs[4:]
    if has_final:
        gf_ref = refs.pop(0)
    out_ref = refs[0]

    x = x_ref[...]
    if has_attn:
        x = x + (_dot(oa_ref[...], woa_ref[...]) + _dot(ob_ref[...], wob_ref[...])
                 + _dot(oc_ref[...], woc_ref[...]))
    xn = _rms(x, g_ref[...], NORM_EPS).astype(BF16)
    n_wide = wg_ref.shape[1] // MXU_WIDTH
    bounds = [MXU_WIDTH * ((n_wide * c + n_chunks - 1) // n_chunks) for c in range(n_chunks + 1)]
    acc = None
    for c in range(n_chunks):
        sl = slice(bounds[c], bounds[c + 1])
        gate = _dot(xn, wg_ref[:, sl])
        up = _dot(xn, wu_ref[:, sl])
        h = (gate * jax.nn.sigmoid(gate) * up).astype(BF16)
        d = _dot(h, wd_ref[sl, :])
        acc = d if acc is None else acc + d
    y = x + MACARON_SCALE * acc
    if has_final:
        y = _rms(y, gf_ref[...], NORM_EPS)
    out_ref[...] = y


def _const_spec(shape):
    nd = len(shape)
    return pl.BlockSpec(shape, lambda *_: (0,) * nd, pipeline_mode=pl.Buffered(1))


def _ffn_call(x2d, g, wg, wu, wd, attn=None, g_final=None, tm=512):
    n, d = x2d.shape
    dff = wg.shape[1]
    args = [x2d]
    specs = [pl.BlockSpec((tm, d), lambda i: (i, 0))]
    if attn is not None:
        o_list, w_list = attn
        for o in o_list:
            args.append(o)
            specs.append(pl.BlockSpec((tm, o.shape[1]), lambda i: (i, 0)))
        for w in w_list:
            args.append(w)
            specs.append(_const_spec(w.shape))
    args += [g.reshape(1, d), wg, wu, wd]
    specs += [_const_spec((1, d)), _const_spec((d, dff)), _const_spec((d, dff)), _const_spec((dff, d))]
    if g_final is not None:
        args.append(g_final.reshape(1, d))
        specs.append(_const_spec((1, d)))
    assert dff % MXU_WIDTH == 0
    n_chunks = 2
    body = functools.partial(_ffn_body, has_attn=attn is not None, has_final=g_final is not None,
                             n_chunks=n_chunks)
    return pl.pallas_call(
        body,
        grid=(n // tm,),
        in_specs=specs,
        out_specs=pl.BlockSpec((tm, d), lambda i: (i, 0)),
        out_shape=jax.ShapeDtypeStruct((n, d), F32),
        compiler_params=pltpu.CompilerParams(dimension_semantics=("arbitrary",),
                                             vmem_limit_bytes=VMEM_LIMIT_BYTES),
        name="ffn",
    )(*args)


N_NAT = 384 + 384 + 256
N_TR = 2 * (384 + 384 + 256)
GATE_ROWS = 32


def _proj_body(x_ref, g_ref, wn_ref, wt_ref, bf_ref, tri_ref,
               cnd_ref, snd_ref, cnc_ref, snc_ref, ctd_ref, std_ref, ctc_ref, stc_ref,
               qtd_ref, kd_ref, vtd_ref, qtf_ref, kf_ref, vtf_ref, cq_ref, qtc_ref, kc_ref, vtc_ref,
               carry_ref):
    tm = x_ref.shape[0]

    @pl.when(pl.program_id(1) == 0)
    def _():
        carry_ref[...] = jnp.zeros_like(carry_ref)

    xn = _rms(x_ref[...], g_ref[...], NORM_EPS).astype(BF16)
    rn = _dot(xn, wn_ref[...])
    rt = _dot_nt(wt_ref[...], xn)

    lane = lax.broadcasted_iota(jnp.int32, (tm, LANES), 1)

    def rope_nat(blk, half, cos, sin_signed):
        first = (lane & half) == 0
        rot = jnp.where(first, pltpu.roll(blk, LANES - half, 1), pltpu.roll(blk, half, 1))
        return (blk * cos + rot * sin_signed).astype(BF16)

    for p in range(3):
        kd_ref[p] = rope_nat(rn[:, LANES * p:LANES * (p + 1)], DIFF_QK_DIM // 2, cnd_ref[...], snd_ref[...])
    for p in range(2):
        c0 = 768 + LANES * p
        kc_ref[p] = rope_nat(rn[:, c0:c0 + LANES], HEAD_DIM // 2, cnc_ref[...], snc_ref[...])

    def rope_tr(ref, p, r0, x1, x2, cos, sin, scale):
        half = x1.shape[0]
        ref[p, r0:r0 + half, :] = ((x1 * cos - x2 * sin) * scale).astype(BF16)
        ref[p, r0 + half:r0 + 2 * half, :] = ((x2 * cos + x1 * sin) * scale).astype(BF16)

    sd = DIFF_QK_DIM ** -0.5 * LOG2E
    hd = DIFF_QK_DIM // 2
    for g in range(2 * N_DIFF_HEADS):
        b0 = DIFF_QK_DIM * g
        rope_tr(qtd_ref, g // 4, DIFF_QK_DIM * (g % 4), rt[b0:b0 + hd], rt[b0 + hd:b0 + 2 * hd],
                ctd_ref[...], std_ref[...], sd)
    sf = HEAD_DIM ** -0.5 * LOG2E
    for p in range(3):
        b0 = 384 + LANES * p
        qtf_ref[p] = (rt[b0:b0 + LANES] * sf).astype(BF16)
    hc = HEAD_DIM // 2
    for h in range(N_DIL_HEADS):
        b0 = 768 + HEAD_DIM * h
        rope_tr(qtc_ref, h // 2, HEAD_DIM * (h % 2), rt[b0:b0 + hc], rt[b0 + hc:b0 + 2 * hc],
                ctc_ref[...], stc_ref[...], sf)

    ones_rows = jnp.ones((BF16_ROWS, tm), BF16)

    def put_vt(ref, base, n_pairs):
        for p in range(n_pairs):
            for h in range(2):
                b0 = base + LANES * p + HEAD_DIM * h
                ref[p, V_ROWS * h:V_ROWS * h + HEAD_DIM, :] = rt[b0:b0 + HEAD_DIM].astype(BF16)
                ref[p, V_ROWS * h + HEAD_DIM:V_ROWS * (h + 1), :] = ones_rows

    put_vt(vtd_ref, 1024, 3)
    put_vt(vtf_ref, 1408, 3)
    put_vt(vtc_ref, 1792, 2)

    z = rt[N_TR:N_TR + GATE_ROWS] + bf_ref[...]
    logf = (jnp.minimum(z, 0.0) - jnp.log1p(jnp.exp(-jnp.abs(z)))) * LOG2E
    part = _dot(jnp.concatenate(_split3(logf), axis=0), tri_ref[...])
    c = part[0:GATE_ROWS] + part[GATE_ROWS:2 * GATE_ROWS] + part[2 * GATE_ROWS:3 * GATE_ROWS] + carry_ref[...]
    carry_ref[...] = c[:, tm - 1:tm]
    cq_ref[...] = c[0:3 * GATE_STRIDE]
    row = lax.broadcasted_iota(jnp.int32, (LANES, tm), 0)
    ones_pattern = jnp.where((row < 3 * BF16_ROWS) & ((row & (BF16_ROWS - 1)) == 0), 1.0, 0.0)
    for p in range(3):
        kf_ref[p, :, 0:LANES] = rn[:, 384 + LANES * p:384 + LANES * (p + 1)].astype(BF16)
        auxt = ones_pattern
        for h in range(2):
            ch = c[GATE_STRIDE * p + h:GATE_STRIDE * p + h + 1]
            for t, term in enumerate(_split3(ch)):
                auxt = jnp.where(row == (3 + h) * BF16_ROWS + t, -term.astype(F32), auxt)
        kf_ref[p, :, LANES:2 * LANES] = auxt.T.astype(BF16)


def _proj_call(x, g_mix, wn, wt, bf_pad, consts, T):
    b, s, d = x.shape
    nt = s // T
    tile5 = lambda npair, r, c: pl.BlockSpec((None, npair, None, r, c), lambda bi, ti: (bi, 0, ti, 0, 0))
    nat_tab = pl.BlockSpec((T, LANES), lambda bi, ti: (ti, 0))
    tr_tab = lambda r: pl.BlockSpec((r, T), lambda bi, ti: (0, ti))
    in_specs = [
        pl.BlockSpec((None, T, d), lambda bi, ti: (bi, ti, 0)),
        _const_spec((1, d)), _const_spec(wn.shape), _const_spec(wt.shape), _const_spec((GATE_ROWS, 1)),
        _const_spec((T, T)),
        nat_tab, nat_tab, nat_tab, nat_tab,
        tr_tab(DIFF_QK_DIM // 2), tr_tab(DIFF_QK_DIM // 2), tr_tab(HEAD_DIM // 2), tr_tab(HEAD_DIM // 2),
    ]
    out_shape = [
        jax.ShapeDtypeStruct((b, 3, nt, LANES, T), BF16),
        jax.ShapeDtypeStruct((b, 3, nt, T, LANES), BF16),
        jax.ShapeDtypeStruct((b, 3, nt, 2 * V_ROWS, T), BF16),
        jax.ShapeDtypeStruct((b, 3, nt, LANES, T), BF16),
        jax.ShapeDtypeStruct((b, 3, nt, T, 2 * LANES), BF16),
        jax.ShapeDtypeStruct((b, 3, nt, 2 * V_ROWS, T), BF16),
        jax.ShapeDtypeStruct((b, nt, 3 * GATE_STRIDE, T), F32),
        jax.ShapeDtypeStruct((b, 2, nt, LANES, T), BF16),
        jax.ShapeDtypeStruct((b, 2, nt, T, LANES), BF16),
        jax.ShapeDtypeStruct((b, 2, nt, 2 * V_ROWS, T), BF16),
    ]
    out_specs = [
        tile5(3, LANES, T), tile5(3, T, LANES), tile5(3, 2 * V_ROWS, T),
        tile5(3, LANES, T), tile5(3, T, 2 * LANES), tile5(3, 2 * V_ROWS, T),
        pl.BlockSpec((None, None, 3 * GATE_STRIDE, T), lambda bi, ti: (bi, ti, 0, 0)),
        tile5(2, LANES, T), tile5(2, T, LANES), tile5(2, 2 * V_ROWS, T),
    ]
    return pl.pallas_call(
        _proj_body,
        grid=(b, nt),
        in_specs=in_specs,
        out_specs=out_specs,
        out_shape=out_shape,
        scratch_shapes=[pltpu.VMEM((GATE_ROWS, 1), F32)],
        compiler_params=pltpu.CompilerParams(dimension_semantics=("arbitrary", "arbitrary"),
                                             vmem_limit_bytes=VMEM_LIMIT_BYTES),
        name="proj",
    )(x, g_mix.reshape(1, d), wn, wt, bf_pad, consts["tri"], consts["cnd"], consts["snd"], consts["cnc"], consts["snc"],
      consts["ctd"], consts["std"], consts["ctc"], consts["stc"])


def _attn_body(*refs, mode, n_back):
    if mode == "fox":
        qt_ref, k_ref, vt_ref, cq_ref, o_ref, qe_ref, m_ref, acc_ref, s_ref, tmax_ref = refs
    elif mode == "diff":
        qt_ref, k_ref, vt_ref, lam_ref, gcol_ref, o_ref, qe_ref, m_ref, acc_ref, s_ref, tmax_ref = refs
    else:
        qt_ref, k_ref, vt_ref, bias_ref, o_ref, qe_ref, m_ref, acc_ref, s_ref, tmax_ref = refs
    n_pairs, _, T = qt_ref.shape
    n_sc = qe_ref.shape[0]
    per_pair = n_sc // n_pairs
    i = pl.program_id(2)

    rows = LANES // per_pair
    zeros_q = jnp.zeros((LANES, T), BF16)
    for sc in range(n_sc):
        p, r = divmod(sc, per_pair)
        qe_ref[sc, 0:LANES, :] = zeros_q
        qe_ref[sc, rows * r:rows * (r + 1), :] = qt_ref[p, rows * r:rows * (r + 1), :]
    if mode == "fox":
        for sc in range(n_sc):
            p, h = divmod(sc, 2)
            terms = _split3(cq_ref[p, h:h + 1, :])
            blocks = [jnp.broadcast_to(t.astype(F32), (BF16_ROWS, T)) for t in terms]
            blocks += [jnp.full((BF16_ROWS, T), 1.0 if h == hh else 0.0, F32) for hh in range(2)]
            blocks += [jnp.zeros((LANES - 5 * BF16_ROWS, T), F32)]
            qe_ref[sc, LANES:2 * LANES, :] = jnp.concatenate(blocks, axis=0).astype(BF16)

    KT = s_ref.shape[1]
    n_sub = T // KT

    def produce(sc, j, u, qlo=0):
        rows_k = slice(KT * u, KT * (u + 1))
        s = _dot(k_ref[sc // per_pair, j, rows_k, :], qe_ref[sc, :, qlo:T])
        if mode == "dil":
            s = s + bias_ref[i - j, rows_k, qlo:T]
        s_ref[sc, :, qlo:T] = s
        tmax_ref[sc, :, qlo:T] = jnp.broadcast_to(jnp.max(s, axis=0, keepdims=True), (SUBLANES, T - qlo))

    def consume(sc, j, u, diag, qlo=0):
        s = s_ref[sc, :, qlo:T]
        if diag and mode != "dil":
            kk = lax.broadcasted_iota(jnp.int32, s.shape, 0) + KT * u
            qq = lax.broadcasted_iota(jnp.int32, s.shape, 1) + qlo
            s = jnp.where(kk <= qq, s, NEG_BIG)
            tmax = jnp.max(s, axis=0, keepdims=True)
        else:
            tmax = tmax_ref[sc, 0:1, qlo:T]
        p, r = divmod(sc, per_pair)
        h = r * 2 // per_pair
        m_old = m_ref[sc, 0:1, qlo:T]
        m_new = jnp.maximum(m_old, tmax)
        alpha = jnp.exp2(m_old - m_new)
        pr = jnp.exp2(s - m_new).astype(BF16)
        pv = _dot(vt_ref[p, j, V_ROWS * h:V_ROWS * (h + 1), KT * u:KT * (u + 1)], pr)
        acc_ref[sc, :, qlo:T] = acc_ref[sc, :, qlo:T] * alpha + pv
        m_ref[sc, :, qlo:T] = jnp.broadcast_to(m_new, (SUBLANES, T - qlo))

    m_ref[...] = jnp.full(m_ref.shape, NEG_BIG, F32)
    acc_ref[...] = jnp.zeros(acc_ref.shape, F32)

    units = [(u, sc) for u in range(n_sub) for sc in range(n_sc)]

    def run_tile(j, diag):
        for w, (u, sc) in enumerate(units):
            if w + 1 < len(units):
                un, scn = units[w + 1]
                produce(scn, j, un, KT * un if diag else 0)
            elif not diag:
                produce(0, j + 1, 0)
            consume(sc, j, u, diag, KT * u if diag else 0)

    lo = jnp.maximum(i - n_back, 0) if mode == "dil" else 0
    produce(0, lo, 0)

    def step(j, carry):
        run_tile(j, False)
        return carry

    lax.fori_loop(lo, i, step, 0)
    run_tile(i, True)

    def normalised(sc):
        a = acc_ref[sc]
        return a[0:HEAD_DIM] / a[HEAD_DIM:HEAD_DIM + 1]

    if mode == "diff":
        lam_init = lam_ref[4:5, 0:1]
        lam = (jnp.exp(jnp.sum(lam_ref[0:1] * lam_ref[1:2], axis=1, keepdims=True))
               - jnp.exp(jnp.sum(lam_ref[2:3] * lam_ref[3:4], axis=1, keepdims=True)) + lam_init)
    for p in range(n_pairs):
        outs = []
        for h in range(2):
            if mode == "diff":
                sc = per_pair * p + 2 * h
                d = normalised(sc) - lam * normalised(sc + 1)
                ms = jnp.mean(d * d, axis=0, keepdims=True)
                outs.append(d * lax.rsqrt(ms + SUBLN_EPS) * gcol_ref[...] * (1.0 - lam_init))
            else:
                outs.append(normalised(per_pair * p + h))
        o_ref[:, LANES * p:LANES * (p + 1)] = jnp.concatenate(outs, axis=0).T.astype(BF16)


def _attn_call(mode, qt, k, vt, extra, T, n_back=0, pairs_per_step=None, n_sub=1):
    b, npair, nt = qt.shape[:3]
    pps = npair if pairs_per_step is None else pairs_per_step
    r = k.shape[-1]
    n_sc = (4 if mode == "diff" else 2) * pps
    in_specs = [
        pl.BlockSpec((None, pps, None, LANES, T), lambda bi, p, i: (bi, p, i, 0, 0)),
        pl.BlockSpec((None, pps, nt, T, r), lambda bi, p, i: (bi, p, 0, 0, 0)),
        pl.BlockSpec((None, pps, nt, 2 * V_ROWS, T), lambda bi, p, i: (bi, p, 0, 0, 0)),
    ]
    if mode == "fox":
        in_specs.append(pl.BlockSpec((None, None, pps, GATE_STRIDE, T), lambda bi, p, i: (bi, i, p, 0, 0)))
    elif mode == "diff":
        in_specs += [_const_spec((8, LANES)), _const_spec((HEAD_DIM, 1))]
    else:
        in_specs.append(_const_spec(extra[0].shape))
    body = functools.partial(_attn_body, mode=mode, n_back=n_back)
    return pl.pallas_call(
        body,
        grid=(b, npair // pps, nt),
        in_specs=in_specs,
        out_specs=pl.BlockSpec((None, T, pps * LANES), lambda bi, p, i: (bi, i, p)),
        out_shape=jax.ShapeDtypeStruct((b, nt * T, npair * LANES), BF16),
        scratch_shapes=[pltpu.VMEM((n_sc, r, T), BF16),
                        pltpu.VMEM((n_sc, SUBLANES, T), F32),
                        pltpu.VMEM((n_sc, V_ROWS, T), F32),
                        pltpu.VMEM((n_sc, T // n_sub, T), F32),
                        pltpu.VMEM((n_sc, SUBLANES, T), F32)],
        compiler_params=pltpu.CompilerParams(dimension_semantics=("arbitrary", "arbitrary", "arbitrary"),
                                             vmem_limit_bytes=VMEM_LIMIT_BYTES),
        name="attn_" + mode,
    )(qt, k, vt, *extra)


def _constants(s, T):
    pos = np.arange(s, dtype=np.float64)[:, None]

    def angles(half):
        inv = ROPE_THETA ** (-np.arange(half, dtype=np.float64) / half)
        return pos * inv[None, :]

    ad, ac = angles(DIFF_QK_DIM // 2), angles(HEAD_DIM // 2)

    def nat(a):
        reps = LANES // (2 * a.shape[1])
        cos = np.tile(np.concatenate([np.cos(a), np.cos(a)], axis=1), (1, reps))
        sin = np.tile(np.concatenate([-np.sin(a), np.sin(a)], axis=1), (1, reps))
        return cos.astype(np.float32), sin.astype(np.float32)

    cnd, snd = nat(ad)
    cnc, snc = nat(ac)
    consts = dict(cnd=cnd, snd=snd, cnc=cnc, snc=snc,
                  ctd=np.cos(ad).T.astype(np.float32), std=np.sin(ad).T.astype(np.float32),
                  ctc=np.cos(ac).T.astype(np.float32), stc=np.sin(ac).T.astype(np.float32))
    consts["tri"] = np.triu(np.ones((T, T), np.float32))
    n_back = -(-max(w for w, _ in DILATED_PATTERNS) // T)
    kk = np.arange(T)[:, None]
    qq = np.arange(T)[None, :]
    bias = np.zeros((n_back + 1, T, T), np.float32)
    for o in range(n_back + 1):
        delta = o * T + qq - kk
        mult = np.zeros((T, T), np.float64)
        for w, dil in DILATED_PATTERNS:
            mult += (delta >= 0) & (delta <= w) & (delta % dil == 0)
        bias[o] = np.where(mult > 0, np.log2(np.maximum(mult, 1.0)), NEG_BIG)
    out = {}
    for name, v in consts.items():
        out[name] = jnp.asarray(v, BF16 if name == "tri" else F32)
    return out, jnp.asarray(bias), n_back


def _proj_weights(w_in, b_f):
    depth, d, _ = w_in.shape
    w = w_in.astype(BF16)
    sizes = [384] * 6 + [N_FOX_HEADS, 256, 256, 256]
    aq, ak, av, fq, fk, fv, fg, cq, ck, cv = jnp.split(w, np.cumsum(sizes)[:-1].tolist(), axis=2)
    rows = np.array([GATE_STRIDE * (h // 2) + h % 2 for h in range(N_FOX_HEADS)])
    fg_cols = jnp.zeros((depth, d, GATE_ROWS), BF16).at[:, :, rows].set(fg)
    bf_pad = jnp.zeros((depth, GATE_ROWS, 1), F32).at[:, rows, 0].set(b_f)
    wn = jnp.concatenate([ak, fk, ck], axis=2)
    wt = jnp.concatenate([aq, fq, cq, av, fv, cv, fg_cols], axis=2).transpose(0, 2, 1)
    return wn, wt, bf_pad


def _forward(x, w_in, b_f, lam_q1, lam_k1, lam_q2, lam_k2, g_sub, w_o, g_ffn1, w1_gate, w1_up, w1_down,
             g_mix, g_ffn2, w2_gate, w2_up, w2_down, g_final, T=512, tm=512):
    b, s, d = x.shape
    depth = w_in.shape[0]
    assert s % T == 0 and (b * s) % tm == 0 and d == HEAD_DIM * (N_DIFF_HEADS + N_FOX_HEADS + N_DIL_HEADS)
    consts, bias, n_back = _constants(s, T)
    wn_all, wt_all, bf_all = _proj_weights(w_in, b_f)
    x2d = x.reshape(b * s, d)
    for l in range(depth):
        lam_init = 0.8 - 0.6 * math.exp(-0.3 * l)
        if l == 0:
            x2d = _ffn_call(x2d, g_ffn1[l], w1_gate[l].astype(BF16), w1_up[l].astype(BF16),
                            w1_down[l].astype(BF16), tm=tm)
        (qtd, kd, vtd, qtf, kf, vtf, cqf, qtc, kc, vtc) = _proj_call(
            x2d.reshape(b, s, d), g_mix[l], wn_all[l], wt_all[l], bf_all[l], consts, T)
        lam_rows = jnp.zeros((8, LANES), F32)
        lam_rows = lam_rows.at[0:4, 0:DIFF_QK_DIM].set(jnp.stack([lam_q1[l], lam_k1[l], lam_q2[l], lam_k2[l]]))
        lam_rows = lam_rows.at[4, :].set(lam_init)
        o_a = _attn_call("diff", qtd, kd, vtd, (lam_rows, g_sub[l].reshape(HEAD_DIM, 1)), T)
        o_b = _attn_call("fox", qtf, kf, vtf, (cqf.reshape(b, s // T, 3, GATE_STRIDE, T),), T)
        o_c = _attn_call("dil", qtc, kc, vtc, (bias,), T, n_back=n_back)
        wo = w_o[l].astype(BF16)
        attn = ([o_a.reshape(b * s, -1), o_b.reshape(b * s, -1), o_c.reshape(b * s, -1)],
                [wo[0:384], wo[384:768], wo[768:1024]])
        last = l == depth - 1
        x2d = _ffn_call(x2d, g_ffn2[l], w2_gate[l].astype(BF16), w2_up[l].astype(BF16), w2_down[l].astype(BF16),
                        attn=attn, g_final=g_final if last else None, tm=tm)
        if not last:
            x2d = _ffn_call(x2d, g_ffn1[l + 1], w1_gate[l + 1].astype(BF16), w1_up[l + 1].astype(BF16),
                            w1_down[l + 1].astype(BF16), tm=tm)
    return x2d.reshape(b, s, d)


def kernel(x, w_in, b_f, lam_q1, lam_k1, lam_q2, lam_k2, g_sub, w_o, g_ffn1, w1_gate, w1_up, w1_down,
           g_mix, g_ffn2, w2_gate, w2_up, w2_down, g_final):
    return _forward(x, w_in, b_f, lam_q1, lam_k1, lam_q2, lam_k2, g_sub, w_o, g_ffn1, w1_gate, w1_up,
                    w1_down, g_mix, g_ffn2, w2_gate, w2_up, w2_down, g_final)
```

```python
import functools
import math

import numpy as np
import jax
import jax.numpy as jnp
from jax import lax
from jax.experimental import pallas as pl
from jax.experimental.pallas import tpu as pltpu

F32 = jnp.float32
BF16 = jnp.bfloat16

HEAD_DIM = 64
N_DIFF_HEADS = 6
N_FOX_HEADS = 6
N_DIL_HEADS = 4
DIFF_QK_DIM = 32
DILATED_PATTERNS = ((128, 1), (512, 4), (2048, 16))
ROPE_THETA = 10000.0
NORM_EPS = 1e-6
SUBLN_EPS = 1e-5
MACARON_SCALE = 0.5
NEG_BIG = -1e30
LOG2E = 1.4426950408889634

LANES = 128
SUBLANES = 8
MXU_WIDTH = 256
BF16_ROWS = 16
V_ROWS = HEAD_DIM + BF16_ROWS
GATE_STRIDE = 8
VMEM_LIMIT_BYTES = 56 * 1024 * 1024
PIPELINE_AHEAD = 2


def _dot(a, b):
    return jnp.dot(a, b, preferred_element_type=F32)


def _dot_nt(a, b):
    return lax.dot_general(a, b, (((1,), (1,)), ((), ())), preferred_element_type=F32)


def _rms(x, g, eps):
    ms = jnp.mean(x * x, axis=-1, keepdims=True)
    return x * lax.rsqrt(ms + eps) * g


def _split3(x):
    hi = x.astype(BF16)
    r1 = x - hi.astype(F32)
    mid = r1.astype(BF16)
    lo = (r1 - mid.astype(F32)).astype(BF16)
    return hi, mid, lo


def _ffn_body(*refs, has_attn, has_final, n_chunks):
    refs = list(refs)
    x_ref = refs.pop(0)
    if has_attn:
        oa_ref, ob_ref, oc_ref, woa_ref, wob_ref, woc_ref = refs[:6]
        refs = refs[6:]
    g_ref, wg_ref, wu_ref, wd_ref = refs[:4]
    refs = refs[4:]
    if has_final:
        gf_ref = refs.pop(0)
    out_ref = refs[0]

    x = x_ref[...]
    if has_attn:
        x = x + (_dot(oa_ref[...], woa_ref[...]) + _dot(ob_ref[...], wob_ref[...])
                 + _dot(oc_ref[...], woc_ref[...]))
    xn = _rms(x, g_ref[...], NORM_EPS).astype(BF16)
    n_wide = wg_ref.shape[1] // MXU_WIDTH
    bounds = [MXU_WIDTH * ((n_wide * c + n_chunks - 1) // n_chunks) for c in range(n_chunks + 1)]
    acc = None
    for c in range(n_chunks):
        sl = slice(bounds[c], bounds[c + 1])
        gate = _dot(xn, wg_ref[:, sl])
        up = _dot(xn, wu_ref[:, sl])
        h = (gate * jax.nn.sigmoid(gate) * up).astype(BF16)
        d = _dot(h, wd_ref[sl, :])
        acc = d if acc is None else acc + d
    y = x + MACARON_SCALE * acc
    if has_final:
        y = _rms(y, gf_ref[...], NORM_EPS)
    out_ref[...] = y


def _const_spec(shape):
    nd = len(shape)
    return pl.BlockSpec(shape, lambda *_: (0,) * nd, pipeline_mode=pl.Buffered(1))


def _ffn_call(x2d, g, wg, wu, wd, attn=None, g_final=None, tm=512):
    n, d = x2d.shape
    dff = wg.shape[1]
    args = [x2d]
    specs = [pl.BlockSpec((tm, d), lambda i: (i, 0))]
    if attn is not None:
        o_list, w_list = attn
        for o in o_list:
            args.append(o)
            specs.append(pl.BlockSpec((tm, o.shape[1]), lambda i: (i, 0)))
        for w in w_list:
            args.append(w)
            specs.append(_const_spec(w.shape))
    args += [g.reshape(1, d), wg, wu, wd]
    specs += [_const_spec((1, d)), _const_spec((d, dff)), _const_spec((d, dff)), _const_spec((dff, d))]
    if g_final is not None:
        args.append(g_final.reshape(1, d))
        specs.append(_const_spec((1, d)))
    assert dff % MXU_WIDTH == 0
    n_chunks = 2
    body = functools.partial(_ffn_body, has_attn=attn is not None, has_final=g_final is not None,
                             n_chunks=n_chunks)
    return pl.pallas_call(
        body,
        grid=(n // tm,),
        in_specs=specs,
        out_specs=pl.BlockSpec((tm, d), lambda i: (i, 0)),
        out_shape=jax.ShapeDtypeStruct((n, d), F32),
        compiler_params=pltpu.CompilerParams(dimension_semantics=("arbitrary",),
                                             vmem_limit_bytes=VMEM_LIMIT_BYTES),
        name="ffn",
    )(*args)


N_NAT = 384 + 384 + 256
N_TR = 2 * (384 + 384 + 256)
GATE_ROWS = 32


def _proj_body(x_ref, g_ref, wn_ref, wt_ref, bf_ref, tri_ref,
               cnd_ref, snd_ref, cnc_ref, snc_ref, ctd_ref, std_ref, ctc_ref, stc_ref,
               qtd_ref, kd_ref, vtd_ref, qtf_ref, kf_ref, vtf_ref, cq_ref, qtc_ref, kc_ref, vtc_ref,
               carry_ref):
    tm = x_ref.shape[0]

    @pl.when(pl.program_id(1) == 0)
    def _():
        carry_ref[...] = jnp.zeros_like(carry_ref)

    xn = _rms(x_ref[...], g_ref[...], NORM_EPS).astype(BF16)
    rn = _dot(xn, wn_ref[...])
    rt = _dot_nt(wt_ref[...], xn)

    lane = lax.broadcasted_iota(jnp.int32, (tm, LANES), 1)

    def rope_nat(blk, half, cos, sin_signed):
        first = (lane & half) == 0
        rot = jnp.where(first, pltpu.roll(blk, LANES - half, 1), pltpu.roll(blk, half, 1))
        return (blk * cos + rot * sin_signed).astype(BF16)

    for p in range(3):
        kd_ref[p] = rope_nat(rn[:, LANES * p:LANES * (p + 1)], DIFF_QK_DIM // 2, cnd_ref[...], snd_ref[...])
    for p in range(2):
        c0 = 768 + LANES * p
        kc_ref[p] = rope_nat(rn[:, c0:c0 + LANES], HEAD_DIM // 2, cnc_ref[...], snc_ref[...])

    def rope_tr(ref, p, r0, x1, x2, cos, sin, scale):
        half = x1.shape[0]
        ref[p, r0:r0 + half, :] = ((x1 * cos - x2 * sin) * scale).astype(BF16)
        ref[p, r0 + half:r0 + 2 * half, :] = ((x2 * cos + x1 * sin) * scale).astype(BF16)

    sd = DIFF_QK_DIM ** -0.5 * LOG2E
    hd = DIFF_QK_DIM // 2
    for g in range(2 * N_DIFF_HEADS):
        b0 = DIFF_QK_DIM * g
        rope_tr(qtd_ref, g // 4, DIFF_QK_DIM * (g % 4), rt[b0:b0 + hd], rt[b0 + hd:b0 + 2 * hd],
                ctd_ref[...], std_ref[...], sd)
    sf = HEAD_DIM ** -0.5 * LOG2E
    for p in range(3):
        b0 = 384 + LANES * p
        qtf_ref[p] = (rt[b0:b0 + LANES] * sf).astype(BF16)
    hc = HEAD_DIM // 2
    for h in range(N_DIL_HEADS):
        b0 = 768 + HEAD_DIM * h
        rope_tr(qtc_ref, h // 2, HEAD_DIM * (h % 2), rt[b0:b0 + hc], rt[b0 + hc:b0 + 2 * hc],
                ctc_ref[...], stc_ref[...], sf)

    ones_rows = jnp.ones((BF16_ROWS, tm), BF16)

    def put_vt(ref, base, n_pairs):
        for p in range(n_pairs):
            for h in range(2):
                b0 = base + LANES * p + HEAD_DIM * h
                ref[p, V_ROWS * h:V_ROWS * h + HEAD_DIM, :] = rt[b0:b0 + HEAD_DIM].astype(BF16)
                ref[p, V_ROWS * h + HEAD_DIM:V_ROWS * (h + 1), :] = ones_rows

    put_vt(vtd_ref, 1024, 3)
    put_vt(vtf_ref, 1408, 3)
    put_vt(vtc_ref, 1792, 2)

    z = rt[N_TR:N_TR + GATE_ROWS] + bf_ref[...]
    logf = (jnp.minimum(z, 0.0) - jnp.log1p(jnp.exp(-jnp.abs(z)))) * LOG2E
    part = _dot(jnp.concatenate(_split3(logf), axis=0), tri_ref[...])
    c = part[0:GATE_ROWS] + part[GATE_ROWS:2 * GATE_ROWS] + part[2 * GATE_ROWS:3 * GATE_ROWS] + carry_ref[...]
    carry_ref[...] = c[:, tm - 1:tm]
    cq_ref[...] = c[0:3 * GATE_STRIDE]
    row = lax.broadcasted_iota(jnp.int32, (LANES, tm), 0)
    ones_pattern = jnp.where((row < 3 * BF16_ROWS) & ((row & (BF16_ROWS - 1)) == 0), 1.0, 0.0)
    for p in range(3):
        kf_ref[p, :, 0:LANES] = rn[:, 384 + LANES * p:384 + LANES * (p + 1)].astype(BF16)
        auxt = ones_pattern
        for h in range(2):
            ch = c[GATE_STRIDE * p + h:GATE_STRIDE * p + h + 1]
            for t, term in enumerate(_split3(ch)):
                auxt = jnp.where(row == (3 + h) * BF16_ROWS + t, -term.astype(F32), auxt)
        kf_ref[p, :, LANES:2 * LANES] = auxt.T.astype(BF16)


def _proj_call(x, g_mix, wn, wt, bf_pad, consts, T):
    b, s, d = x.shape
    nt = s // T
    tile5 = lambda npair, r, c: pl.BlockSpec((None, npair, None, r, c), lambda bi, ti: (bi, 0, ti, 0, 0))
    nat_tab = pl.BlockSpec((T, LANES), lambda bi, ti: (ti, 0))
    tr_tab = lambda r: pl.BlockSpec((r, T), lambda bi, ti: (0, ti))
    in_specs = [
        pl.BlockSpec((None, T, d), lambda bi, ti: (bi, ti, 0)),
        _const_spec((1, d)), _const_spec(wn.shape), _const_spec(wt.shape), _const_spec((GATE_ROWS, 1)),
        _const_spec((T, T)),
        nat_tab, nat_tab, nat_tab, nat_tab,
        tr_tab(DIFF_QK_DIM // 2), tr_tab(DIFF_QK_DIM // 2), tr_tab(HEAD_DIM // 2), tr_tab(HEAD_DIM // 2),
    ]
    out_shape = [
        jax.ShapeDtypeStruct((b, 3, nt, LANES, T), BF16),
        jax.ShapeDtypeStruct((b, 3, nt, T, LANES), BF16),
        jax.ShapeDtypeStruct((b, 3, nt, 2 * V_ROWS, T), BF16),
        jax.ShapeDtypeStruct((b, 3, nt, LANES, T), BF16),
        jax.ShapeDtypeStruct((b, 3, nt, T, 2 * LANES), BF16),
        jax.ShapeDtypeStruct((b, 3, nt, 2 * V_ROWS, T), BF16),
        jax.ShapeDtypeStruct((b, nt, 3 * GATE_STRIDE, T), F32),
        jax.ShapeDtypeStruct((b, 2, nt, LANES, T), BF16),
        jax.ShapeDtypeStruct((b, 2, nt, T, LANES), BF16),
        jax.ShapeDtypeStruct((b, 2, nt, 2 * V_ROWS, T), BF16),
    ]
    out_specs = [
        tile5(3, LANES, T), tile5(3, T, LANES), tile5(3, 2 * V_ROWS, T),
        tile5(3, LANES, T), tile5(3, T, 2 * LANES), tile5(3, 2 * V_ROWS, T),
        pl.BlockSpec((None, None, 3 * GATE_STRIDE, T), lambda bi, ti: (bi, ti, 0, 0)),
        tile5(2, LANES, T), tile5(2, T, LANES), tile5(2, 2 * V_ROWS, T),
    ]
    return pl.pallas_call(
        _proj_body,
        grid=(b, nt),
        in_specs=in_specs,
        out_specs=out_specs,
        out_shape=out_shape,
        scratch_shapes=[pltpu.VMEM((GATE_ROWS, 1), F32)],
        compiler_params=pltpu.CompilerParams(dimension_semantics=("arbitrary", "arbitrary"),
                                             vmem_limit_bytes=VMEM_LIMIT_BYTES),
        name="proj",
    )(x, g_mix.reshape(1, d), wn, wt, bf_pad, consts["tri"], consts["cnd"], consts["snd"], consts["cnc"], consts["snc"],
      consts["ctd"], consts["std"], consts["ctc"], consts["stc"])


def _attn_body(*refs, mode, n_back):
    if mode == "fox":
        qt_ref, k_ref, vt_ref, cq_ref, o_ref, qe_ref, m_ref, acc_ref, s_ref, tmax_ref = refs
    elif mode == "diff":
        qt_ref, k_ref, vt_ref, lam_ref, gcol_ref, o_ref, qe_ref, m_ref, acc_ref, s_ref, tmax_ref = refs
    else:
        qt_ref, k_ref, vt_ref, bias_ref, o_ref, qe_ref, m_ref, acc_ref, s_ref, tmax_ref = refs
    n_pairs, _, T = qt_ref.shape
    n_sc = qe_ref.shape[0]
    per_pair = n_sc // n_pairs
    i = pl.program_id(2)

    rows = LANES // per_pair
    zeros_q = jnp.zeros((LANES, T), BF16)
    for sc in range(n_sc):
        p, r = divmod(sc, per_pair)
        qe_ref[sc, 0:LANES, :] = zeros_q
        qe_ref[sc, rows * r:rows * (r + 1), :] = qt_ref[p, rows * r:rows * (r + 1), :]
    if mode == "fox":
        for sc in range(n_sc):
            p, h = divmod(sc, 2)
            terms = _split3(cq_ref[p, h:h + 1, :])
            blocks = [jnp.broadcast_to(t.astype(F32), (BF16_ROWS, T)) for t in terms]
            blocks += [jnp.full((BF16_ROWS, T), 1.0 if h == hh else 0.0, F32) for hh in range(2)]
            blocks += [jnp.zeros((LANES - 5 * BF16_ROWS, T), F32)]
            qe_ref[sc, LANES:2 * LANES, :] = jnp.concatenate(blocks, axis=0).astype(BF16)

    KT = s_ref.shape[1]
    n_sub = T // KT

    def produce(sc, j, u, qlo=0):
        rows_k = slice(KT * u, KT * (u + 1))
        s = _dot(k_ref[sc // per_pair, j, rows_k, :], qe_ref[sc, :, qlo:T])
        if mode == "dil":
            s = s + bias_ref[i - j, rows_k, qlo:T]
        s_ref[sc, :, qlo:T] = s
        tmax_ref[sc, :, qlo:T] = jnp.broadcast_to(jnp.max(s, axis=0, keepdims=True), (SUBLANES, T - qlo))

    def consume(sc, j, u, diag, qlo=0):
        s = s_ref[sc, :, qlo:T]
        if diag and mode != "dil":
            kk = lax.broadcasted_iota(jnp.int32, s.shape, 0) + KT * u
            qq = lax.broadcasted_iota(jnp.int32, s.shape, 1) + qlo
            s = jnp.where(kk <= qq, s, NEG_BIG)
            tmax = jnp.max(s, axis=0, keepdims=True)
        else:
            tmax = tmax_ref[sc, 0:1, qlo:T]
        p, r = divmod(sc, per_pair)
        h = r * 2 // per_pair
        m_old = m_ref[sc, 0:1, qlo:T]
        m_new = jnp.maximum(m_old, tmax)
        alpha = jnp.exp2(m_old - m_new)
        pr = jnp.exp2(s - m_new).astype(BF16)
        pv = _dot(vt_ref[p, j, V_ROWS * h:V_ROWS * (h + 1), KT * u:KT * (u + 1)], pr)
        acc_ref[sc, :, qlo:T] = acc_ref[sc, :, qlo:T] * alpha + pv
        m_ref[sc, :, qlo:T] = jnp.broadcast_to(m_new, (SUBLANES, T - qlo))

    m_ref[...] = jnp.full(m_ref.shape, NEG_BIG, F32)
    acc_ref[...] = jnp.zeros(acc_ref.shape, F32)

    units = [(u, sc) for u in range(n_sub) for sc in range(n_sc)]
    ahead = min(PIPELINE_AHEAD, len(units) - 1)

    def run_tile(j, diag):
        for w, (u, sc) in enumerate(units):
            if w + ahead < len(units):
                un, scn = units[w + ahead]
                produce(scn, j, un, KT * un if diag else 0)
            elif not diag:
                un, scn = units[w + ahead - len(units)]
                produce(scn, j + 1, un)
            consume(sc, j, u, diag, KT * u if diag else 0)

    lo = jnp.maximum(i - n_back, 0) if mode == "dil" else 0
    for un, scn in units[:ahead]:
        produce(scn, lo, un)

    def step(j, carry):
        run_tile(j, False)
        return carry

    lax.fori_loop(lo, i, step, 0)
    run_tile(i, True)

    def normalised(sc):
        a = acc_ref[sc]
        return a[0:HEAD_DIM] / a[HEAD_DIM:HEAD_DIM + 1]

    if mode == "diff":
        lam_init = lam_ref[4:5, 0:1]
        lam = (jnp.exp(jnp.sum(lam_ref[0:1] * lam_ref[1:2], axis=1, keepdims=True))
               - jnp.exp(jnp.sum(lam_ref[2:3] * lam_ref[3:4], axis=1, keepdims=True)) + lam_init)
    for p in range(n_pairs):
        outs = []
        for h in range(2):
            if mode == "diff":
                sc = per_pair * p + 2 * h
                d = normalised(sc) - lam * normalised(sc + 1)
                ms = jnp.mean(d * d, axis=0, keepdims=True)
                outs.append(d * lax.rsqrt(ms + SUBLN_EPS) * gcol_ref[...] * (1.0 - lam_init))
            else:
                outs.append(normalised(per_pair * p + h))
        o_ref[:, LANES * p:LANES * (p + 1)] = jnp.concatenate(outs, axis=0).T.astype(BF16)


def _attn_call(mode, qt, k, vt, extra, T, n_back=0, pairs_per_step=None, n_sub=1):
    b, npair, nt = qt.shape[:3]
    pps = npair if pairs_per_step is None else pairs_per_step
    r = k.shape[-1]
    n_sc = (4 if mode == "diff" else 2) * pps
    in_specs = [
        pl.BlockSpec((None, pps, None, LANES, T), lambda bi, p, i: (bi, p, i, 0, 0)),
        pl.BlockSpec((None, pps, nt, T, r), lambda bi, p, i: (bi, p, 0, 0, 0)),
        pl.BlockSpec((None, pps, nt, 2 * V_ROWS, T), lambda bi, p, i: (bi, p, 0, 0, 0)),
    ]
    if mode == "fox":
        in_specs.append(pl.BlockSpec((None, None, pps, GATE_STRIDE, T), lambda bi, p, i: (bi, i, p, 0, 0)))
    elif mode == "diff":
        in_specs += [_const_spec((8, LANES)), _const_spec((HEAD_DIM, 1))]
    else:
        in_specs.append(_const_spec(extra[0].shape))
    body = functools.partial(_attn_body, mode=mode, n_back=n_back)
    return pl.pallas_call(
        body,
        grid=(b, npair // pps, nt),
        in_specs=in_specs,
        out_specs=pl.BlockSpec((None, T, pps * LANES), lambda bi, p, i: (bi, i, p)),
        out_shape=jax.ShapeDtypeStruct((b, nt * T, npair * LANES), BF16),
        scratch_shapes=[pltpu.VMEM((n_sc, r, T), BF16),
                        pltpu.VMEM((n_sc, SUBLANES, T), F32),
                        pltpu.VMEM((n_sc, V_ROWS, T), F32),
                        pltpu.VMEM((n_sc, T // n_sub, T), F32),
                        pltpu.VMEM((n_sc, SUBLANES, T), F32)],
        compiler_params=pltpu.CompilerParams(dimension_semantics=("arbitrary", "arbitrary", "arbitrary"),
                                             vmem_limit_bytes=VMEM_LIMIT_BYTES),
        name="attn_" + mode,
    )(qt, k, vt, *extra)


def _constants(s, T):
    pos = np.arange(s, dtype=np.float64)[:, None]

    def angles(half):
        inv = ROPE_THETA ** (-np.arange(half, dtype=np.float64) / half)
        return pos * inv[None, :]

    ad, ac = angles(DIFF_QK_DIM // 2), angles(HEAD_DIM // 2)

    def nat(a):
        reps = LANES // (2 * a.shape[1])
        cos = np.tile(np.concatenate([np.cos(a), np.cos(a)], axis=1), (1, reps))
        sin = np.tile(np.concatenate([-np.sin(a), np.sin(a)], axis=1), (1, reps))
        return cos.astype(np.float32), sin.astype(np.float32)

    cnd, snd = nat(ad)
    cnc, snc = nat(ac)
    consts = dict(cnd=cnd, snd=snd, cnc=cnc, snc=snc,
                  ctd=np.cos(ad).T.astype(np.float32), std=np.sin(ad).T.astype(np.float32),
                  ctc=np.cos(ac).T.astype(np.float32), stc=np.sin(ac).T.astype(np.float32))
    consts["tri"] = np.triu(np.ones((T, T), np.float32))
    n_back = -(-max(w for w, _ in DILATED_PATTERNS) // T)
    kk = np.arange(T)[:, None]
    qq = np.arange(T)[None, :]
    bias = np.zeros((n_back + 1, T, T), np.float32)
    for o in range(n_back + 1):
        delta = o * T + qq - kk
        mult = np.zeros((T, T), np.float64)
        for w, dil in DILATED_PATTERNS:
            mult += (delta >= 0) & (delta <= w) & (delta % dil == 0)
        bias[o] = np.where(mult > 0, np.log2(np.maximum(mult, 1.0)), NEG_BIG)
    out = {}
    for name, v in consts.items():
        out[name] = jnp.asarray(v, BF16 if name == "tri" else F32)
    return out, jnp.asarray(bias), n_back


def _proj_weights(w_in, b_f):
    depth, d, _ = w_in.shape
    w = w_in.astype(BF16)
    sizes = [384] * 6 + [N_FOX_HEADS, 256, 256, 256]
    aq, ak, av, fq, fk, fv, fg, cq, ck, cv = jnp.split(w, np.cumsum(sizes)[:-1].tolist(), axis=2)
    rows = np.array([GATE_STRIDE * (h // 2) + h % 2 for h in range(N_FOX_HEADS)])
    fg_cols = jnp.zeros((depth, d, GATE_ROWS), BF16).at[:, :, rows].set(fg)
    bf_pad = jnp.zeros((depth, GATE_ROWS, 1), F32).at[:, rows, 0].set(b_f)
    wn = jnp.concatenate([ak, fk, ck], axis=2)
    wt = jnp.concatenate([aq, fq, cq, av, fv, cv, fg_cols], axis=2).transpose(0, 2, 1)
    return wn, wt, bf_pad


def _forward(x, w_in, b_f, lam_q1, lam_k1, lam_q2, lam_k2, g_sub, w_o, g_ffn1, w1_gate, w1_up, w1_down,
             g_mix, g_ffn2, w2_gate, w2_up, w2_down, g_final, T=512, tm=512):
    b, s, d = x.shape
    depth = w_in.shape[0]
    assert s % T == 0 and (b * s) % tm == 0 and d == HEAD_DIM * (N_DIFF_HEADS + N_FOX_HEADS + N_DIL_HEADS)
    consts, bias, n_back = _constants(s, T)
    wn_all, wt_all, bf_all = _proj_weights(w_in, b_f)
    x2d = x.reshape(b * s, d)
    for l in range(depth):
        lam_init = 0.8 - 0.6 * math.exp(-0.3 * l)
        if l == 0:
            x2d = _ffn_call(x2d, g_ffn1[l], w1_gate[l].astype(BF16), w1_up[l].astype(BF16),
                            w1_down[l].astype(BF16), tm=tm)
        (qtd, kd, vtd, qtf, kf, vtf, cqf, qtc, kc, vtc) = _proj_call(
            x2d.reshape(b, s, d), g_mix[l], wn_all[l], wt_all[l], bf_all[l], consts, T)
        lam_rows = jnp.zeros((8, LANES), F32)
        lam_rows = lam_rows.at[0:4, 0:DIFF_QK_DIM].set(jnp.stack([lam_q1[l], lam_k1[l], lam_q2[l], lam_k2[l]]))
        lam_rows = lam_rows.at[4, :].set(lam_init)
        o_a = _attn_call("diff", qtd, kd, vtd, (lam_rows, g_sub[l].reshape(HEAD_DIM, 1)), T)
        o_b = _attn_call("fox", qtf, kf, vtf, (cqf.reshape(b, s // T, 3, GATE_STRIDE, T),), T)
        o_c = _attn_call("dil", qtc, kc, vtc, (bias,), T, n_back=n_back)
        wo = w_o[l].astype(BF16)
        attn = ([o_a.reshape(b * s, -1), o_b.reshape(b * s, -1), o_c.reshape(b * s, -1)],
                [wo[0:384], wo[384:768], wo[768:1024]])
        last = l == depth - 1
        x2d = _ffn_call(x2d, g_ffn2[l], w2_gate[l].astype(BF16), w2_up[l].astype(BF16), w2_down[l].astype(BF16),
                        attn=attn, g_final=g_final if last else None, tm=tm)
        if not last:
            x2d = _ffn_call(x2d, g_ffn1[l + 1], w1_gate[l + 1].astype(BF16), w1_up[l + 1].astype(BF16),
                            w1_down[l + 1].astype(BF16), tm=tm)
    return x2d.reshape(b, s, d)


def kernel(x, w_in, b_f, lam_q1, lam_k1, lam_q2, lam_k2, g_sub, w_o, g_ffn1, w1_gate, w1_up, w1_down,
           g_mix, g_ffn2, w2_gate, w2_up, w2_down, g_final):
    return _forward(x, w_in, b_f, lam_q1, lam_k1, lam_q2, lam_k2, g_sub, w_o, g_ffn1, w1_gate, w1_up,
                    w1_down, g_mix, g_ffn2, w2_gate, w2_up, w2_down, g_final)
```

```python
import functools
import math

import numpy as np
import jax
import jax.numpy as jnp
from jax import lax
from jax.experimental import pallas as pl
from jax.experimental.pallas import tpu as pltpu

F32 = jnp.float32
BF16 = jnp.bfloat16

HEAD_DIM = 64
N_DIFF_HEADS = 6
N_FOX_HEADS = 6
N_DIL_HEADS = 4
DIFF_QK_DIM = 32
DILATED_PATTERNS = ((128, 1), (512, 4), (2048, 16))
ROPE_THETA = 10000.0
NORM_EPS = 1e-6
SUBLN_EPS = 1e-5
MACARON_SCALE = 0.5
NEG_BIG = -1e30
LOG2E = 1.4426950408889634

LANES = 128
SUBLANES = 8
MXU_WIDTH = 256
BF16_ROWS = 16
V_ROWS = HEAD_DIM + BF16_ROWS
GATE_STRIDE = 8
VMEM_LIMIT_BYTES = 56 * 1024 * 1024
PIPELINE_AHEAD = 2


def _dot(a, b):
    return jnp.dot(a, b, preferred_element_type=F32)


def _dot_nt(a, b):
    return lax.dot_general(a, b, (((1,), (1,)), ((), ())), preferred_element_type=F32)


def _rms(x, g, eps):
    ms = jnp.mean(x * x, axis=-1, keepdims=True)
    return x * lax.rsqrt(ms + eps) * g


def _split3(x):
    hi = x.astype(BF16)
    r1 = x - hi.astype(F32)
    mid = r1.astype(BF16)
    lo = (r1 - mid.astype(F32)).astype(BF16)
    return hi, mid, lo


def _ffn_body(*refs, has_attn, has_final, n_chunks):
    refs = list(refs)
    x_ref = refs.pop(0)
    if has_attn:
        oa_ref, ob_ref, oc_ref, woa_ref, wob_ref, woc_ref = refs[:6]
        refs = refs[6:]
    g_ref, wg_ref, wu_ref, wd_ref = refs[:4]
    refs = refs[4:]
    if has_final:
        gf_ref = refs.pop(0)
    out_ref = refs[0]

    x = x_ref[...]
    if has_attn:
        x = x + (_dot(oa_ref[...], woa_ref[...]) + _dot(ob_ref[...], wob_ref[...])
                 + _dot(oc_ref[...], woc_ref[...]))
    xn = _rms(x, g_ref[...], NORM_EPS).astype(BF16)
    n_wide = wg_ref.shape[1] // MXU_WIDTH
    bounds = [MXU_WIDTH * ((n_wide * c + n_chunks - 1) // n_chunks) for c in range(n_chunks + 1)]
    acc = None
    for c in range(n_chunks):
        sl = slice(bounds[c], bounds[c + 1])
        gate = _dot(xn, wg_ref[:, sl])
        up = _dot(xn, wu_ref[:, sl])
        h = (gate * jax.nn.sigmoid(gate) * up).astype(BF16)
        d = _dot(h, wd_ref[sl, :])
        acc = d if acc is None else acc + d
    y = x + MACARON_SCALE * acc
    if has_final:
        y = _rms(y, gf_ref[...], NORM_EPS)
    out_ref[...] = y


def _const_spec(shape):
    nd = len(shape)
    return pl.BlockSpec(shape, lambda *_: (0,) * nd, pipeline_mode=pl.Buffered(1))


def _ffn_call(x2d, g, wg, wu, wd, attn=None, g_final=None, tm=512):
    n, d = x2d.shape
    dff = wg.shape[1]
    args = [x2d]
    specs = [pl.BlockSpec((tm, d), lambda i: (i, 0))]
    if attn is not None:
        o_list, w_list = attn
        for o in o_list:
            args.append(o)
            specs.append(pl.BlockSpec((tm, o.shape[1]), lambda i: (i, 0)))
        for w in w_list:
            args.append(w)
            specs.append(_const_spec(w.shape))
    args += [g.reshape(1, d), wg, wu, wd]
    specs += [_const_spec((1, d)), _const_spec((d, dff)), _const_spec((d, dff)), _const_spec((dff, d))]
    if g_final is not None:
        args.append(g_final.reshape(1, d))
        specs.append(_const_spec((1, d)))
    assert dff % MXU_WIDTH == 0
    n_chunks = 2
    body = functools.partial(_ffn_body, has_attn=attn is not None, has_final=g_final is not None,
                             n_chunks=n_chunks)
    return pl.pallas_call(
        body,
        grid=(n // tm,),
        in_specs=specs,
        out_specs=pl.BlockSpec((tm, d), lambda i: (i, 0)),
        out_shape=jax.ShapeDtypeStruct((n, d), F32),
        compiler_params=pltpu.CompilerParams(dimension_semantics=("arbitrary",),
                                             vmem_limit_bytes=VMEM_LIMIT_BYTES),
        name="ffn",
    )(*args)


N_NAT = 384 + 384 + 256
N_TR = 2 * (384 + 384 + 256)
GATE_ROWS = 32


def _proj_body(x_ref, g_ref, wn_ref, wt_ref, bf_ref, tri_ref,
               cnd_ref, snd_ref, cnc_ref, snc_ref, ctd_ref, std_ref, ctc_ref, stc_ref,
               qtd_ref, kd_ref, vtd_ref, qtf_ref, kf_ref, vtf_ref, cq_ref, qtc_ref, kc_ref, vtc_ref,
               carry_ref):
    tm = x_ref.shape[0]

    @pl.when(pl.program_id(1) == 0)
    def _():
        carry_ref[...] = jnp.zeros_like(carry_ref)

    xn = _rms(x_ref[...], g_ref[...], NORM_EPS).astype(BF16)
    rn = _dot(xn, wn_ref[...])
    rt = _dot_nt(wt_ref[...], xn)

    lane = lax.broadcasted_iota(jnp.int32, (tm, LANES), 1)

    def rope_nat(blk, half, cos, sin_signed):
        first = (lane & half) == 0
        rot = jnp.where(first, pltpu.roll(blk, LANES - half, 1), pltpu.roll(blk, half, 1))
        return (blk * cos + rot * sin_signed).astype(BF16)

    for p in range(3):
        kd_ref[p] = rope_nat(rn[:, LANES * p:LANES * (p + 1)], DIFF_QK_DIM // 2, cnd_ref[...], snd_ref[...])
    for p in range(2):
        c0 = 768 + LANES * p
        kc_ref[p] = rope_nat(rn[:, c0:c0 + LANES], HEAD_DIM // 2, cnc_ref[...], snc_ref[...])

    def rope_tr(ref, p, r0, x1, x2, cos, sin, scale):
        half = x1.shape[0]
        ref[p, r0:r0 + half, :] = ((x1 * cos - x2 * sin) * scale).astype(BF16)
        ref[p, r0 + half:r0 + 2 * half, :] = ((x2 * cos + x1 * sin) * scale).astype(BF16)

    sd = DIFF_QK_DIM ** -0.5 * LOG2E
    hd = DIFF_QK_DIM // 2
    for g in range(2 * N_DIFF_HEADS):
        b0 = DIFF_QK_DIM * g
        rope_tr(qtd_ref, g // 4, DIFF_QK_DIM * (g % 4), rt[b0:b0 + hd], rt[b0 + hd:b0 + 2 * hd],
                ctd_ref[...], std_ref[...], sd)
    sf = HEAD_DIM ** -0.5 * LOG2E
    for p in range(3):
        b0 = 384 + LANES * p
        qtf_ref[p] = (rt[b0:b0 + LANES] * sf).astype(BF16)
    hc = HEAD_DIM // 2
    for h in range(N_DIL_HEADS):
        b0 = 768 + HEAD_DIM * h
        rope_tr(qtc_ref, h // 2, HEAD_DIM * (h % 2), rt[b0:b0 + hc], rt[b0 + hc:b0 + 2 * hc],
                ctc_ref[...], stc_ref[...], sf)

    ones_rows = jnp.ones((BF16_ROWS, tm), BF16)

    def put_vt(ref, base, n_pairs):
        for p in range(n_pairs):
            for h in range(2):
                b0 = base + LANES * p + HEAD_DIM * h
                ref[p, V_ROWS * h:V_ROWS * h + HEAD_DIM, :] = rt[b0:b0 + HEAD_DIM].astype(BF16)
                ref[p, V_ROWS * h + HEAD_DIM:V_ROWS * (h + 1), :] = ones_rows

    put_vt(vtd_ref, 1024, 3)
    put_vt(vtf_ref, 1408, 3)
    put_vt(vtc_ref, 1792, 2)

    z = rt[N_TR:N_TR + GATE_ROWS] + bf_ref[...]
    logf = (jnp.minimum(z, 0.0) - jnp.log1p(jnp.exp(-jnp.abs(z)))) * LOG2E
    part = _dot(jnp.concatenate(_split3(logf), axis=0), tri_ref[...])
    c = part[0:GATE_ROWS] + part[GATE_ROWS:2 * GATE_ROWS] + part[2 * GATE_ROWS:3 * GATE_ROWS] + carry_ref[...]
    carry_ref[...] = c[:, tm - 1:tm]
    cq_ref[...] = c[0:3 * GATE_STRIDE]
    row = lax.broadcasted_iota(jnp.int32, (LANES, tm), 0)
    ones_pattern = jnp.where((row < 3 * BF16_ROWS) & ((row & (BF16_ROWS - 1)) == 0), 1.0, 0.0)
    for p in range(3):
        kf_ref[p, :, 0:LANES] = rn[:, 384 + LANES * p:384 + LANES * (p + 1)].astype(BF16)
        auxt = ones_pattern
        for h in range(2):
            ch = c[GATE_STRIDE * p + h:GATE_STRIDE * p + h + 1]
            for t, term in enumerate(_split3(ch)):
                auxt = jnp.where(row == (3 + h) * BF16_ROWS + t, -term.astype(F32), auxt)
        kf_ref[p, :, LANES:2 * LANES] = auxt.T.astype(BF16)


def _proj_call(x, g_mix, wn, wt, bf_pad, consts, T):
    b, s, d = x.shape
    nt = s // T
    tile5 = lambda npair, r, c: pl.BlockSpec((None, npair, None, r, c), lambda bi, ti: (bi, 0, ti, 0, 0))
    nat_tab = pl.BlockSpec((T, LANES), lambda bi, ti: (ti, 0))
    tr_tab = lambda r: pl.BlockSpec((r, T), lambda bi, ti: (0, ti))
    in_specs = [
        pl.BlockSpec((None, T, d), lambda bi, ti: (bi, ti, 0)),
        _const_spec((1, d)), _const_spec(wn.shape), _const_spec(wt.shape), _const_spec((GATE_ROWS, 1)),
        _const_spec((T, T)),
        nat_tab, nat_tab, nat_tab, nat_tab,
        tr_tab(DIFF_QK_DIM // 2), tr_tab(DIFF_QK_DIM // 2), tr_tab(HEAD_DIM // 2), tr_tab(HEAD_DIM // 2),
    ]
    out_shape = [
        jax.ShapeDtypeStruct((b, 3, nt, LANES, T), BF16),
        jax.ShapeDtypeStruct((b, 3, nt, T, LANES), BF16),
        jax.ShapeDtypeStruct((b, 3, nt, 2 * V_ROWS, T), BF16),
        jax.ShapeDtypeStruct((b, 3, nt, LANES, T), BF16),
        jax.ShapeDtypeStruct((b, 3, nt, T, 2 * LANES), BF16),
        jax.ShapeDtypeStruct((b, 3, nt, 2 * V_ROWS, T), BF16),
        jax.ShapeDtypeStruct((b, nt, 3 * GATE_STRIDE, T), F32),
        jax.ShapeDtypeStruct((b, 2, nt, LANES, T), BF16),
        jax.ShapeDtypeStruct((b, 2, nt, T, LANES), BF16),
        jax.ShapeDtypeStruct((b, 2, nt, 2 * V_ROWS, T), BF16),
    ]
    out_specs = [
        tile5(3, LANES, T), tile5(3, T, LANES), tile5(3, 2 * V_ROWS, T),
        tile5(3, LANES, T), tile5(3, T, 2 * LANES), tile5(3, 2 * V_ROWS, T),
        pl.BlockSpec((None, None, 3 * GATE_STRIDE, T), lambda bi, ti: (bi, ti, 0, 0)),
        tile5(2, LANES, T), tile5(2, T, LANES), tile5(2, 2 * V_ROWS, T),
    ]
    return pl.pallas_call(
        _proj_body,
        grid=(b, nt),
        in_specs=in_specs,
        out_specs=out_specs,
        out_shape=out_shape,
        scratch_shapes=[pltpu.VMEM((GATE_ROWS, 1), F32)],
        compiler_params=pltpu.CompilerParams(dimension_semantics=("arbitrary", "arbitrary"),
                                             vmem_limit_bytes=VMEM_LIMIT_BYTES),
        name="proj",
    )(x, g_mix.reshape(1, d), wn, wt, bf_pad, consts["tri"], consts["cnd"], consts["snd"], consts["cnc"], consts["snc"],
      consts["ctd"], consts["std"], consts["ctc"], consts["stc"])


def _attn_body(*refs, mode, n_back):
    if mode == "fox":
        qt_ref, k_ref, vt_ref, cq_ref, o_ref, qe_ref, m_ref, acc_ref, s_ref, tmax_ref = refs
    elif mode == "diff":
        qt_ref, k_ref, vt_ref, lam_ref, gcol_ref, o_ref, qe_ref, m_ref, acc_ref, s_ref, tmax_ref = refs
    else:
        qt_ref, k_ref, vt_ref, bias_ref, o_ref, qe_ref, m_ref, acc_ref, s_ref, tmax_ref = refs
    n_pairs, nt, _, T = qt_ref.shape
    n_sc = s_ref.shape[0]
    per_pair = n_sc // n_pairs
    rows = LANES // per_pair
    q_tiles = (pl.program_id(2), nt - 1 - pl.program_id(2))

    def first_tile(i):
        return jnp.maximum(i - n_back, 0) if mode == "dil" else 0

    def setup(st, i):
        zeros_q = jnp.zeros((LANES, T), BF16)
        for sc in range(n_sc):
            p, r = divmod(sc, per_pair)
            qe_ref[st, sc, 0:LANES, :] = zeros_q
            qe_ref[st, sc, rows * r:rows * (r + 1), :] = qt_ref[p, i, rows * r:rows * (r + 1), :]
        if mode == "fox":
            for sc in range(n_sc):
                p, h = divmod(sc, 2)
                terms = _split3(cq_ref[i, p, h:h + 1, :])
                blocks = [jnp.broadcast_to(t.astype(F32), (BF16_ROWS, T)) for t in terms]
                blocks += [jnp.full((BF16_ROWS, T), 1.0 if h == hh else 0.0, F32) for hh in range(2)]
                blocks += [jnp.zeros((LANES - 5 * BF16_ROWS, T), F32)]
                qe_ref[st, sc, LANES:2 * LANES, :] = jnp.concatenate(blocks, axis=0).astype(BF16)
        m_ref[st] = jnp.full(m_ref.shape[1:], NEG_BIG, F32)
        acc_ref[st] = jnp.zeros(acc_ref.shape[1:], F32)

    def produce(st, sc, j):
        s = _dot(k_ref[sc // per_pair, j], qe_ref[st, sc])
        if mode == "dil":
            s = s + bias_ref[q_tiles[st] - j]
        s_ref[sc] = s
        tmax_ref[sc] = jnp.broadcast_to(jnp.max(s, axis=0, keepdims=True), (SUBLANES, T))

    def consume(st, sc, j, diag):
        s = s_ref[sc]
        if diag and mode != "dil":
            kk = lax.broadcasted_iota(jnp.int32, s.shape, 0)
            qq = lax.broadcasted_iota(jnp.int32, s.shape, 1)
            s = jnp.where(kk <= qq, s, NEG_BIG)
            tmax = jnp.max(s, axis=0, keepdims=True)
        else:
            tmax = tmax_ref[sc, 0:1, :]
        p, r = divmod(sc, per_pair)
        h = r * 2 // per_pair
        m_old = m_ref[st, sc, 0:1, :]
        m_new = jnp.maximum(m_old, tmax)
        alpha = jnp.exp2(m_old - m_new)
        pr = jnp.exp2(s - m_new).astype(BF16)
        pv = _dot(vt_ref[p, j, V_ROWS * h:V_ROWS * (h + 1), :], pr)
        acc_ref[st, sc] = acc_ref[st, sc] * alpha + pv
        m_ref[st, sc] = jnp.broadcast_to(m_new, (SUBLANES, T))

    ahead = min(PIPELINE_AHEAD, n_sc - 1)

    def run_tile(st, j, diag):
        for sc in range(n_sc):
            if sc + ahead < n_sc:
                produce(st, sc + ahead, j)
            elif not diag:
                produce(st, sc + ahead - n_sc, j + 1)
            elif st == 0:
                produce(1, sc + ahead - n_sc, first_tile(q_tiles[1]))
            consume(st, sc, j, diag)

    def finalize(st, i):
        def normalised(sc):
            a = acc_ref[st, sc]
            return a[0:HEAD_DIM] / a[HEAD_DIM:HEAD_DIM + 1]

        if mode == "diff":
            lam_init = lam_ref[4:5, 0:1]
            lam = (jnp.exp(jnp.sum(lam_ref[0:1] * lam_ref[1:2], axis=1, keepdims=True))
                   - jnp.exp(jnp.sum(lam_ref[2:3] * lam_ref[3:4], axis=1, keepdims=True)) + lam_init)
        for p in range(n_pairs):
            outs = []
            for h in range(2):
                if mode == "diff":
                    sc = per_pair * p + 2 * h
                    d = normalised(sc) - lam * normalised(sc + 1)
                    ms = jnp.mean(d * d, axis=0, keepdims=True)
                    outs.append(d * lax.rsqrt(ms + SUBLN_EPS) * gcol_ref[...] * (1.0 - lam_init))
                else:
                    outs.append(normalised(per_pair * p + h))
            o_ref[pl.ds(pl.multiple_of(i * T, T), T), LANES * p:LANES * (p + 1)] = (
                jnp.concatenate(outs, axis=0).T.astype(BF16))

    for st in range(2):
        setup(st, q_tiles[st])
    for sc in range(ahead):
        produce(0, sc, first_tile(q_tiles[0]))
    for st in range(2):
        i = q_tiles[st]

        def step(j, carry, st=st):
            run_tile(st, j, False)
            return carry

        lax.fori_loop(first_tile(i), i, step, 0)
        run_tile(st, i, True)
    for st in range(2):
        finalize(st, q_tiles[st])


def _attn_call(mode, qt, k, vt, extra, T, n_back=0, pairs_per_step=None):
    b, npair, nt = qt.shape[:3]
    assert nt % 2 == 0
    pps = npair if pairs_per_step is None else pairs_per_step
    r = k.shape[-1]
    n_sc = (4 if mode == "diff" else 2) * pps
    in_specs = [
        pl.BlockSpec((None, pps, nt, LANES, T), lambda bi, p, i: (bi, p, 0, 0, 0)),
        pl.BlockSpec((None, pps, nt, T, r), lambda bi, p, i: (bi, p, 0, 0, 0)),
        pl.BlockSpec((None, pps, nt, 2 * V_ROWS, T), lambda bi, p, i: (bi, p, 0, 0, 0)),
    ]
    if mode == "fox":
        in_specs.append(pl.BlockSpec((None, nt, pps, GATE_STRIDE, T), lambda bi, p, i: (bi, 0, p, 0, 0)))
    elif mode == "diff":
        in_specs += [_const_spec((8, LANES)), _const_spec((HEAD_DIM, 1))]
    else:
        in_specs.append(_const_spec(extra[0].shape))
    body = functools.partial(_attn_body, mode=mode, n_back=n_back)
    return pl.pallas_call(
        body,
        grid=(b, npair // pps, nt // 2),
        in_specs=in_specs,
        out_specs=pl.BlockSpec((None, nt * T, pps * LANES), lambda bi, p, i: (bi, 0, p)),
        out_shape=jax.ShapeDtypeStruct((b, nt * T, npair * LANES), BF16),
        scratch_shapes=[pltpu.VMEM((2, n_sc, r, T), BF16),
                        pltpu.VMEM((2, n_sc, SUBLANES, T), F32),
                        pltpu.VMEM((2, n_sc, V_ROWS, T), F32),
                        pltpu.VMEM((n_sc, T, T), F32),
                        pltpu.VMEM((n_sc, SUBLANES, T), F32)],
        compiler_params=pltpu.CompilerParams(dimension_semantics=("arbitrary", "arbitrary", "arbitrary"),
                                             vmem_limit_bytes=VMEM_LIMIT_BYTES),
        name="attn_" + mode,
    )(qt, k, vt, *extra)


def _constants(s, T):
    pos = np.arange(s, dtype=np.float64)[:, None]

    def angles(half):
        inv = ROPE_THETA ** (-np.arange(half, dtype=np.float64) / half)
        return pos * inv[None, :]

    ad, ac = angles(DIFF_QK_DIM // 2), angles(HEAD_DIM // 2)

    def nat(a):
        reps = LANES // (2 * a.shape[1])
        cos = np.tile(np.concatenate([np.cos(a), np.cos(a)], axis=1), (1, reps))
        sin = np.tile(np.concatenate([-np.sin(a), np.sin(a)], axis=1), (1, reps))
        return cos.astype(np.float32), sin.astype(np.float32)

    cnd, snd = nat(ad)
    cnc, snc = nat(ac)
    consts = dict(cnd=cnd, snd=snd, cnc=cnc, snc=snc,
                  ctd=np.cos(ad).T.astype(np.float32), std=np.sin(ad).T.astype(np.float32),
                  ctc=np.cos(ac).T.astype(np.float32), stc=np.sin(ac).T.astype(np.float32))
    consts["tri"] = np.triu(np.ones((T, T), np.float32))
    n_back = -(-max(w for w, _ in DILATED_PATTERNS) // T)
    kk = np.arange(T)[:, None]
    qq = np.arange(T)[None, :]
    bias = np.zeros((n_back + 1, T, T), np.float32)
    for o in range(n_back + 1):
        delta = o * T + qq - kk
        mult = np.zeros((T, T), np.float64)
        for w, dil in DILATED_PATTERNS:
            mult += (delta >= 0) & (delta <= w) & (delta % dil == 0)
        bias[o] = np.where(mult > 0, np.log2(np.maximum(mult, 1.0)), NEG_BIG)
    out = {}
    for name, v in consts.items():
        out[name] = jnp.asarray(v, BF16 if name == "tri" else F32)
    return out, jnp.asarray(bias), n_back


def _proj_weights(w_in, b_f):
    depth, d, _ = w_in.shape
    w = w_in.astype(BF16)
    sizes = [384] * 6 + [N_FOX_HEADS, 256, 256, 256]
    aq, ak, av, fq, fk, fv, fg, cq, ck, cv = jnp.split(w, np.cumsum(sizes)[:-1].tolist(), axis=2)
    rows = np.array([GATE_STRIDE * (h // 2) + h % 2 for h in range(N_FOX_HEADS)])
    fg_cols = jnp.zeros((depth, d, GATE_ROWS), BF16).at[:, :, rows].set(fg)
    bf_pad = jnp.zeros((depth, GATE_ROWS, 1), F32).at[:, rows, 0].set(b_f)
    wn = jnp.concatenate([ak, fk, ck], axis=2)
    wt = jnp.concatenate([aq, fq, cq, av, fv, cv, fg_cols], axis=2).transpose(0, 2, 1)
    return wn, wt, bf_pad


def _forward(x, w_in, b_f, lam_q1, lam_k1, lam_q2, lam_k2, g_sub, w_o, g_ffn1, w1_gate, w1_up, w1_down,
             g_mix, g_ffn2, w2_gate, w2_up, w2_down, g_final, T=512, tm=512):
    b, s, d = x.shape
    depth = w_in.shape[0]
    assert s % T == 0 and (b * s) % tm == 0 and d == HEAD_DIM * (N_DIFF_HEADS + N_FOX_HEADS + N_DIL_HEADS)
    consts, bias, n_back = _constants(s, T)
    wn_all, wt_all, bf_all = _proj_weights(w_in, b_f)
    x2d = x.reshape(b * s, d)
    for l in range(depth):
        lam_init = 0.8 - 0.6 * math.exp(-0.3 * l)
        if l == 0:
            x2d = _ffn_call(x2d, g_ffn1[l], w1_gate[l].astype(BF16), w1_up[l].astype(BF16),
                            w1_down[l].astype(BF16), tm=tm)
        (qtd, kd, vtd, qtf, kf, vtf, cqf, qtc, kc, vtc) = _proj_call(
            x2d.reshape(b, s, d), g_mix[l], wn_all[l], wt_all[l], bf_all[l], consts, T)
        lam_rows = jnp.zeros((8, LANES), F32)
        lam_rows = lam_rows.at[0:4, 0:DIFF_QK_DIM].set(jnp.stack([lam_q1[l], lam_k1[l], lam_q2[l], lam_k2[l]]))
        lam_rows = lam_rows.at[4, :].set(lam_init)
        o_a = _attn_call("diff", qtd, kd, vtd, (lam_rows, g_sub[l].reshape(HEAD_DIM, 1)), T)
        o_b = _attn_call("fox", qtf, kf, vtf, (cqf.reshape(b, s // T, 3, GATE_STRIDE, T),), T)
        o_c = _attn_call("dil", qtc, kc, vtc, (bias,), T, n_back=n_back)
        wo = w_o[l].astype(BF16)
        attn = ([o_a.reshape(b * s, -1), o_b.reshape(b * s, -1), o_c.reshape(b * s, -1)],
                [wo[0:384], wo[384:768], wo[768:1024]])
        last = l == depth - 1
        x2d = _ffn_call(x2d, g_ffn2[l], w2_gate[l].astype(BF16), w2_up[l].astype(BF16), w2_down[l].astype(BF16),
                        attn=attn, g_final=g_final if last else None, tm=tm)
        if not last:
            x2d = _ffn_call(x2d, g_ffn1[l + 1], w1_gate[l + 1].astype(BF16), w1_up[l + 1].astype(BF16),
                            w1_down[l + 1].astype(BF16), tm=tm)
    return x2d.reshape(b, s, d)


def kernel(x, w_in, b_f, lam_q1, lam_k1, lam_q2, lam_k2, g_sub, w_o, g_ffn1, w1_gate, w1_up, w1_down,
           g_mix, g_ffn2, w2_gate, w2_up, w2_down, g_final):
    return _forward(x, w_in, b_f, lam_q1, lam_k1, lam_q2, lam_k2, g_sub, w_o, g_ffn1, w1_gate, w1_up,
                    w1_down, g_mix, g_ffn2, w2_gate, w2_up, w2_down, g_final)
```

```python
import functools
import math

import numpy as np
import jax
import jax.numpy as jnp
from jax import lax
from jax.experimental import pallas as pl
from jax.experimental.pallas import tpu as pltpu

F32 = jnp.float32
BF16 = jnp.bfloat16

HEAD_DIM = 64
N_DIFF_HEADS = 6
N_FOX_HEADS = 6
N_DIL_HEADS = 4
DIFF_QK_DIM = 32
DILATED_PATTERNS = ((128, 1), (512, 4), (2048, 16))
ROPE_THETA = 10000.0
NORM_EPS = 1e-6
SUBLN_EPS = 1e-5
MACARON_SCALE = 0.5
NEG_BIG = -1e30
LOG2E = 1.4426950408889634

LANES = 128
SUBLANES = 8
MXU_WIDTH = 256
BF16_ROWS = 16
V_ROWS = HEAD_DIM + BF16_ROWS
GATE_STRIDE = 8
VMEM_LIMIT_BYTES = 56 * 1024 * 1024
PIPELINE_AHEAD = 2


def _dot(a, b):
    return jnp.dot(a, b, preferred_element_type=F32)


def _dot_nt(a, b):
    return lax.dot_general(a, b, (((1,), (1,)), ((), ())), preferred_element_type=F32)


def _rms(x, g, eps):
    ms = jnp.mean(x * x, axis=-1, keepdims=True)
    return x * lax.rsqrt(ms + eps) * g


def _split3(x):
    hi = x.astype(BF16)
    r1 = x - hi.astype(F32)
    mid = r1.astype(BF16)
    lo = (r1 - mid.astype(F32)).astype(BF16)
    return hi, mid, lo


def _ffn_body(*refs, has_attn, has_final, n_chunks):
    refs = list(refs)
    x_ref = refs.pop(0)
    if has_attn:
        oa_ref, ob_ref, oc_ref, wo_ref = refs[:4]
        refs = refs[4:]
    g_ref, wg_ref, wu_ref, wd_ref = refs[:4]
    refs = refs[4:]
    if has_final:
        gf_ref = refs.pop(0)
    out_ref = refs[0]

    x = x_ref[...]
    if has_attn:
        o = jnp.concatenate([oa_ref[...], ob_ref[...], oc_ref[...]], axis=1)
        x = x + _dot(o, wo_ref[...])
    xn = _rms(x, g_ref[...], NORM_EPS).astype(BF16)
    n_wide = wg_ref.shape[1] // MXU_WIDTH
    bounds = [MXU_WIDTH * ((n_wide * c + n_chunks - 1) // n_chunks) for c in range(n_chunks + 1)]
    acc = None
    for c in range(n_chunks):
        sl = slice(bounds[c], bounds[c + 1])
        gate = _dot(xn, wg_ref[:, sl])
        up = _dot(xn, wu_ref[:, sl])
        h = (gate * jax.nn.sigmoid(gate) * up).astype(BF16)
        d = _dot(h, wd_ref[sl, :])
        acc = d if acc is None else acc + d
    y = x + MACARON_SCALE * acc
    if has_final:
        y = _rms(y, gf_ref[...], NORM_EPS)
    out_ref[...] = y


def _const_spec(shape):
    nd = len(shape)
    return pl.BlockSpec(shape, lambda *_: (0,) * nd, pipeline_mode=pl.Buffered(1))


def _layer_spec(shape, l):
    nd = len(shape)
    return pl.BlockSpec((None,) + tuple(shape[1:]), lambda *_: (l,) + (0,) * (nd - 1),
                        pipeline_mode=pl.Buffered(1))


def _ffn_call(x2d, l, g, wg, wu, wd, attn=None, g_final=None, tm=512):
    n, d = x2d.shape
    dff = wg.shape[2]
    args = [x2d]
    specs = [pl.BlockSpec((tm, d), lambda i: (i, 0))]
    if attn is not None:
        o_list, wo = attn
        for o in o_list:
            args.append(o)
            specs.append(pl.BlockSpec((tm, o.shape[1]), lambda i: (i, 0)))
        args.append(wo)
        specs.append(_layer_spec(wo.shape, l))
    args += [g, wg, wu, wd]
    specs += [_layer_spec(g.shape, l), _layer_spec(wg.shape, l), _layer_spec(wu.shape, l),
              _layer_spec(wd.shape, l)]
    if g_final is not None:
        args.append(g_final.reshape(1, d))
        specs.append(_const_spec((1, d)))
    assert dff % MXU_WIDTH == 0
    n_chunks = 2
    body = functools.partial(_ffn_body, has_attn=attn is not None, has_final=g_final is not None,
                             n_chunks=n_chunks)
    return pl.pallas_call(
        body,
        grid=(n // tm,),
        in_specs=specs,
        out_specs=pl.BlockSpec((tm, d), lambda i: (i, 0)),
        out_shape=jax.ShapeDtypeStruct((n, d), F32),
        compiler_params=pltpu.CompilerParams(dimension_semantics=("arbitrary",),
                                             vmem_limit_bytes=VMEM_LIMIT_BYTES),
        name="ffn",
    )(*args)


N_NAT = 384 + 384 + 256
N_TR = 2 * (384 + 384 + 256)
GATE_ROWS = 32


def _proj_body(x_ref, g_ref, wn_ref, wt_ref, bf_ref, tri_ref,
               cnd_ref, snd_ref, cnc_ref, snc_ref, ctd_ref, std_ref, ctc_ref, stc_ref,
               qtd_ref, kd_ref, vtd_ref, qtf_ref, kf_ref, vtf_ref, cq_ref, qtc_ref, kc_ref, vtc_ref,
               carry_ref):
    tm = x_ref.shape[0]

    @pl.when(pl.program_id(1) == 0)
    def _():
        carry_ref[...] = jnp.zeros_like(carry_ref)

    xn = _rms(x_ref[...], g_ref[...], NORM_EPS).astype(BF16)
    rn = _dot(xn, wn_ref[...])
    rt = _dot_nt(wt_ref[...], xn)

    lane = lax.broadcasted_iota(jnp.int32, (tm, LANES), 1)

    def rope_nat(blk, half, cos, sin_signed):
        first = (lane & half) == 0
        rot = jnp.where(first, pltpu.roll(blk, LANES - half, 1), pltpu.roll(blk, half, 1))
        return (blk * cos + rot * sin_signed).astype(BF16)

    for p in range(3):
        kd_ref[p] = rope_nat(rn[:, LANES * p:LANES * (p + 1)], DIFF_QK_DIM // 2, cnd_ref[...], snd_ref[...])
    for p in range(2):
        c0 = 768 + LANES * p
        kc_ref[p] = rope_nat(rn[:, c0:c0 + LANES], HEAD_DIM // 2, cnc_ref[...], snc_ref[...])

    def rope_tr(ref, p, r0, x1, x2, cos, sin, scale):
        half = x1.shape[0]
        ref[p, r0:r0 + half, :] = ((x1 * cos - x2 * sin) * scale).astype(BF16)
        ref[p, r0 + half:r0 + 2 * half, :] = ((x2 * cos + x1 * sin) * scale).astype(BF16)

    sd = DIFF_QK_DIM ** -0.5 * LOG2E
    hd = DIFF_QK_DIM // 2
    for g in range(2 * N_DIFF_HEADS):
        b0 = DIFF_QK_DIM * g
        rope_tr(qtd_ref, g // 4, DIFF_QK_DIM * (g % 4), rt[b0:b0 + hd], rt[b0 + hd:b0 + 2 * hd],
                ctd_ref[...], std_ref[...], sd)
    sf = HEAD_DIM ** -0.5 * LOG2E
    for p in range(3):
        b0 = 384 + LANES * p
        qtf_ref[p] = (rt[b0:b0 + LANES] * sf).astype(BF16)
    hc = HEAD_DIM // 2
    for h in range(N_DIL_HEADS):
        b0 = 768 + HEAD_DIM * h
        rope_tr(qtc_ref, h // 2, HEAD_DIM * (h % 2), rt[b0:b0 + hc], rt[b0 + hc:b0 + 2 * hc],
                ctc_ref[...], stc_ref[...], sf)

    ones_rows = jnp.ones((BF16_ROWS, tm), BF16)

    def put_vt(ref, base, n_pairs):
        for p in range(n_pairs):
            for h in range(2):
                b0 = base + LANES * p + HEAD_DIM * h
                ref[p, V_ROWS * h:V_ROWS * h + HEAD_DIM, :] = rt[b0:b0 + HEAD_DIM].astype(BF16)
                ref[p, V_ROWS * h + HEAD_DIM:V_ROWS * (h + 1), :] = ones_rows

    put_vt(vtd_ref, 1024, 3)
    put_vt(vtf_ref, 1408, 3)
    put_vt(vtc_ref, 1792, 2)

    z = rt[N_TR:N_TR + GATE_ROWS] + bf_ref[...]
    logf = (jnp.minimum(z, 0.0) - jnp.log1p(jnp.exp(-jnp.abs(z)))) * LOG2E
    part = _dot(jnp.concatenate(_split3(logf), axis=0), tri_ref[...])
    c = part[0:GATE_ROWS] + part[GATE_ROWS:2 * GATE_ROWS] + part[2 * GATE_ROWS:3 * GATE_ROWS] + carry_ref[...]
    carry_ref[...] = c[:, tm - 1:tm]
    cq_ref[...] = c[0:3 * GATE_STRIDE]
    row = lax.broadcasted_iota(jnp.int32, (LANES, tm), 0)
    ones_pattern = jnp.where((row < 3 * BF16_ROWS) & ((row & (BF16_ROWS - 1)) == 0), 1.0, 0.0)
    for p in range(3):
        kf_ref[p, :, 0:LANES] = rn[:, 384 + LANES * p:384 + LANES * (p + 1)].astype(BF16)
        auxt = ones_pattern
        for h in range(2):
            ch = c[GATE_STRIDE * p + h:GATE_STRIDE * p + h + 1]
            for t, term in enumerate(_split3(ch)):
                auxt = jnp.where(row == (3 + h) * BF16_ROWS + t, -term.astype(F32), auxt)
        kf_ref[p, :, LANES:2 * LANES] = auxt.T.astype(BF16)


def _proj_call(x, l, g_mix, wn, wt, bf_pad, consts, T):
    b, s, d = x.shape
    nt = s // T
    tile5 = lambda npair, r, c: pl.BlockSpec((None, npair, None, r, c), lambda bi, ti: (bi, 0, ti, 0, 0))
    nat_tab = pl.BlockSpec((T, LANES), lambda bi, ti: (ti, 0))
    tr_tab = lambda r: pl.BlockSpec((r, T), lambda bi, ti: (0, ti))
    in_specs = [
        pl.BlockSpec((None, T, d), lambda bi, ti: (bi, ti, 0)),
        _layer_spec(g_mix.shape, l), _layer_spec(wn.shape, l), _layer_spec(wt.shape, l),
        _layer_spec(bf_pad.shape, l), _const_spec((T, T)),
        nat_tab, nat_tab, nat_tab, nat_tab,
        tr_tab(DIFF_QK_DIM // 2), tr_tab(DIFF_QK_DIM // 2), tr_tab(HEAD_DIM // 2), tr_tab(HEAD_DIM // 2),
    ]
    out_shape = [
        jax.ShapeDtypeStruct((b, 3, nt, LANES, T), BF16),
        jax.ShapeDtypeStruct((b, 3, nt, T, LANES), BF16),
        jax.ShapeDtypeStruct((b, 3, nt, 2 * V_ROWS, T), BF16),
        jax.ShapeDtypeStruct((b, 3, nt, LANES, T), BF16),
        jax.ShapeDtypeStruct((b, 3, nt, T, 2 * LANES), BF16),
        jax.ShapeDtypeStruct((b, 3, nt, 2 * V_ROWS, T), BF16),
        jax.ShapeDtypeStruct((b, nt, 3 * GATE_STRIDE, T), F32),
        jax.ShapeDtypeStruct((b, 2, nt, LANES, T), BF16),
        jax.ShapeDtypeStruct((b, 2, nt, T, LANES), BF16),
        jax.ShapeDtypeStruct((b, 2, nt, 2 * V_ROWS, T), BF16),
    ]
    out_specs = [
        tile5(3, LANES, T), tile5(3, T, LANES), tile5(3, 2 * V_ROWS, T),
        tile5(3, LANES, T), tile5(3, T, 2 * LANES), tile5(3, 2 * V_ROWS, T),
        pl.BlockSpec((None, None, 3 * GATE_STRIDE, T), lambda bi, ti: (bi, ti, 0, 0)),
        tile5(2, LANES, T), tile5(2, T, LANES), tile5(2, 2 * V_ROWS, T),
    ]
    return pl.pallas_call(
        _proj_body,
        grid=(b, nt),
        in_specs=in_specs,
        out_specs=out_specs,
        out_shape=out_shape,
        scratch_shapes=[pltpu.VMEM((GATE_ROWS, 1), F32)],
        compiler_params=pltpu.CompilerParams(dimension_semantics=("arbitrary", "arbitrary"),
                                             vmem_limit_bytes=VMEM_LIMIT_BYTES),
        name="proj",
    )(x, g_mix, wn, wt, bf_pad, consts["tri"], consts["cnd"], consts["snd"], consts["cnc"], consts["snc"],
      consts["ctd"], consts["std"], consts["ctc"], consts["stc"])


def _attn_body(*refs, mode, n_back):
    if mode == "fox":
        qt_ref, k_ref, vt_ref, cq_ref, o_ref, qe_ref, m_ref, acc_ref, s_ref, tmax_ref = refs
    elif mode == "diff":
        qt_ref, k_ref, vt_ref, lam_ref, gcol_ref, o_ref, qe_ref, m_ref, acc_ref, s_ref, tmax_ref = refs
    else:
        qt_ref, k_ref, vt_ref, bias_ref, o_ref, qe_ref, m_ref, acc_ref, s_ref, tmax_ref = refs
    n_pairs, nt, _, T = qt_ref.shape
    n_sc = s_ref.shape[0]
    per_pair = n_sc // n_pairs
    rows = LANES // per_pair
    q_tiles = (pl.program_id(2), nt - 1 - pl.program_id(2))

    def first_tile(i):
        return jnp.maximum(i - n_back, 0) if mode == "dil" else 0

    def setup(st, i):
        zeros_q = jnp.zeros((LANES, T), BF16)
        for sc in range(n_sc):
            p, r = divmod(sc, per_pair)
            qe_ref[st, sc, 0:LANES, :] = zeros_q
            qe_ref[st, sc, rows * r:rows * (r + 1), :] = qt_ref[p, i, rows * r:rows * (r + 1), :]
        if mode == "fox":
            for sc in range(n_sc):
                p, h = divmod(sc, 2)
                terms = _split3(cq_ref[i, p, h:h + 1, :])
                blocks = [jnp.broadcast_to(t.astype(F32), (BF16_ROWS, T)) for t in terms]
                blocks += [jnp.full((BF16_ROWS, T), 1.0 if h == hh else 0.0, F32) for hh in range(2)]
                blocks += [jnp.zeros((LANES - 5 * BF16_ROWS, T), F32)]
                qe_ref[st, sc, LANES:2 * LANES, :] = jnp.concatenate(blocks, axis=0).astype(BF16)
        m_ref[st] = jnp.full(m_ref.shape[1:], NEG_BIG, F32)
        acc_ref[st] = jnp.zeros(acc_ref.shape[1:], F32)

    def produce(st, sc, j):
        s = _dot(k_ref[sc // per_pair, j], qe_ref[st, sc])
        if mode == "dil":
            s = s + bias_ref[q_tiles[st] - j]
        s_ref[sc] = s
        tmax_ref[sc] = jnp.broadcast_to(jnp.max(s, axis=0, keepdims=True), (SUBLANES, T))

    def consume(st, sc, j, diag):
        s = s_ref[sc]
        if diag and mode != "dil":
            kk = lax.broadcasted_iota(jnp.int32, s.shape, 0)
            qq = lax.broadcasted_iota(jnp.int32, s.shape, 1)
            s = jnp.where(kk <= qq, s, NEG_BIG)
            tmax = jnp.max(s, axis=0, keepdims=True)
        else:
            tmax = tmax_ref[sc, 0:1, :]
        p, r = divmod(sc, per_pair)
        h = r * 2 // per_pair
        m_old = m_ref[st, sc, 0:1, :]
        m_new = jnp.maximum(m_old, tmax)
        alpha = jnp.exp2(m_old - m_new)
        pr = jnp.exp2(s - m_new).astype(BF16)
        pv = _dot(vt_ref[p, j, V_ROWS * h:V_ROWS * (h + 1), :], pr)
        acc_ref[st, sc] = acc_ref[st, sc] * alpha + pv
        m_ref[st, sc] = jnp.broadcast_to(m_new, (SUBLANES, T))

    ahead = min(PIPELINE_AHEAD, n_sc - 1)

    def run_tile(st, j, diag):
        for sc in range(n_sc):
            if sc + ahead < n_sc:
                produce(st, sc + ahead, j)
            elif not diag:
                produce(st, sc + ahead - n_sc, j + 1)
            elif st == 0:
                produce(1, sc + ahead - n_sc, first_tile(q_tiles[1]))
            consume(st, sc, j, diag)

    def finalize(st, i):
        def normalised(sc):
            a = acc_ref[st, sc]
            return a[0:HEAD_DIM] / a[HEAD_DIM:HEAD_DIM + 1]

        if mode == "diff":
            lam_init = lam_ref[4:5, 0:1]
            lam = (jnp.exp(jnp.sum(lam_ref[0:1] * lam_ref[1:2], axis=1, keepdims=True))
                   - jnp.exp(jnp.sum(lam_ref[2:3] * lam_ref[3:4], axis=1, keepdims=True)) + lam_init)
        for p in range(n_pairs):
            outs = []
            for h in range(2):
                if mode == "diff":
                    sc = per_pair * p + 2 * h
                    d = normalised(sc) - lam * normalised(sc + 1)
                    ms = jnp.mean(d * d, axis=0, keepdims=True)
                    outs.append(d * lax.rsqrt(ms + SUBLN_EPS) * gcol_ref[...] * (1.0 - lam_init))
                else:
                    outs.append(normalised(per_pair * p + h))
            o_ref[pl.ds(pl.multiple_of(i * T, T), T), LANES * p:LANES * (p + 1)] = (
                jnp.concatenate(outs, axis=0).T.astype(BF16))

    for st in range(2):
        setup(st, q_tiles[st])
    for sc in range(ahead):
        produce(0, sc, first_tile(q_tiles[0]))
    for st in range(2):
        i = q_tiles[st]

        def step(j, carry, st=st):
            run_tile(st, j, False)
            return carry

        lax.fori_loop(first_tile(i), i, step, 0)
        run_tile(st, i, True)
    for st in range(2):
        finalize(st, q_tiles[st])


def _attn_call(mode, qt, k, vt, extra, T, n_back=0, pairs_per_step=None):
    b, npair, nt = qt.shape[:3]
    assert nt % 2 == 0
    pps = npair if pairs_per_step is None else pairs_per_step
    r = k.shape[-1]
    n_sc = (4 if mode == "diff" else 2) * pps
    in_specs = [
        pl.BlockSpec((None, pps, nt, LANES, T), lambda bi, p, i: (bi, p, 0, 0, 0)),
        pl.BlockSpec((None, pps, nt, T, r), lambda bi, p, i: (bi, p, 0, 0, 0)),
        pl.BlockSpec((None, pps, nt, 2 * V_ROWS, T), lambda bi, p, i: (bi, p, 0, 0, 0)),
    ]
    if mode == "fox":
        in_specs.append(pl.BlockSpec((None, nt, pps, GATE_STRIDE, T), lambda bi, p, i: (bi, 0, p, 0, 0)))
    elif mode == "diff":
        in_specs += [_const_spec((8, LANES)), _const_spec((HEAD_DIM, 1))]
    else:
        in_specs.append(_const_spec(extra[0].shape))
    body = functools.partial(_attn_body, mode=mode, n_back=n_back)
    return pl.pallas_call(
        body,
        grid=(b, npair // pps, nt // 2),
        in_specs=in_specs,
        out_specs=pl.BlockSpec((None, nt * T, pps * LANES), lambda bi, p, i: (bi, 0, p)),
        out_shape=jax.ShapeDtypeStruct((b, nt * T, npair * LANES), BF16),
        scratch_shapes=[pltpu.VMEM((2, n_sc, r, T), BF16),
                        pltpu.VMEM((2, n_sc, SUBLANES, T), F32),
                        pltpu.VMEM((2, n_sc, V_ROWS, T), F32),
                        pltpu.VMEM((n_sc, T, T), F32),
                        pltpu.VMEM((n_sc, SUBLANES, T), F32)],
        compiler_params=pltpu.CompilerParams(dimension_semantics=("arbitrary", "arbitrary", "arbitrary"),
                                             vmem_limit_bytes=VMEM_LIMIT_BYTES),
        name="attn_" + mode,
    )(qt, k, vt, *extra)


def _constants(s, T):
    pos = np.arange(s, dtype=np.float64)[:, None]

    def angles(half):
        inv = ROPE_THETA ** (-np.arange(half, dtype=np.float64) / half)
        return pos * inv[None, :]

    ad, ac = angles(DIFF_QK_DIM // 2), angles(HEAD_DIM // 2)

    def nat(a):
        reps = LANES // (2 * a.shape[1])
        cos = np.tile(np.concatenate([np.cos(a), np.cos(a)], axis=1), (1, reps))
        sin = np.tile(np.concatenate([-np.sin(a), np.sin(a)], axis=1), (1, reps))
        return cos.astype(np.float32), sin.astype(np.float32)

    cnd, snd = nat(ad)
    cnc, snc = nat(ac)
    consts = dict(cnd=cnd, snd=snd, cnc=cnc, snc=snc,
                  ctd=np.cos(ad).T.astype(np.float32), std=np.sin(ad).T.astype(np.float32),
                  ctc=np.cos(ac).T.astype(np.float32), stc=np.sin(ac).T.astype(np.float32))
    consts["tri"] = np.triu(np.ones((T, T), np.float32))
    n_back = -(-max(w for w, _ in DILATED_PATTERNS) // T)
    kk = np.arange(T)[:, None]
    qq = np.arange(T)[None, :]
    bias = np.zeros((n_back + 1, T, T), np.float32)
    for o in range(n_back + 1):
        delta = o * T + qq - kk
        mult = np.zeros((T, T), np.float64)
        for w, dil in DILATED_PATTERNS:
            mult += (delta >= 0) & (delta <= w) & (delta % dil == 0)
        bias[o] = np.where(mult > 0, np.log2(np.maximum(mult, 1.0)), NEG_BIG)
    out = {}
    for name, v in consts.items():
        out[name] = jnp.asarray(v, BF16 if name == "tri" else F32)
    return out, jnp.asarray(bias), n_back


def _proj_weights(w_in, b_f):
    depth, d, _ = w_in.shape
    w = w_in.astype(BF16)
    sizes = [384] * 6 + [N_FOX_HEADS, 256, 256, 256]
    aq, ak, av, fq, fk, fv, fg, cq, ck, cv = jnp.split(w, np.cumsum(sizes)[:-1].tolist(), axis=2)
    rows = np.array([GATE_STRIDE * (h // 2) + h % 2 for h in range(N_FOX_HEADS)])
    fg_cols = jnp.zeros((depth, d, GATE_ROWS), BF16).at[:, :, rows].set(fg)
    bf_pad = jnp.zeros((depth, GATE_ROWS, 1), F32).at[:, rows, 0].set(b_f)
    wn = jnp.concatenate([ak, fk, ck], axis=2)
    wt = jnp.concatenate([aq, fq, cq, av, fv, cv, fg_cols], axis=2).transpose(0, 2, 1)
    return wn, wt, bf_pad


def _forward(x, w_in, b_f, lam_q1, lam_k1, lam_q2, lam_k2, g_sub, w_o, g_ffn1, w1_gate, w1_up, w1_down,
             g_mix, g_ffn2, w2_gate, w2_up, w2_down, g_final, T=512, tm=512):
    b, s, d = x.shape
    depth = w_in.shape[0]
    assert s % T == 0 and (b * s) % tm == 0 and d == HEAD_DIM * (N_DIFF_HEADS + N_FOX_HEADS + N_DIL_HEADS)
    consts, bias, n_back = _constants(s, T)
    wn_all, wt_all, bf_all = _proj_weights(w_in, b_f)
    ffn1 = (g_ffn1[:, None, :], w1_gate.astype(BF16), w1_up.astype(BF16), w1_down.astype(BF16))
    ffn2 = (g_ffn2[:, None, :], w2_gate.astype(BF16), w2_up.astype(BF16), w2_down.astype(BF16))
    g_mix3 = g_mix[:, None, :]
    wo_all = w_o.astype(BF16)
    x2d = x.reshape(b * s, d)
    for l in range(depth):
        lam_init = 0.8 - 0.6 * math.exp(-0.3 * l)
        if l == 0:
            x2d = _ffn_call(x2d, l, *ffn1, tm=tm)
        (qtd, kd, vtd, qtf, kf, vtf, cqf, qtc, kc, vtc) = _proj_call(
            x2d.reshape(b, s, d), l, g_mix3, wn_all, wt_all, bf_all, consts, T)
        lam_rows = jnp.zeros((8, LANES), F32)
        lam_rows = lam_rows.at[0:4, 0:DIFF_QK_DIM].set(jnp.stack([lam_q1[l], lam_k1[l], lam_q2[l], lam_k2[l]]))
        lam_rows = lam_rows.at[4, :].set(lam_init)
        o_a = _attn_call("diff", qtd, kd, vtd, (lam_rows, g_sub[l].reshape(HEAD_DIM, 1)), T)
        o_b = _attn_call("fox", qtf, kf, vtf, (cqf.reshape(b, s // T, 3, GATE_STRIDE, T),), T)
        o_c = _attn_call("dil", qtc, kc, vtc, (bias,), T, n_back=n_back)
        attn = ([o_a.reshape(b * s, -1), o_b.reshape(b * s, -1), o_c.reshape(b * s, -1)], wo_all)
        last = l == depth - 1
        x2d = _ffn_call(x2d, l, *ffn2, attn=attn, g_final=g_final if last else None, tm=tm)
        if not last:
            x2d = _ffn_call(x2d, l + 1, *ffn1, tm=tm)
    return x2d.reshape(b, s, d)


def kernel(x, w_in, b_f, lam_q1, lam_k1, lam_q2, lam_k2, g_sub, w_o, g_ffn1, w1_gate, w1_up, w1_down,
           g_mix, g_ffn2, w2_gate, w2_up, w2_down, g_final):
    return _forward(x, w_in, b_f, lam_q1, lam_k1, lam_q2, lam_k2, g_sub, w_o, g_ffn1, w1_gate, w1_up,
                    w1_down, g_mix, g_ffn2, w2_gate, w2_up, w2_down, g_final)
```

```python
import functools
import math

import numpy as np
import jax
import jax.numpy as jnp
from jax import lax
from jax.experimental import pallas as pl
from jax.experimental.pallas import tpu as pltpu

F32 = jnp.float32
BF16 = jnp.bfloat16

HEAD_DIM = 64
N_DIFF_HEADS = 6
N_FOX_HEADS = 6
N_DIL_HEADS = 4
DIFF_QK_DIM = 32
DILATED_PATTERNS = ((128, 1), (512, 4), (2048, 16))
ROPE_THETA = 10000.0
NORM_EPS = 1e-6
SUBLN_EPS = 1e-5
MACARON_SCALE = 0.5
NEG_BIG = -1e30
LOG2E = 1.4426950408889634

LANES = 128
SUBLANES = 8
MXU_WIDTH = 256
BF16_ROWS = 16
V_ROWS = HEAD_DIM + BF16_ROWS
GATE_STRIDE = 8
VMEM_LIMIT_BYTES = 56 * 1024 * 1024
PIPELINE_AHEAD = 2


def _dot(a, b):
    return jnp.dot(a, b, preferred_element_type=F32)


def _dot_nt(a, b):
    return lax.dot_general(a, b, (((1,), (1,)), ((), ())), preferred_element_type=F32)


def _rms(x, g, eps):
    ms = jnp.mean(x * x, axis=-1, keepdims=True)
    return x * lax.rsqrt(ms + eps) * g


def _split3(x):
    hi = x.astype(BF16)
    r1 = x - hi.astype(F32)
    mid = r1.astype(BF16)
    lo = (r1 - mid.astype(F32)).astype(BF16)
    return hi, mid, lo


def _ffn_body(*refs, has_attn, has_final, n_chunks):
    refs = list(refs)
    x_ref = refs.pop(0)
    if has_attn:
        oa_ref, ob_ref, oc_ref, wo_ref = refs[:4]
        refs = refs[4:]
    g_ref, wg_ref, wu_ref, wd_ref = refs[:4]
    refs = refs[4:]
    if has_final:
        gf_ref = refs.pop(0)
    out_ref = refs[0]

    x = x_ref[...]
    if has_attn:
        o = jnp.concatenate([oa_ref[...], ob_ref[...], oc_ref[...]], axis=1)
        x = x + _dot(o, wo_ref[...])
    xn = _rms(x, g_ref[...], NORM_EPS).astype(BF16)
    n_wide = wg_ref.shape[1] // MXU_WIDTH
    bounds = [MXU_WIDTH * ((n_wide * c + n_chunks - 1) // n_chunks) for c in range(n_chunks + 1)]
    acc = None
    for c in range(n_chunks):
        sl = slice(bounds[c], bounds[c + 1])
        gate = _dot(xn, wg_ref[:, sl])
        up = _dot(xn, wu_ref[:, sl])
        h = (gate * jax.nn.sigmoid(gate) * up).astype(BF16)
        d = _dot(h, wd_ref[sl, :])
        acc = d if acc is None else acc + d
    y = x + MACARON_SCALE * acc
    if has_final:
        y = _rms(y, gf_ref[...], NORM_EPS)
    out_ref[...] = y


def _const_spec(shape):
    nd = len(shape)
    return pl.BlockSpec(shape, lambda *_: (0,) * nd, pipeline_mode=pl.Buffered(1))


def _layer_spec(shape, l):
    nd = len(shape)
    return pl.BlockSpec((None,) + tuple(shape[1:]), lambda *_: (l,) + (0,) * (nd - 1),
                        pipeline_mode=pl.Buffered(1))


def _ffn_call(x2d, l, g, wg, wu, wd, attn=None, g_final=None, tm=512):
    n, d = x2d.shape
    dff = wg.shape[2]
    args = [x2d]
    specs = [pl.BlockSpec((tm, d), lambda i: (i, 0))]
    if attn is not None:
        o_list, wo = attn
        for o in o_list:
            args.append(o)
            specs.append(pl.BlockSpec((tm, o.shape[1]), lambda i: (i, 0)))
        args.append(wo)
        specs.append(_layer_spec(wo.shape, l))
    args += [g, wg, wu, wd]
    specs += [_layer_spec(g.shape, l), _layer_spec(wg.shape, l), _layer_spec(wu.shape, l),
              _layer_spec(wd.shape, l)]
    if g_final is not None:
        args.append(g_final.reshape(1, d))
        specs.append(_const_spec((1, d)))
    assert dff % MXU_WIDTH == 0
    n_chunks = 2
    body = functools.partial(_ffn_body, has_attn=attn is not None, has_final=g_final is not None,
                             n_chunks=n_chunks)
    return pl.pallas_call(
        body,
        grid=(n // tm,),
        in_specs=specs,
        out_specs=pl.BlockSpec((tm, d), lambda i: (i, 0)),
        out_shape=jax.ShapeDtypeStruct((n, d), F32),
        compiler_params=pltpu.CompilerParams(dimension_semantics=("arbitrary",),
                                             vmem_limit_bytes=VMEM_LIMIT_BYTES),
        name="ffn",
    )(*args)


N_NAT = 384 + 384 + 256
N_TR = 2 * (384 + 384 + 256)
GATE_ROWS = 32


def _proj_body(x_ref, g_ref, wn_ref, wt_ref, bf_ref, tri_ref,
               cnd_ref, snd_ref, cnc_ref, snc_ref, ctd_ref, std_ref, ctc_ref, stc_ref,
               qtd_ref, kd_ref, vtd_ref, qtf_ref, kf_ref, vtf_ref, cq_ref, qtc_ref, kc_ref, vtc_ref,
               carry_ref):
    tm = x_ref.shape[0]

    @pl.when(pl.program_id(1) == 0)
    def _():
        carry_ref[...] = jnp.zeros_like(carry_ref)

    xn = _rms(x_ref[...], g_ref[...], NORM_EPS).astype(BF16)
    rn = _dot(xn, wn_ref[...])
    rt = _dot_nt(wt_ref[...], xn)

    lane = lax.broadcasted_iota(jnp.int32, (tm, LANES), 1)

    def rope_nat(blk, half, cos, sin_signed):
        first = (lane & half) == 0
        rot = jnp.where(first, pltpu.roll(blk, LANES - half, 1), pltpu.roll(blk, half, 1))
        return (blk * cos + rot * sin_signed).astype(BF16)

    for p in range(3):
        kd_ref[p] = rope_nat(rn[:, LANES * p:LANES * (p + 1)], DIFF_QK_DIM // 2, cnd_ref[...], snd_ref[...])
    for p in range(2):
        c0 = 768 + LANES * p
        kc_ref[p] = rope_nat(rn[:, c0:c0 + LANES], HEAD_DIM // 2, cnc_ref[...], snc_ref[...])

    def rope_tr(ref, p, r0, x1, x2, cos, sin, scale):
        half = x1.shape[0]
        ref[p, r0:r0 + half, :] = ((x1 * cos - x2 * sin) * scale).astype(BF16)
        ref[p, r0 + half:r0 + 2 * half, :] = ((x2 * cos + x1 * sin) * scale).astype(BF16)

    sd = DIFF_QK_DIM ** -0.5 * LOG2E
    hd = DIFF_QK_DIM // 2
    for g in range(2 * N_DIFF_HEADS):
        b0 = DIFF_QK_DIM * g
        rope_tr(qtd_ref, g // 4, DIFF_QK_DIM * (g % 4), rt[b0:b0 + hd], rt[b0 + hd:b0 + 2 * hd],
                ctd_ref[...], std_ref[...], sd)
    sf = HEAD_DIM ** -0.5 * LOG2E
    for p in range(3):
        b0 = 384 + LANES * p
        qtf_ref[p] = (rt[b0:b0 + LANES] * sf).astype(BF16)
    hc = HEAD_DIM // 2
    for h in range(N_DIL_HEADS):
        b0 = 768 + HEAD_DIM * h
        rope_tr(qtc_ref, h // 2, HEAD_DIM * (h % 2), rt[b0:b0 + hc], rt[b0 + hc:b0 + 2 * hc],
                ctc_ref[...], stc_ref[...], sf)

    ones_rows = jnp.ones((BF16_ROWS, tm), BF16)

    def put_vt(ref, base, n_pairs):
        for p in range(n_pairs):
            for h in range(2):
                b0 = base + LANES * p + HEAD_DIM * h
                ref[p, V_ROWS * h:V_ROWS * h + HEAD_DIM, :] = rt[b0:b0 + HEAD_DIM].astype(BF16)
                ref[p, V_ROWS * h + HEAD_DIM:V_ROWS * (h + 1), :] = ones_rows

    put_vt(vtd_ref, 1024, 3)
    put_vt(vtf_ref, 1408, 3)
    put_vt(vtc_ref, 1792, 2)

    z = rt[N_TR:N_TR + GATE_ROWS] + bf_ref[...]
    logf = (jnp.minimum(z, 0.0) - jnp.log1p(jnp.exp(-jnp.abs(z)))) * LOG2E
    part = _dot(jnp.concatenate(_split3(logf), axis=0), tri_ref[...])
    c = part[0:GATE_ROWS] + part[GATE_ROWS:2 * GATE_ROWS] + part[2 * GATE_ROWS:3 * GATE_ROWS] + carry_ref[...]
    carry_ref[...] = c[:, tm - 1:tm]
    cq_ref[...] = c[0:3 * GATE_STRIDE]
    row = lax.broadcasted_iota(jnp.int32, (LANES, tm), 0)
    ones_pattern = jnp.where((row < 3 * BF16_ROWS) & ((row & (BF16_ROWS - 1)) == 0), 1.0, 0.0)
    for p in range(3):
        kf_ref[p, :, 0:LANES] = rn[:, 384 + LANES * p:384 + LANES * (p + 1)].astype(BF16)
        auxt = ones_pattern
        for h in range(2):
            ch = c[GATE_STRIDE * p + h:GATE_STRIDE * p + h + 1]
            for t, term in enumerate(_split3(ch)):
                auxt = jnp.where(row == (3 + h) * BF16_ROWS + t, -term.astype(F32), auxt)
        kf_ref[p, :, LANES:2 * LANES] = auxt.T.astype(BF16)


def _proj_call(x, l, g_mix, wn, wt, bf_pad, consts, T):
    b, s, d = x.shape
    nt = s // T
    tile5 = lambda npair, r, c: pl.BlockSpec((None, npair, None, r, c), lambda bi, ti: (bi, 0, ti, 0, 0))
    nat_tab = pl.BlockSpec((T, LANES), lambda bi, ti: (ti, 0))
    tr_tab = lambda r: pl.BlockSpec((r, T), lambda bi, ti: (0, ti))
    in_specs = [
        pl.BlockSpec((None, T, d), lambda bi, ti: (bi, ti, 0)),
        _layer_spec(g_mix.shape, l), _layer_spec(wn.shape, l), _layer_spec(wt.shape, l),
        _layer_spec(bf_pad.shape, l), _const_spec((T, T)),
        nat_tab, nat_tab, nat_tab, nat_tab,
        tr_tab(DIFF_QK_DIM // 2), tr_tab(DIFF_QK_DIM // 2), tr_tab(HEAD_DIM // 2), tr_tab(HEAD_DIM // 2),
    ]
    out_shape = [
        jax.ShapeDtypeStruct((b, 3, nt, LANES, T), BF16),
        jax.ShapeDtypeStruct((b, 3, nt, T, LANES), BF16),
        jax.ShapeDtypeStruct((b, 3, nt, 2 * V_ROWS, T), BF16),
        jax.ShapeDtypeStruct((b, 3, nt, LANES, T), BF16),
        jax.ShapeDtypeStruct((b, 3, nt, T, 2 * LANES), BF16),
        jax.ShapeDtypeStruct((b, 3, nt, 2 * V_ROWS, T), BF16),
        jax.ShapeDtypeStruct((b, nt, 3 * GATE_STRIDE, T), F32),
        jax.ShapeDtypeStruct((b, 2, nt, LANES, T), BF16),
        jax.ShapeDtypeStruct((b, 2, nt, T, LANES), BF16),
        jax.ShapeDtypeStruct((b, 2, nt, 2 * V_ROWS, T), BF16),
    ]
    out_specs = [
        tile5(3, LANES, T), tile5(3, T, LANES), tile5(3, 2 * V_ROWS, T),
        tile5(3, LANES, T), tile5(3, T, 2 * LANES), tile5(3, 2 * V_ROWS, T),
        pl.BlockSpec((None, None, 3 * GATE_STRIDE, T), lambda bi, ti: (bi, ti, 0, 0)),
        tile5(2, LANES, T), tile5(2, T, LANES), tile5(2, 2 * V_ROWS, T),
    ]
    return pl.pallas_call(
        _proj_body,
        grid=(b, nt),
        in_specs=in_specs,
        out_specs=out_specs,
        out_shape=out_shape,
        scratch_shapes=[pltpu.VMEM((GATE_ROWS, 1), F32)],
        compiler_params=pltpu.CompilerParams(dimension_semantics=("arbitrary", "arbitrary"),
                                             vmem_limit_bytes=VMEM_LIMIT_BYTES),
        name="proj",
    )(x, g_mix, wn, wt, bf_pad, consts["tri"], consts["cnd"], consts["snd"], consts["cnc"], consts["snc"],
      consts["ctd"], consts["std"], consts["ctc"], consts["stc"])


def _attn_body(*refs, mode, n_back):
    if mode == "fox":
        qt_ref, k_ref, vt_ref, cq_ref, o_ref, qe_ref, m_ref, acc_ref, s_ref, tmax_ref = refs
    elif mode == "diff":
        qt_ref, k_ref, vt_ref, lam_ref, gcol_ref, o_ref, qe_ref, m_ref, acc_ref, s_ref, tmax_ref = refs
    else:
        qt_ref, k_ref, vt_ref, bias_ref, o_ref, qe_ref, m_ref, acc_ref, s_ref, tmax_ref = refs
    n_pairs, nt, _, T = qt_ref.shape
    n_sc = s_ref.shape[0]
    per_pair = n_sc // n_pairs
    rows = LANES // per_pair
    q_tiles = (pl.program_id(2), nt - 1 - pl.program_id(2))

    def first_tile(i):
        return jnp.maximum(i - n_back, 0) if mode == "dil" else 0

    def setup(st, i):
        zeros_q = jnp.zeros((LANES, T), BF16)
        for sc in range(n_sc):
            p, r = divmod(sc, per_pair)
            qe_ref[st, sc, 0:LANES, :] = zeros_q
            qe_ref[st, sc, rows * r:rows * (r + 1), :] = qt_ref[p, i, rows * r:rows * (r + 1), :]
        if mode == "fox":
            for sc in range(n_sc):
                p, h = divmod(sc, 2)
                terms = _split3(cq_ref[i, p, h:h + 1, :])
                blocks = [jnp.broadcast_to(t.astype(F32), (BF16_ROWS, T)) for t in terms]
                blocks += [jnp.full((BF16_ROWS, T), 1.0 if h == hh else 0.0, F32) for hh in range(2)]
                blocks += [jnp.zeros((LANES - 5 * BF16_ROWS, T), F32)]
                qe_ref[st, sc, LANES:2 * LANES, :] = jnp.concatenate(blocks, axis=0).astype(BF16)
        m_ref[st] = jnp.full(m_ref.shape[1:], NEG_BIG, F32)
        acc_ref[st] = jnp.zeros(acc_ref.shape[1:], F32)

    half = T // 2
    full_blocks = ((slice(0, T), slice(0, T)),)
    diag_blocks = ((slice(0, half), slice(0, half)), (slice(0, T), slice(half, T)))

    def produce(st, sc, j, diag=False):
        for rk, cq in (diag_blocks if diag else full_blocks):
            s = _dot(k_ref[sc // per_pair, j, rk, :], qe_ref[st, sc, :, cq])
            if mode == "dil":
                s = s + bias_ref[q_tiles[st] - j, rk, cq]
            s_ref[sc, rk, cq] = s
            tmax_ref[sc, :, cq] = jnp.broadcast_to(jnp.max(s, axis=0, keepdims=True), (SUBLANES, s.shape[1]))

    def consume(st, sc, j, diag):
        p, r = divmod(sc, per_pair)
        h = r * 2 // per_pair
        for rk, cq in (diag_blocks if diag else full_blocks):
            s = s_ref[sc, rk, cq]
            if diag and mode != "dil":
                kk = lax.broadcasted_iota(jnp.int32, s.shape, 0)
                qq = lax.broadcasted_iota(jnp.int32, s.shape, 1) + cq.start
                s = jnp.where(kk <= qq, s, NEG_BIG)
                tmax = jnp.max(s, axis=0, keepdims=True)
            else:
                tmax = tmax_ref[sc, 0:1, cq]
            m_old = m_ref[st, sc, 0:1, cq]
            m_new = jnp.maximum(m_old, tmax)
            alpha = jnp.exp2(m_old - m_new)
            pr = jnp.exp2(s - m_new).astype(BF16)
            pv = _dot(vt_ref[p, j, V_ROWS * h:V_ROWS * (h + 1), rk], pr)
            acc_ref[st, sc, :, cq] = acc_ref[st, sc, :, cq] * alpha + pv
            m_ref[st, sc, :, cq] = jnp.broadcast_to(m_new, (SUBLANES, s.shape[1]))

    ahead = min(PIPELINE_AHEAD, n_sc - 1)

    def run_tile(st, j, diag):
        for sc in range(n_sc):
            if sc + ahead < n_sc:
                produce(st, sc + ahead, j, diag)
            elif not diag:
                produce(st, sc + ahead - n_sc, j + 1)
            elif st == 0:
                produce(1, sc + ahead - n_sc, first_tile(q_tiles[1]))
            consume(st, sc, j, diag)

    def finalize(st, i):
        def normalised(sc):
            a = acc_ref[st, sc]
            return a[0:HEAD_DIM] / a[HEAD_DIM:HEAD_DIM + 1]

        if mode == "diff":
            lam_init = lam_ref[4:5, 0:1]
            lam = (jnp.exp(jnp.sum(lam_ref[0:1] * lam_ref[1:2], axis=1, keepdims=True))
                   - jnp.exp(jnp.sum(lam_ref[2:3] * lam_ref[3:4], axis=1, keepdims=True)) + lam_init)
        for p in range(n_pairs):
            outs = []
            for h in range(2):
                if mode == "diff":
                    sc = per_pair * p + 2 * h
                    d = normalised(sc) - lam * normalised(sc + 1)
                    ms = jnp.mean(d * d, axis=0, keepdims=True)
                    outs.append(d * lax.rsqrt(ms + SUBLN_EPS) * gcol_ref[...] * (1.0 - lam_init))
                else:
                    outs.append(normalised(per_pair * p + h))
            o_ref[pl.ds(pl.multiple_of(i * T, T), T), LANES * p:LANES * (p + 1)] = (
                jnp.concatenate(outs, axis=0).T.astype(BF16))

    for st in range(2):
        setup(st, q_tiles[st])
    for sc in range(ahead):
        produce(0, sc, first_tile(q_tiles[0]))
    for st in range(2):
        i = q_tiles[st]

        def step(j, carry, st=st):
            run_tile(st, j, False)
            return carry

        lax.fori_loop(first_tile(i), i, step, 0)
        run_tile(st, i, True)
    for st in range(2):
        finalize(st, q_tiles[st])


def _attn_call(mode, qt, k, vt, extra, T, n_back=0, pairs_per_step=None):
    b, npair, nt = qt.shape[:3]
    assert nt % 2 == 0
    pps = npair if pairs_per_step is None else pairs_per_step
    r = k.shape[-1]
    n_sc = (4 if mode == "diff" else 2) * pps
    in_specs = [
        pl.BlockSpec((None, pps, nt, LANES, T), lambda bi, p, i: (bi, p, 0, 0, 0)),
        pl.BlockSpec((None, pps, nt, T, r), lambda bi, p, i: (bi, p, 0, 0, 0)),
        pl.BlockSpec((None, pps, nt, 2 * V_ROWS, T), lambda bi, p, i: (bi, p, 0, 0, 0)),
    ]
    if mode == "fox":
        in_specs.append(pl.BlockSpec((None, nt, pps, GATE_STRIDE, T), lambda bi, p, i: (bi, 0, p, 0, 0)))
    elif mode == "diff":
        in_specs += [_const_spec((8, LANES)), _const_spec((HEAD_DIM, 1))]
    else:
        in_specs.append(_const_spec(extra[0].shape))
    body = functools.partial(_attn_body, mode=mode, n_back=n_back)
    return pl.pallas_call(
        body,
        grid=(b, npair // pps, nt // 2),
        in_specs=in_specs,
        out_specs=pl.BlockSpec((None, nt * T, pps * LANES), lambda bi, p, i: (bi, 0, p)),
        out_shape=jax.ShapeDtypeStruct((b, nt * T, npair * LANES), BF16),
        scratch_shapes=[pltpu.VMEM((2, n_sc, r, T), BF16),
                        pltpu.VMEM((2, n_sc, SUBLANES, T), F32),
                        pltpu.VMEM((2, n_sc, V_ROWS, T), F32),
                        pltpu.VMEM((n_sc, T, T), F32),
                        pltpu.VMEM((n_sc, SUBLANES, T), F32)],
        compiler_params=pltpu.CompilerParams(dimension_semantics=("arbitrary", "arbitrary", "arbitrary"),
                                             vmem_limit_bytes=VMEM_LIMIT_BYTES),
        name="attn_" + mode,
    )(qt, k, vt, *extra)


def _constants(s, T):
    pos = np.arange(s, dtype=np.float64)[:, None]

    def angles(half):
        inv = ROPE_THETA ** (-np.arange(half, dtype=np.float64) / half)
        return pos * inv[None, :]

    ad, ac = angles(DIFF_QK_DIM // 2), angles(HEAD_DIM // 2)

    def nat(a):
        reps = LANES // (2 * a.shape[1])
        cos = np.tile(np.concatenate([np.cos(a), np.cos(a)], axis=1), (1, reps))
        sin = np.tile(np.concatenate([-np.sin(a), np.sin(a)], axis=1), (1, reps))
        return cos.astype(np.float32), sin.astype(np.float32)

    cnd, snd = nat(ad)
    cnc, snc = nat(ac)
    consts = dict(cnd=cnd, snd=snd, cnc=cnc, snc=snc,
                  ctd=np.cos(ad).T.astype(np.float32), std=np.sin(ad).T.astype(np.float32),
                  ctc=np.cos(ac).T.astype(np.float32), stc=np.sin(ac).T.astype(np.float32))
    consts["tri"] = np.triu(np.ones((T, T), np.float32))
    n_back = -(-max(w for w, _ in DILATED_PATTERNS) // T)
    kk = np.arange(T)[:, None]
    qq = np.arange(T)[None, :]
    bias = np.zeros((n_back + 1, T, T), np.float32)
    for o in range(n_back + 1):
        delta = o * T + qq - kk
        mult = np.zeros((T, T), np.float64)
        for w, dil in DILATED_PATTERNS:
            mult += (delta >= 0) & (delta <= w) & (delta % dil == 0)
        bias[o] = np.where(mult > 0, np.log2(np.maximum(mult, 1.0)), NEG_BIG)
    out = {}
    for name, v in consts.items():
        out[name] = jnp.asarray(v, BF16 if name == "tri" else F32)
    return out, jnp.asarray(bias), n_back


def _proj_weights(w_in, b_f):
    depth, d, _ = w_in.shape
    w = w_in.astype(BF16)
    sizes = [384] * 6 + [N_FOX_HEADS, 256, 256, 256]
    aq, ak, av, fq, fk, fv, fg, cq, ck, cv = jnp.split(w, np.cumsum(sizes)[:-1].tolist(), axis=2)
    rows = np.array([GATE_STRIDE * (h // 2) + h % 2 for h in range(N_FOX_HEADS)])
    fg_cols = jnp.zeros((depth, d, GATE_ROWS), BF16).at[:, :, rows].set(fg)
    bf_pad = jnp.zeros((depth, GATE_ROWS, 1), F32).at[:, rows, 0].set(b_f)
    wn = jnp.concatenate([ak, fk, ck], axis=2)
    wt = jnp.concatenate([aq, fq, cq, av, fv, cv, fg_cols], axis=2).transpose(0, 2, 1)
    return wn, wt, bf_pad


def _forward(x, w_in, b_f, lam_q1, lam_k1, lam_q2, lam_k2, g_sub, w_o, g_ffn1, w1_gate, w1_up, w1_down,
             g_mix, g_ffn2, w2_gate, w2_up, w2_down, g_final, T=512, tm=512):
    b, s, d = x.shape
    depth = w_in.shape[0]
    assert s % T == 0 and (b * s) % tm == 0 and d == HEAD_DIM * (N_DIFF_HEADS + N_FOX_HEADS + N_DIL_HEADS)
    consts, bias, n_back = _constants(s, T)
    wn_all, wt_all, bf_all = _proj_weights(w_in, b_f)
    ffn1 = (g_ffn1[:, None, :], w1_gate.astype(BF16), w1_up.astype(BF16), w1_down.astype(BF16))
    ffn2 = (g_ffn2[:, None, :], w2_gate.astype(BF16), w2_up.astype(BF16), w2_down.astype(BF16))
    g_mix3 = g_mix[:, None, :]
    wo_all = w_o.astype(BF16)
    x2d = x.reshape(b * s, d)
    for l in range(depth):
        lam_init = 0.8 - 0.6 * math.exp(-0.3 * l)
        if l == 0:
            x2d = _ffn_call(x2d, l, *ffn1, tm=tm)
        (qtd, kd, vtd, qtf, kf, vtf, cqf, qtc, kc, vtc) = _proj_call(
            x2d.reshape(b, s, d), l, g_mix3, wn_all, wt_all, bf_all, consts, T)
        lam_rows = jnp.zeros((8, LANES), F32)
        lam_rows = lam_rows.at[0:4, 0:DIFF_QK_DIM].set(jnp.stack([lam_q1[l], lam_k1[l], lam_q2[l], lam_k2[l]]))
        lam_rows = lam_rows.at[4, :].set(lam_init)
        o_a = _attn_call("diff", qtd, kd, vtd, (lam_rows, g_sub[l].reshape(HEAD_DIM, 1)), T)
        o_b = _attn_call("fox", qtf, kf, vtf, (cqf.reshape(b, s // T, 3, GATE_STRIDE, T),), T)
        o_c = _attn_call("dil", qtc, kc, vtc, (bias,), T, n_back=n_back)
        attn = ([o_a.reshape(b * s, -1), o_b.reshape(b * s, -1), o_c.reshape(b * s, -1)], wo_all)
        last = l == depth - 1
        x2d = _ffn_call(x2d, l, *ffn2, attn=attn, g_final=g_final if last else None, tm=tm)
        if not last:
            x2d = _ffn_call(x2d, l + 1, *ffn1, tm=tm)
    return x2d.reshape(b, s, d)


def kernel(x, w_in, b_f, lam_q1, lam_k1, lam_q2, lam_k2, g_sub, w_o, g_ffn1, w1_gate, w1_up, w1_down,
           g_mix, g_ffn2, w2_gate, w2_up, w2_down, g_final):
    return _forward(x, w_in, b_f, lam_q1, lam_k1, lam_q2, lam_k2, g_sub, w_o, g_ffn1, w1_gate, w1_up,
                    w1_down, g_mix, g_ffn2, w2_gate, w2_up, w2_down, g_final)
```

```python
import functools
import math

import numpy as np
import jax
import jax.numpy as jnp
from jax import lax
from jax.experimental import pallas as pl
from jax.experimental.pallas import tpu as pltpu

F32 = jnp.float32
BF16 = jnp.bfloat16

HEAD_DIM = 64
N_DIFF_HEADS = 6
N_FOX_HEADS = 6
N_DIL_HEADS = 4
DIFF_QK_DIM = 32
DILATED_PATTERNS = ((128, 1), (512, 4), (2048, 16))
ROPE_THETA = 10000.0
NORM_EPS = 1e-6
SUBLN_EPS = 1e-5
MACARON_SCALE = 0.5
NEG_BIG = -1e30
LOG2E = 1.4426950408889634

LANES = 128
SUBLANES = 8
MXU_WIDTH = 256
BF16_ROWS = 16
V_ROWS = HEAD_DIM + BF16_ROWS
GATE_STRIDE = 8
VMEM_LIMIT_BYTES = 56 * 1024 * 1024
PIPELINE_AHEAD = 2


def _dot(a, b):
    return jnp.dot(a, b, preferred_element_type=F32)


def _dot_nt(a, b):
    return lax.dot_general(a, b, (((1,), (1,)), ((), ())), preferred_element_type=F32)


def _rms(x, g, eps):
    ms = jnp.mean(x * x, axis=-1, keepdims=True)
    return x * lax.rsqrt(ms + eps) * g


def _split3(x):
    hi = x.astype(BF16)
    r1 = x - hi.astype(F32)
    mid = r1.astype(BF16)
    lo = (r1 - mid.astype(F32)).astype(BF16)
    return hi, mid, lo


def _ffn_body(*refs, has_attn, has_final, n_chunks):
    refs = list(refs)
    x_ref = refs.pop(0)
    if has_attn:
        oa_ref, ob_ref, oc_ref, wo_ref = refs[:4]
        refs = refs[4:]
    g_ref, wg_ref, wu_ref, wd_ref = refs[:4]
    refs = refs[4:]
    if has_final:
        gf_ref = refs.pop(0)
    out_ref = refs[0]

    x = x_ref[...]
    if has_attn:
        o = jnp.concatenate([oa_ref[...], ob_ref[...], oc_ref[...]], axis=1)
        x = x + _dot(o, wo_ref[...])
    xn = _rms(x, g_ref[...], NORM_EPS).astype(BF16)
    n_wide = wg_ref.shape[1] // MXU_WIDTH
    bounds = [MXU_WIDTH * ((n_wide * c + n_chunks - 1) // n_chunks) for c in range(n_chunks + 1)]
    acc = None
    for c in range(n_chunks):
        sl = slice(bounds[c], bounds[c + 1])
        gate = _dot(xn, wg_ref[:, sl])
        up = _dot(xn, wu_ref[:, sl])
        h = (gate * jax.nn.sigmoid(gate) * up).astype(BF16)
        d = _dot(h, wd_ref[sl, :])
        acc = d if acc is None else acc + d
    y = x + MACARON_SCALE * acc
    if has_final:
        y = _rms(y, gf_ref[...], NORM_EPS)
    out_ref[...] = y


def _const_spec(shape):
    nd = len(shape)
    return pl.BlockSpec(shape, lambda *_: (0,) * nd, pipeline_mode=pl.Buffered(1))


def _layer_spec(shape, l):
    nd = len(shape)
    return pl.BlockSpec((None,) + tuple(shape[1:]), lambda *_: (l,) + (0,) * (nd - 1),
                        pipeline_mode=pl.Buffered(1))


def _ffn_call(x2d, l, g, wg, wu, wd, attn=None, g_final=None, tm=512):
    n, d = x2d.shape
    dff = wg.shape[2]
    args = [x2d]
    specs = [pl.BlockSpec((tm, d), lambda i: (i, 0))]
    if attn is not None:
        o_list, wo = attn
        for o in o_list:
            args.append(o)
            specs.append(pl.BlockSpec((tm, o.shape[1]), lambda i: (i, 0)))
        args.append(wo)
        specs.append(_layer_spec(wo.shape, l))
    args += [g, wg, wu, wd]
    specs += [_layer_spec(g.shape, l), _layer_spec(wg.shape, l), _layer_spec(wu.shape, l),
              _layer_spec(wd.shape, l)]
    if g_final is not None:
        args.append(g_final.reshape(1, d))
        specs.append(_const_spec((1, d)))
    assert dff % MXU_WIDTH == 0
    n_chunks = 2
    body = functools.partial(_ffn_body, has_attn=attn is not None, has_final=g_final is not None,
                             n_chunks=n_chunks)
    return pl.pallas_call(
        body,
        grid=(n // tm,),
        in_specs=specs,
        out_specs=pl.BlockSpec((tm, d), lambda i: (i, 0)),
        out_shape=jax.ShapeDtypeStruct((n, d), F32),
        compiler_params=pltpu.CompilerParams(dimension_semantics=("arbitrary",),
                                             vmem_limit_bytes=VMEM_LIMIT_BYTES),
        name="ffn",
    )(*args)


N_NAT = 384 + 384 + 256
N_TR = 2 * (384 + 384 + 256)
GATE_ROWS = 32


def _proj_body(x_ref, g_ref, wn_ref, wt_ref, bf_ref, tri_ref,
               cnd_ref, snd_ref, cnc_ref, snc_ref, ctd_ref, std_ref, ctc_ref, stc_ref,
               qtd_ref, kd_ref, vtd_ref, qtf_ref, kf_ref, vtf_ref, cq_ref, qtc_ref, kc_ref, vtc_ref,
               carry_ref):
    tm = x_ref.shape[0]

    @pl.when(pl.program_id(1) == 0)
    def _():
        carry_ref[...] = jnp.zeros_like(carry_ref)

    xn = _rms(x_ref[...], g_ref[...], NORM_EPS).astype(BF16)
    rn = _dot(xn, wn_ref[...])
    rt = _dot_nt(wt_ref[...], xn)

    lane = lax.broadcasted_iota(jnp.int32, (tm, LANES), 1)

    def rope_nat(blk, half, cos, sin_signed):
        first = (lane & half) == 0
        rot = jnp.where(first, pltpu.roll(blk, LANES - half, 1), pltpu.roll(blk, half, 1))
        return (blk * cos + rot * sin_signed).astype(BF16)

    for p in range(3):
        kd_ref[p] = rope_nat(rn[:, LANES * p:LANES * (p + 1)], DIFF_QK_DIM // 2, cnd_ref[...], snd_ref[...])
    for p in range(2):
        c0 = 768 + LANES * p
        kc_ref[p] = rope_nat(rn[:, c0:c0 + LANES], HEAD_DIM // 2, cnc_ref[...], snc_ref[...])

    def rope_tr(ref, p, r0, x1, x2, cos, sin, scale):
        half = x1.shape[0]
        ref[p, r0:r0 + half, :] = ((x1 * cos - x2 * sin) * scale).astype(BF16)
        ref[p, r0 + half:r0 + 2 * half, :] = ((x2 * cos + x1 * sin) * scale).astype(BF16)

    sd = DIFF_QK_DIM ** -0.5 * LOG2E
    hd = DIFF_QK_DIM // 2
    for g in range(2 * N_DIFF_HEADS):
        b0 = DIFF_QK_DIM * g
        rope_tr(qtd_ref, g // 4, DIFF_QK_DIM * (g % 4), rt[b0:b0 + hd], rt[b0 + hd:b0 + 2 * hd],
                ctd_ref[...], std_ref[...], sd)
    sf = HEAD_DIM ** -0.5 * LOG2E
    for p in range(3):
        b0 = 384 + LANES * p
        qtf_ref[p] = (rt[b0:b0 + LANES] * sf).astype(BF16)
    hc = HEAD_DIM // 2
    for h in range(N_DIL_HEADS):
        b0 = 768 + HEAD_DIM * h
        rope_tr(qtc_ref, h // 2, HEAD_DIM * (h % 2), rt[b0:b0 + hc], rt[b0 + hc:b0 + 2 * hc],
                ctc_ref[...], stc_ref[...], sf)

    ones_rows = jnp.ones((BF16_ROWS, tm), BF16)

    def put_vt(ref, base, n_pairs):
        for p in range(n_pairs):
            for h in range(2):
                b0 = base + LANES * p + HEAD_DIM * h
                ref[p, V_ROWS * h:V_ROWS * h + HEAD_DIM, :] = rt[b0:b0 + HEAD_DIM].astype(BF16)
                ref[p, V_ROWS * h + HEAD_DIM:V_ROWS * (h + 1), :] = ones_rows

    put_vt(vtd_ref, 1024, 3)
    put_vt(vtf_ref, 1408, 3)
    put_vt(vtc_ref, 1792, 2)

    z = rt[N_TR:N_TR + GATE_ROWS] + bf_ref[...]
    logf = (jnp.minimum(z, 0.0) - jnp.log1p(jnp.exp(-jnp.abs(z)))) * LOG2E
    part = _dot(jnp.concatenate(_split3(logf), axis=0), tri_ref[...])
    c = part[0:GATE_ROWS] + part[GATE_ROWS:2 * GATE_ROWS] + part[2 * GATE_ROWS:3 * GATE_ROWS] + carry_ref[...]
    carry_ref[...] = c[:, tm - 1:tm]
    cq_ref[...] = c[0:3 * GATE_STRIDE]
    row = lax.broadcasted_iota(jnp.int32, (LANES, tm), 0)
    ones_pattern = jnp.where((row < 3 * BF16_ROWS) & ((row & (BF16_ROWS - 1)) == 0), 1.0, 0.0)
    for p in range(3):
        kf_ref[p, :, 0:LANES] = rn[:, 384 + LANES * p:384 + LANES * (p + 1)].astype(BF16)
        auxt = ones_pattern
        for h in range(2):
            ch = c[GATE_STRIDE * p + h:GATE_STRIDE * p + h + 1]
            for t, term in enumerate(_split3(ch)):
                auxt = jnp.where(row == (3 + h) * BF16_ROWS + t, -term.astype(F32), auxt)
        kf_ref[p, :, LANES:2 * LANES] = auxt.T.astype(BF16)


def _proj_call(x, l, g_mix, wn, wt, bf_pad, consts, T):
    b, s, d = x.shape
    nt = s // T
    tile5 = lambda npair, r, c: pl.BlockSpec((None, npair, None, r, c), lambda bi, ti: (bi, 0, ti, 0, 0))
    nat_tab = pl.BlockSpec((T, LANES), lambda bi, ti: (ti, 0))
    tr_tab = lambda r: pl.BlockSpec((r, T), lambda bi, ti: (0, ti))
    in_specs = [
        pl.BlockSpec((None, T, d), lambda bi, ti: (bi, ti, 0)),
        _layer_spec(g_mix.shape, l), _layer_spec(wn.shape, l), _layer_spec(wt.shape, l),
        _layer_spec(bf_pad.shape, l), _const_spec((T, T)),
        nat_tab, nat_tab, nat_tab, nat_tab,
        tr_tab(DIFF_QK_DIM // 2), tr_tab(DIFF_QK_DIM // 2), tr_tab(HEAD_DIM // 2), tr_tab(HEAD_DIM // 2),
    ]
    out_shape = [
        jax.ShapeDtypeStruct((b, 3, nt, LANES, T), BF16),
        jax.ShapeDtypeStruct((b, 3, nt, T, LANES), BF16),
        jax.ShapeDtypeStruct((b, 3, nt, 2 * V_ROWS, T), BF16),
        jax.ShapeDtypeStruct((b, 3, nt, LANES, T), BF16),
        jax.ShapeDtypeStruct((b, 3, nt, T, 2 * LANES), BF16),
        jax.ShapeDtypeStruct((b, 3, nt, 2 * V_ROWS, T), BF16),
        jax.ShapeDtypeStruct((b, nt, 3 * GATE_STRIDE, T), F32),
        jax.ShapeDtypeStruct((b, 2, nt, LANES, T), BF16),
        jax.ShapeDtypeStruct((b, 2, nt, T, LANES), BF16),
        jax.ShapeDtypeStruct((b, 2, nt, 2 * V_ROWS, T), BF16),
    ]
    out_specs = [
        tile5(3, LANES, T), tile5(3, T, LANES), tile5(3, 2 * V_ROWS, T),
        tile5(3, LANES, T), tile5(3, T, 2 * LANES), tile5(3, 2 * V_ROWS, T),
        pl.BlockSpec((None, None, 3 * GATE_STRIDE, T), lambda bi, ti: (bi, ti, 0, 0)),
        tile5(2, LANES, T), tile5(2, T, LANES), tile5(2, 2 * V_ROWS, T),
    ]
    return pl.pallas_call(
        _proj_body,
        grid=(b, nt),
        in_specs=in_specs,
        out_specs=out_specs,
        out_shape=out_shape,
        scratch_shapes=[pltpu.VMEM((GATE_ROWS, 1), F32)],
        compiler_params=pltpu.CompilerParams(dimension_semantics=("arbitrary", "arbitrary"),
                                             vmem_limit_bytes=VMEM_LIMIT_BYTES),
        name="proj",
    )(x, g_mix, wn, wt, bf_pad, consts["tri"], consts["cnd"], consts["snd"], consts["cnc"], consts["snc"],
      consts["ctd"], consts["std"], consts["ctc"], consts["stc"])


def _attn_body(*refs, mode, n_back):
    if mode == "fox":
        qt_ref, k_ref, vt_ref, cq_ref, o_ref, qe_ref, m_ref, acc_ref, s_ref, tmax_ref = refs
    elif mode == "diff":
        qt_ref, k_ref, vt_ref, lam_ref, gcol_ref, o_ref, qe_ref, m_ref, acc_ref, s_ref, tmax_ref = refs
    else:
        qt_ref, k_ref, vt_ref, bias_ref, o_ref, qe_ref, m_ref, acc_ref, s_ref, tmax_ref = refs
    n_pairs, nt, _, T = qt_ref.shape
    n_sc = s_ref.shape[0]
    per_pair = n_sc // n_pairs
    rows = LANES // per_pair
    q_tiles = (pl.program_id(2), nt - 1 - pl.program_id(2))

    def first_tile(i):
        return jnp.maximum(i - n_back, 0) if mode == "dil" else 0

    def setup(st, i):
        for sc in range(n_sc):
            p, r = divmod(sc, per_pair)
            qe_ref[st, sc, rows * r:rows * (r + 1), :] = qt_ref[p, i, rows * r:rows * (r + 1), :]
        if mode == "fox":
            for sc in range(n_sc):
                p, h = divmod(sc, 2)
                terms = _split3(cq_ref[i, p, h:h + 1, :])
                blocks = [jnp.broadcast_to(t.astype(F32), (BF16_ROWS, T)) for t in terms]
                blocks += [jnp.full((BF16_ROWS, T), 1.0 if h == hh else 0.0, F32) for hh in range(2)]
                blocks += [jnp.zeros((LANES - 5 * BF16_ROWS, T), F32)]
                qe_ref[st, sc, LANES:2 * LANES, :] = jnp.concatenate(blocks, axis=0).astype(BF16)
        m_ref[st] = jnp.full(m_ref.shape[1:], NEG_BIG, F32)
        acc_ref[st] = jnp.zeros(acc_ref.shape[1:], F32)

    half = T // 2
    full_blocks = ((slice(0, T), slice(0, T)),)
    diag_blocks = ((slice(0, half), slice(0, half)), (slice(0, T), slice(half, T)))

    def produce(st, sc, j, diag=False):
        for rk, cq in (diag_blocks if diag else full_blocks):
            s = _dot(k_ref[sc // per_pair, j, rk, :], qe_ref[st, sc, :, cq])
            if mode == "dil":
                s = s + bias_ref[q_tiles[st] - j, rk, cq]
            s_ref[sc, rk, cq] = s
            if mode == "dil" or not diag:
                tmax_ref[sc, :, cq] = jnp.broadcast_to(jnp.max(s, axis=0, keepdims=True),
                                                       (SUBLANES, s.shape[1]))

    def consume(st, sc, j, diag):
        p, r = divmod(sc, per_pair)
        h = r * 2 // per_pair
        for rk, cq in (diag_blocks if diag else full_blocks):
            s = s_ref[sc, rk, cq]
            if diag and mode != "dil":
                kk = lax.broadcasted_iota(jnp.int32, s.shape, 0)
                qq = lax.broadcasted_iota(jnp.int32, s.shape, 1) + cq.start
                s = jnp.where(kk <= qq, s, NEG_BIG)
                tmax = jnp.max(s, axis=0, keepdims=True)
            else:
                tmax = tmax_ref[sc, 0:1, cq]
            m_old = m_ref[st, sc, 0:1, cq]
            m_new = jnp.maximum(m_old, tmax)
            alpha = jnp.exp2(m_old - m_new)
            pr = jnp.exp2(s - m_new).astype(BF16)
            pv = _dot(vt_ref[p, j, V_ROWS * h:V_ROWS * (h + 1), rk], pr)
            acc_ref[st, sc, :, cq] = acc_ref[st, sc, :, cq] * alpha + pv
            m_ref[st, sc, :, cq] = jnp.broadcast_to(m_new, (SUBLANES, s.shape[1]))

    ahead = min(PIPELINE_AHEAD, n_sc - 1)

    def run_tile(st, j, diag):
        for sc in range(n_sc):
            if sc + ahead < n_sc:
                produce(st, sc + ahead, j, diag)
            elif not diag:
                produce(st, sc + ahead - n_sc, j + 1)
            elif st == 0:
                produce(1, sc + ahead - n_sc, first_tile(q_tiles[1]))
            consume(st, sc, j, diag)

    def finalize(st, i):
        def normalised(sc):
            a = acc_ref[st, sc]
            return a[0:HEAD_DIM] / a[HEAD_DIM:HEAD_DIM + 1]

        if mode == "diff":
            lam_init = lam_ref[4:5, 0:1]
            lam = (jnp.exp(jnp.sum(lam_ref[0:1] * lam_ref[1:2], axis=1, keepdims=True))
                   - jnp.exp(jnp.sum(lam_ref[2:3] * lam_ref[3:4], axis=1, keepdims=True)) + lam_init)
        for p in range(n_pairs):
            outs = []
            for h in range(2):
                if mode == "diff":
                    sc = per_pair * p + 2 * h
                    d = normalised(sc) - lam * normalised(sc + 1)
                    ms = jnp.mean(d * d, axis=0, keepdims=True)
                    outs.append(d * lax.rsqrt(ms + SUBLN_EPS) * gcol_ref[...] * (1.0 - lam_init))
                else:
                    outs.append(normalised(per_pair * p + h))
            o_ref[pl.ds(pl.multiple_of(i * T, T), T), LANES * p:LANES * (p + 1)] = (
                jnp.concatenate(outs, axis=0).T.astype(BF16))

    @pl.when((pl.program_id(0) == 0) & (pl.program_id(1) == 0) & (pl.program_id(2) == 0))
    def _():
        qe_ref[...] = jnp.zeros(qe_ref.shape, BF16)

    for st in range(2):
        setup(st, q_tiles[st])
    for sc in range(ahead):
        produce(0, sc, first_tile(q_tiles[0]))
    for st in range(2):
        i = q_tiles[st]

        def step(j, carry, st=st):
            run_tile(st, j, False)
            return carry

        lax.fori_loop(first_tile(i), i, step, 0)
        run_tile(st, i, True)
    for st in range(2):
        finalize(st, q_tiles[st])


def _attn_call(mode, qt, k, vt, extra, T, n_back=0, pairs_per_step=None, layer=0):
    b, npair, nt = qt.shape[:3]
    assert nt % 2 == 0
    pps = npair if pairs_per_step is None else pairs_per_step
    r = k.shape[-1]
    n_sc = (4 if mode == "diff" else 2) * pps
    in_specs = [
        pl.BlockSpec((None, pps, nt, LANES, T), lambda bi, p, i: (bi, p, 0, 0, 0)),
        pl.BlockSpec((None, pps, nt, T, r), lambda bi, p, i: (bi, p, 0, 0, 0)),
        pl.BlockSpec((None, pps, nt, 2 * V_ROWS, T), lambda bi, p, i: (bi, p, 0, 0, 0)),
    ]
    if mode == "fox":
        in_specs.append(pl.BlockSpec((None, nt, pps, GATE_STRIDE, T), lambda bi, p, i: (bi, 0, p, 0, 0)))
    elif mode == "diff":
        in_specs += [_layer_spec(extra[0].shape, layer), _layer_spec(extra[1].shape, layer)]
    else:
        in_specs.append(_const_spec(extra[0].shape))
    body = functools.partial(_attn_body, mode=mode, n_back=n_back)
    return pl.pallas_call(
        body,
        grid=(b, npair // pps, nt // 2),
        in_specs=in_specs,
        out_specs=pl.BlockSpec((None, nt * T, pps * LANES), lambda bi, p, i: (bi, 0, p)),
        out_shape=jax.ShapeDtypeStruct((b, nt * T, npair * LANES), BF16),
        scratch_shapes=[pltpu.VMEM((2, n_sc, r, T), BF16),
                        pltpu.VMEM((2, n_sc, SUBLANES, T), F32),
                        pltpu.VMEM((2, n_sc, V_ROWS, T), F32),
                        pltpu.VMEM((n_sc, T, T), F32),
                        pltpu.VMEM((n_sc, SUBLANES, T), F32)],
        compiler_params=pltpu.CompilerParams(dimension_semantics=("arbitrary", "arbitrary", "arbitrary"),
                                             vmem_limit_bytes=VMEM_LIMIT_BYTES),
        name="attn_" + mode,
    )(qt, k, vt, *extra)


def _constants(s, T):
    pos = np.arange(s, dtype=np.float64)[:, None]

    def angles(half):
        inv = ROPE_THETA ** (-np.arange(half, dtype=np.float64) / half)
        return pos * inv[None, :]

    ad, ac = angles(DIFF_QK_DIM // 2), angles(HEAD_DIM // 2)

    def nat(a):
        reps = LANES // (2 * a.shape[1])
        cos = np.tile(np.concatenate([np.cos(a), np.cos(a)], axis=1), (1, reps))
        sin = np.tile(np.concatenate([-np.sin(a), np.sin(a)], axis=1), (1, reps))
        return cos.astype(np.float32), sin.astype(np.float32)

    cnd, snd = nat(ad)
    cnc, snc = nat(ac)
    consts = dict(cnd=cnd, snd=snd, cnc=cnc, snc=snc,
                  ctd=np.cos(ad).T.astype(np.float32), std=np.sin(ad).T.astype(np.float32),
                  ctc=np.cos(ac).T.astype(np.float32), stc=np.sin(ac).T.astype(np.float32))
    consts["tri"] = np.triu(np.ones((T, T), np.float32))
    n_back = -(-max(w for w, _ in DILATED_PATTERNS) // T)
    kk = np.arange(T)[:, None]
    qq = np.arange(T)[None, :]
    bias = np.zeros((n_back + 1, T, T), np.float32)
    for o in range(n_back + 1):
        delta = o * T + qq - kk
        mult = np.zeros((T, T), np.float64)
        for w, dil in DILATED_PATTERNS:
            mult += (delta >= 0) & (delta <= w) & (delta % dil == 0)
        bias[o] = np.where(mult > 0, np.log2(np.maximum(mult, 1.0)), NEG_BIG)
    out = {}
    for name, v in consts.items():
        out[name] = jnp.asarray(v, BF16 if name == "tri" else F32)
    return out, jnp.asarray(bias), n_back


def _proj_weights(w_in, b_f):
    depth, d, _ = w_in.shape
    w = w_in.astype(BF16)
    sizes = [384] * 6 + [N_FOX_HEADS, 256, 256, 256]
    aq, ak, av, fq, fk, fv, fg, cq, ck, cv = jnp.split(w, np.cumsum(sizes)[:-1].tolist(), axis=2)
    rows = np.array([GATE_STRIDE * (h // 2) + h % 2 for h in range(N_FOX_HEADS)])
    fg_cols = jnp.zeros((depth, d, GATE_ROWS), BF16).at[:, :, rows].set(fg)
    bf_pad = jnp.zeros((depth, GATE_ROWS, 1), F32).at[:, rows, 0].set(b_f)
    wn = jnp.concatenate([ak, fk, ck], axis=2)
    wt = jnp.concatenate([aq, fq, cq, av, fv, cv, fg_cols], axis=2).transpose(0, 2, 1)
    return wn, wt, bf_pad


def _forward(x, w_in, b_f, lam_q1, lam_k1, lam_q2, lam_k2, g_sub, w_o, g_ffn1, w1_gate, w1_up, w1_down,
             g_mix, g_ffn2, w2_gate, w2_up, w2_down, g_final, T=512, tm=512):
    b, s, d = x.shape
    depth = w_in.shape[0]
    assert s % T == 0 and (b * s) % tm == 0 and d == HEAD_DIM * (N_DIFF_HEADS + N_FOX_HEADS + N_DIL_HEADS)
    consts, bias, n_back = _constants(s, T)
    wn_all, wt_all, bf_all = _proj_weights(w_in, b_f)
    ffn1 = (g_ffn1[:, None, :], w1_gate.astype(BF16), w1_up.astype(BF16), w1_down.astype(BF16))
    ffn2 = (g_ffn2[:, None, :], w2_gate.astype(BF16), w2_up.astype(BF16), w2_down.astype(BF16))
    g_mix3 = g_mix[:, None, :]
    wo_all = w_o.astype(BF16)
    lam_init = np.array([0.8 - 0.6 * math.exp(-0.3 * l) for l in range(depth)], np.float32)
    lam_all = jnp.zeros((depth, SUBLANES, LANES), F32)
    lam_all = lam_all.at[:, 0:4, 0:DIFF_QK_DIM].set(jnp.stack([lam_q1, lam_k1, lam_q2, lam_k2], axis=1))
    lam_all = lam_all.at[:, 4, :].set(jnp.asarray(lam_init)[:, None])
    x2d = x.reshape(b * s, d)
    for l in range(depth):
        if l == 0:
            x2d = _ffn_call(x2d, l, *ffn1, tm=tm)
        (qtd, kd, vtd, qtf, kf, vtf, cqf, qtc, kc, vtc) = _proj_call(
            x2d.reshape(b, s, d), l, g_mix3, wn_all, wt_all, bf_all, consts, T)
        o_a = _attn_call("diff", qtd, kd, vtd, (lam_all, g_sub[:, :, None]), T, layer=l)
        o_b = _attn_call("fox", qtf, kf, vtf, (cqf.reshape(b, s // T, 3, GATE_STRIDE, T),), T)
        o_c = _attn_call("dil", qtc, kc, vtc, (bias,), T, n_back=n_back)
        attn = ([o_a.reshape(b * s, -1), o_b.reshape(b * s, -1), o_c.reshape(b * s, -1)], wo_all)
        last = l == depth - 1
        x2d = _ffn_call(x2d, l, *ffn2, attn=attn, g_final=g_final if last else None, tm=tm)
        if not last:
            x2d = _ffn_call(x2d, l + 1, *ffn1, tm=tm)
    return x2d.reshape(b, s, d)


def kernel(x, w_in, b_f, lam_q1, lam_k1, lam_q2, lam_k2, g_sub, w_o, g_ffn1, w1_gate, w1_up, w1_down,
           g_mix, g_ffn2, w2_gate, w2_up, w2_down, g_final):
    return _forward(x, w_in, b_f, lam_q1, lam_k1, lam_q2, lam_k2, g_sub, w_o, g_ffn1, w1_gate, w1_up,
                    w1_down, g_mix, g_ffn2, w2_gate, w2_up, w2_down, g_final)
```

```python
import functools
import math

import numpy as np
import jax
import jax.numpy as jnp
from jax import lax
from jax.experimental import pallas as pl
from jax.experimental.pallas import tpu as pltpu

F32 = jnp.float32
BF16 = jnp.bfloat16

HEAD_DIM = 64
N_DIFF_HEADS = 6
N_FOX_HEADS = 6
N_DIL_HEADS = 4
DIFF_QK_DIM = 32
DILATED_PATTERNS = ((128, 1), (512, 4), (2048, 16))
ROPE_THETA = 10000.0
NORM_EPS = 1e-6
SUBLN_EPS = 1e-5
MACARON_SCALE = 0.5
NEG_BIG = -1e30
LOG2E = 1.4426950408889634

LANES = 128
SUBLANES = 8
MXU_WIDTH = 256
BF16_ROWS = 16
V_ROWS = HEAD_DIM + BF16_ROWS
GATE_STRIDE = 8
VMEM_LIMIT_BYTES = 56 * 1024 * 1024
PIPELINE_AHEAD = 2


def _dot(a, b):
    return jnp.dot(a, b, preferred_element_type=F32)


def _dot_nt(a, b):
    return lax.dot_general(a, b, (((1,), (1,)), ((), ())), preferred_element_type=F32)


def _rms(x, g, eps):
    ms = jnp.mean(x * x, axis=-1, keepdims=True)
    return x * lax.rsqrt(ms + eps) * g


def _split3(x):
    hi = x.astype(BF16)
    r1 = x - hi.astype(F32)
    mid = r1.astype(BF16)
    lo = (r1 - mid.astype(F32)).astype(BF16)
    return hi, mid, lo


def _ffn_body(*refs, has_attn, has_final, n_chunks):
    refs = list(refs)
    x_ref = refs.pop(0)
    if has_attn:
        oa_ref, ob_ref, oc_ref, wo_ref = refs[:4]
        refs = refs[4:]
    g_ref, wg_ref, wu_ref, wd_ref = refs[:4]
    refs = refs[4:]
    if has_final:
        gf_ref = refs.pop(0)
    out_ref = refs[0]

    x = x_ref[...]
    if has_attn:
        o = jnp.concatenate([oa_ref[...], ob_ref[...], oc_ref[...]], axis=1)
        x = x + _dot(o, wo_ref[...])
    xn = _rms(x, g_ref[...], NORM_EPS).astype(BF16)
    n_wide = wg_ref.shape[1] // MXU_WIDTH
    bounds = [MXU_WIDTH * ((n_wide * c + n_chunks - 1) // n_chunks) for c in range(n_chunks + 1)]
    acc = None
    for c in range(n_chunks):
        sl = slice(bounds[c], bounds[c + 1])
        gate = _dot(xn, wg_ref[:, sl])
        up = _dot(xn, wu_ref[:, sl])
        h = (gate * jax.nn.sigmoid(gate) * up).astype(BF16)
        d = _dot(h, wd_ref[sl, :])
        acc = d if acc is None else acc + d
    y = x + MACARON_SCALE * acc
    if has_final:
        y = _rms(y, gf_ref[...], NORM_EPS)
    out_ref[...] = y


def _const_spec(shape):
    nd = len(shape)
    return pl.BlockSpec(shape, lambda *_: (0,) * nd, pipeline_mode=pl.Buffered(1))


def _layer_spec(shape, l):
    nd = len(shape)
    return pl.BlockSpec((None,) + tuple(shape[1:]), lambda *_: (l,) + (0,) * (nd - 1),
                        pipeline_mode=pl.Buffered(1))


def _ffn_call(x2d, l, g, wg, wu, wd, attn=None, g_final=None, tm=512):
    n, d = x2d.shape
    dff = wg.shape[2]
    args = [x2d]
    specs = [pl.BlockSpec((tm, d), lambda i: (i, 0))]
    if attn is not None:
        o_list, wo = attn
        for o in o_list:
            args.append(o)
            specs.append(pl.BlockSpec((tm, o.shape[1]), lambda i: (i, 0)))
        args.append(wo)
        specs.append(_layer_spec(wo.shape, l))
    args += [g, wg, wu, wd]
    specs += [_layer_spec(g.shape, l), _layer_spec(wg.shape, l), _layer_spec(wu.shape, l),
              _layer_spec(wd.shape, l)]
    if g_final is not None:
        args.append(g_final.reshape(1, d))
        specs.append(_const_spec((1, d)))
    assert dff % MXU_WIDTH == 0
    n_chunks = 2
    body = functools.partial(_ffn_body, has_attn=attn is not None, has_final=g_final is not None,
                             n_chunks=n_chunks)
    return pl.pallas_call(
        body,
        grid=(n // tm,),
        in_specs=specs,
        out_specs=pl.BlockSpec((tm, d), lambda i: (i, 0)),
        out_shape=jax.ShapeDtypeStruct((n, d), F32),
        compiler_params=pltpu.CompilerParams(dimension_semantics=("arbitrary",),
                                             vmem_limit_bytes=VMEM_LIMIT_BYTES),
        name="ffn",
    )(*args)


N_NAT = 384 + 384 + 256
N_TR = 2 * (384 + 384 + 256)
GATE_ROWS = 32


def _proj_body(x_ref, g_ref, wn_ref, wt_ref, bf_ref, tri_ref,
               cnd_ref, snd_ref, cnc_ref, snc_ref, ctd_ref, std_ref, ctc_ref, stc_ref,
               qtd_ref, kd_ref, vtd_ref, qtf_ref, kf_ref, vtf_ref, cq_ref, qtc_ref, kc_ref, vtc_ref,
               carry_ref):
    tm = x_ref.shape[0]

    @pl.when(pl.program_id(1) == 0)
    def _():
        carry_ref[...] = jnp.zeros_like(carry_ref)

    xn = _rms(x_ref[...], g_ref[...], NORM_EPS).astype(BF16)
    rn = _dot(xn, wn_ref[...])
    rt_all = _dot_nt(wt_ref[...], xn)
    gate_t = rt_all[0:GATE_ROWS]
    rt = rt_all[GATE_ROWS:]

    lane = lax.broadcasted_iota(jnp.int32, (tm, LANES), 1)

    def rope_nat(blk, half, cos, sin_signed):
        first = (lane & half) == 0
        rot = jnp.where(first, pltpu.roll(blk, LANES - half, 1), pltpu.roll(blk, half, 1))
        return (blk * cos + rot * sin_signed).astype(BF16)

    for p in range(3):
        kd_ref[p] = rope_nat(rn[:, LANES * p:LANES * (p + 1)], DIFF_QK_DIM // 2, cnd_ref[...], snd_ref[...])
    for p in range(2):
        c0 = 768 + LANES * p
        kc_ref[p] = rope_nat(rn[:, c0:c0 + LANES], HEAD_DIM // 2, cnc_ref[...], snc_ref[...])

    def rope_tr(ref, p, r0, x1, x2, cos, sin, scale):
        half = x1.shape[0]
        ref[p, r0:r0 + half, :] = ((x1 * cos - x2 * sin) * scale).astype(BF16)
        ref[p, r0 + half:r0 + 2 * half, :] = ((x2 * cos + x1 * sin) * scale).astype(BF16)

    sd = DIFF_QK_DIM ** -0.5 * LOG2E
    hd = DIFF_QK_DIM // 2
    for g in range(2 * N_DIFF_HEADS):
        b0 = DIFF_QK_DIM * g
        rope_tr(qtd_ref, g // 4, DIFF_QK_DIM * (g % 4), rt[b0:b0 + hd], rt[b0 + hd:b0 + 2 * hd],
                ctd_ref[...], std_ref[...], sd)
    sf = HEAD_DIM ** -0.5 * LOG2E
    for p in range(3):
        b0 = 384 + LANES * p
        qtf_ref[p] = (rt[b0:b0 + LANES] * sf).astype(BF16)
    hc = HEAD_DIM // 2
    for h in range(N_DIL_HEADS):
        b0 = 768 + HEAD_DIM * h
        rope_tr(qtc_ref, h // 2, HEAD_DIM * (h % 2), rt[b0:b0 + hc], rt[b0 + hc:b0 + 2 * hc],
                ctc_ref[...], stc_ref[...], sf)

    ones_rows = jnp.ones((BF16_ROWS, tm), BF16)

    def put_vt(ref, base, n_pairs):
        for p in range(n_pairs):
            for h in range(2):
                b0 = base + LANES * p + HEAD_DIM * h
                ref[p, V_ROWS * h:V_ROWS * h + HEAD_DIM, :] = rt[b0:b0 + HEAD_DIM].astype(BF16)
                ref[p, V_ROWS * h + HEAD_DIM:V_ROWS * (h + 1), :] = ones_rows

    put_vt(vtd_ref, 1024, 3)
    put_vt(vtf_ref, 1408, 3)
    put_vt(vtc_ref, 1792, 2)

    z = gate_t + bf_ref[...]
    logf = (jnp.minimum(z, 0.0) - jnp.log1p(jnp.exp(-jnp.abs(z)))) * LOG2E
    part = _dot(jnp.concatenate(_split3(logf), axis=0), tri_ref[...])
    c = part[0:GATE_ROWS] + part[GATE_ROWS:2 * GATE_ROWS] + part[2 * GATE_ROWS:3 * GATE_ROWS] + carry_ref[...]
    carry_ref[...] = c[:, tm - 1:tm]
    cq_ref[...] = c[0:3 * GATE_STRIDE]
    row = lax.broadcasted_iota(jnp.int32, (LANES, tm), 0)
    ones_pattern = jnp.where((row < 3 * BF16_ROWS) & ((row & (BF16_ROWS - 1)) == 0), 1.0, 0.0)
    for p in range(3):
        kf_ref[p, :, 0:LANES] = rn[:, 384 + LANES * p:384 + LANES * (p + 1)].astype(BF16)
        auxt = ones_pattern
        for h in range(2):
            ch = c[GATE_STRIDE * p + h:GATE_STRIDE * p + h + 1]
            for t, term in enumerate(_split3(ch)):
                auxt = jnp.where(row == (3 + h) * BF16_ROWS + t, -term.astype(F32), auxt)
        kf_ref[p, :, LANES:2 * LANES] = auxt.T.astype(BF16)


def _proj_call(x, l, g_mix, wn, wt, bf_pad, consts, T):
    b, s, d = x.shape
    nt = s // T
    tile5 = lambda npair, r, c: pl.BlockSpec((None, npair, None, r, c), lambda bi, ti: (bi, 0, ti, 0, 0))
    nat_tab = pl.BlockSpec((T, LANES), lambda bi, ti: (ti, 0))
    tr_tab = lambda r: pl.BlockSpec((r, T), lambda bi, ti: (0, ti))
    in_specs = [
        pl.BlockSpec((None, T, d), lambda bi, ti: (bi, ti, 0)),
        _layer_spec(g_mix.shape, l), _layer_spec(wn.shape, l), _layer_spec(wt.shape, l),
        _layer_spec(bf_pad.shape, l), _const_spec((T, T)),
        nat_tab, nat_tab, nat_tab, nat_tab,
        tr_tab(DIFF_QK_DIM // 2), tr_tab(DIFF_QK_DIM // 2), tr_tab(HEAD_DIM // 2), tr_tab(HEAD_DIM // 2),
    ]
    out_shape = [
        jax.ShapeDtypeStruct((b, 3, nt, LANES, T), BF16),
        jax.ShapeDtypeStruct((b, 3, nt, T, LANES), BF16),
        jax.ShapeDtypeStruct((b, 3, nt, 2 * V_ROWS, T), BF16),
        jax.ShapeDtypeStruct((b, 3, nt, LANES, T), BF16),
        jax.ShapeDtypeStruct((b, 3, nt, T, 2 * LANES), BF16),
        jax.ShapeDtypeStruct((b, 3, nt, 2 * V_ROWS, T), BF16),
        jax.ShapeDtypeStruct((b, nt, 3 * GATE_STRIDE, T), F32),
        jax.ShapeDtypeStruct((b, 2, nt, LANES, T), BF16),
        jax.ShapeDtypeStruct((b, 2, nt, T, LANES), BF16),
        jax.ShapeDtypeStruct((b, 2, nt, 2 * V_ROWS, T), BF16),
    ]
    out_specs = [
        tile5(3, LANES, T), tile5(3, T, LANES), tile5(3, 2 * V_ROWS, T),
        tile5(3, LANES, T), tile5(3, T, 2 * LANES), tile5(3, 2 * V_ROWS, T),
        pl.BlockSpec((None, None, 3 * GATE_STRIDE, T), lambda bi, ti: (bi, ti, 0, 0)),
        tile5(2, LANES, T), tile5(2, T, LANES), tile5(2, 2 * V_ROWS, T),
    ]
    return pl.pallas_call(
        _proj_body,
        grid=(b, nt),
        in_specs=in_specs,
        out_specs=out_specs,
        out_shape=out_shape,
        scratch_shapes=[pltpu.VMEM((GATE_ROWS, 1), F32)],
        compiler_params=pltpu.CompilerParams(dimension_semantics=("arbitrary", "arbitrary"),
                                             vmem_limit_bytes=VMEM_LIMIT_BYTES),
        name="proj",
    )(x, g_mix, wn, wt, bf_pad, consts["tri"], consts["cnd"], consts["snd"], consts["cnc"], consts["snc"],
      consts["ctd"], consts["std"], consts["ctc"], consts["stc"])


def _attn_body(*refs, mode, n_back):
    if mode == "fox":
        qt_ref, k_ref, vt_ref, cq_ref, o_ref, qe_ref, m_ref, acc_ref, s_ref, tmax_ref = refs
    elif mode == "diff":
        qt_ref, k_ref, vt_ref, lam_ref, gcol_ref, o_ref, qe_ref, m_ref, acc_ref, s_ref, tmax_ref = refs
    else:
        qt_ref, k_ref, vt_ref, bias_ref, o_ref, qe_ref, m_ref, acc_ref, s_ref, tmax_ref = refs
    n_pairs, nt, _, T = qt_ref.shape
    n_sc = s_ref.shape[0]
    per_pair = n_sc // n_pairs
    rows = LANES // per_pair
    q_tiles = (pl.program_id(2), nt - 1 - pl.program_id(2))

    def first_tile(i):
        return jnp.maximum(i - n_back, 0) if mode == "dil" else 0

    def setup(st, i):
        for sc in range(n_sc):
            p, r = divmod(sc, per_pair)
            qe_ref[st, sc, rows * r:rows * (r + 1), :] = qt_ref[p, i, rows * r:rows * (r + 1), :]
        if mode == "fox":
            for sc in range(n_sc):
                p, h = divmod(sc, 2)
                terms = _split3(cq_ref[i, p, h:h + 1, :])
                blocks = [jnp.broadcast_to(t.astype(F32), (BF16_ROWS, T)) for t in terms]
                blocks += [jnp.full((BF16_ROWS, T), 1.0 if h == hh else 0.0, F32) for hh in range(2)]
                blocks += [jnp.zeros((LANES - 5 * BF16_ROWS, T), F32)]
                qe_ref[st, sc, LANES:2 * LANES, :] = jnp.concatenate(blocks, axis=0).astype(BF16)
        m_ref[st] = jnp.full(m_ref.shape[1:], NEG_BIG, F32)
        acc_ref[st] = jnp.zeros(acc_ref.shape[1:], F32)

    half = T // 2
    full_blocks = ((slice(0, T), slice(0, T)),)
    diag_blocks = ((slice(0, half), slice(0, half)), (slice(0, T), slice(half, T)))

    def produce(st, sc, j, diag=False):
        for rk, cq in (diag_blocks if diag else full_blocks):
            s = _dot(k_ref[sc // per_pair, j, rk, :], qe_ref[st, sc, :, cq])
            if mode == "dil":
                s = s + bias_ref[q_tiles[st] - j, rk, cq]
            s_ref[sc, rk, cq] = s
            if mode == "dil" or not diag:
                tmax_ref[sc, :, cq] = jnp.broadcast_to(jnp.max(s, axis=0, keepdims=True),
                                                       (SUBLANES, s.shape[1]))

    def consume(st, sc, j, diag):
        p, r = divmod(sc, per_pair)
        h = r * 2 // per_pair
        for rk, cq in (diag_blocks if diag else full_blocks):
            s = s_ref[sc, rk, cq]
            if diag and mode != "dil":
                kk = lax.broadcasted_iota(jnp.int32, s.shape, 0)
                qq = lax.broadcasted_iota(jnp.int32, s.shape, 1) + cq.start
                s = jnp.where(kk <= qq, s, NEG_BIG)
                tmax = jnp.max(s, axis=0, keepdims=True)
            else:
                tmax = tmax_ref[sc, 0:1, cq]
            m_old = m_ref[st, sc, 0:1, cq]
            m_new = jnp.maximum(m_old, tmax)
            alpha = jnp.exp2(m_old - m_new)
            pr = jnp.exp2(s - m_new).astype(BF16)
            pv = _dot(vt_ref[p, j, V_ROWS * h:V_ROWS * (h + 1), rk], pr)
            acc_ref[st, sc, :, cq] = acc_ref[st, sc, :, cq] * alpha + pv
            m_ref[st, sc, :, cq] = jnp.broadcast_to(m_new, (SUBLANES, s.shape[1]))

    ahead = min(PIPELINE_AHEAD, n_sc - 1)

    def run_tile(st, j, diag):
        for sc in range(n_sc):
            if sc + ahead < n_sc:
                produce(st, sc + ahead, j, diag)
            elif not diag:
                produce(st, sc + ahead - n_sc, j + 1)
            elif st == 0:
                produce(1, sc + ahead - n_sc, first_tile(q_tiles[1]))
            consume(st, sc, j, diag)

    def finalize(st, i):
        def normalised(sc):
            a = acc_ref[st, sc]
            return a[0:HEAD_DIM] / a[HEAD_DIM:HEAD_DIM + 1]

        if mode == "diff":
            lam_init = lam_ref[4:5, 0:1]
            lam = (jnp.exp(jnp.sum(lam_ref[0:1] * lam_ref[1:2], axis=1, keepdims=True))
                   - jnp.exp(jnp.sum(lam_ref[2:3] * lam_ref[3:4], axis=1, keepdims=True)) + lam_init)
        for p in range(n_pairs):
            outs = []
            for h in range(2):
                if mode == "diff":
                    sc = per_pair * p + 2 * h
                    d = normalised(sc) - lam * normalised(sc + 1)
                    ms = jnp.mean(d * d, axis=0, keepdims=True)
                    outs.append(d * lax.rsqrt(ms + SUBLN_EPS) * gcol_ref[...] * (1.0 - lam_init))
                else:
                    outs.append(normalised(per_pair * p + h))
            o_ref[pl.ds(pl.multiple_of(i * T, T), T), LANES * p:LANES * (p + 1)] = (
                jnp.concatenate(outs, axis=0).T.astype(BF16))

    @pl.when((pl.program_id(0) == 0) & (pl.program_id(1) == 0) & (pl.program_id(2) == 0))
    def _():
        qe_ref[...] = jnp.zeros(qe_ref.shape, BF16)

    for st in range(2):
        setup(st, q_tiles[st])
    for sc in range(ahead):
        produce(0, sc, first_tile(q_tiles[0]))
    for st in range(2):
        i = q_tiles[st]

        def step(j, carry, st=st):
            run_tile(st, j, False)
            return carry

        lax.fori_loop(first_tile(i), i, step, 0)
        run_tile(st, i, True)
    for st in range(2):
        finalize(st, q_tiles[st])


def _attn_call(mode, qt, k, vt, extra, T, n_back=0, pairs_per_step=None, layer=0):
    b, npair, nt = qt.shape[:3]
    assert nt % 2 == 0
    pps = npair if pairs_per_step is None else pairs_per_step
    r = k.shape[-1]
    n_sc = (4 if mode == "diff" else 2) * pps
    in_specs = [
        pl.BlockSpec((None, pps, nt, LANES, T), lambda bi, p, i: (bi, p, 0, 0, 0)),
        pl.BlockSpec((None, pps, nt, T, r), lambda bi, p, i: (bi, p, 0, 0, 0)),
        pl.BlockSpec((None, pps, nt, 2 * V_ROWS, T), lambda bi, p, i: (bi, p, 0, 0, 0)),
    ]
    if mode == "fox":
        in_specs.append(pl.BlockSpec((None, nt, pps, GATE_STRIDE, T), lambda bi, p, i: (bi, 0, p, 0, 0)))
    elif mode == "diff":
        in_specs += [_layer_spec(extra[0].shape, layer), _layer_spec(extra[1].shape, layer)]
    else:
        in_specs.append(_const_spec(extra[0].shape))
    body = functools.partial(_attn_body, mode=mode, n_back=n_back)
    return pl.pallas_call(
        body,
        grid=(b, npair // pps, nt // 2),
        in_specs=in_specs,
        out_specs=pl.BlockSpec((None, nt * T, pps * LANES), lambda bi, p, i: (bi, 0, p)),
        out_shape=jax.ShapeDtypeStruct((b, nt * T, npair * LANES), BF16),
        scratch_shapes=[pltpu.VMEM((2, n_sc, r, T), BF16),
                        pltpu.VMEM((2, n_sc, SUBLANES, T), F32),
                        pltpu.VMEM((2, n_sc, V_ROWS, T), F32),
                        pltpu.VMEM((n_sc, T, T), F32),
                        pltpu.VMEM((n_sc, SUBLANES, T), F32)],
        compiler_params=pltpu.CompilerParams(dimension_semantics=("arbitrary", "arbitrary", "arbitrary"),
                                             vmem_limit_bytes=VMEM_LIMIT_BYTES),
        name="attn_" + mode,
    )(qt, k, vt, *extra)


def _constants(s, T):
    pos = np.arange(s, dtype=np.float64)[:, None]

    def angles(half):
        inv = ROPE_THETA ** (-np.arange(half, dtype=np.float64) / half)
        return pos * inv[None, :]

    ad, ac = angles(DIFF_QK_DIM // 2), angles(HEAD_DIM // 2)

    def nat(a):
        reps = LANES // (2 * a.shape[1])
        cos = np.tile(np.concatenate([np.cos(a), np.cos(a)], axis=1), (1, reps))
        sin = np.tile(np.concatenate([-np.sin(a), np.sin(a)], axis=1), (1, reps))
        return cos.astype(np.float32), sin.astype(np.float32)

    cnd, snd = nat(ad)
    cnc, snc = nat(ac)
    consts = dict(cnd=cnd, snd=snd, cnc=cnc, snc=snc,
                  ctd=np.cos(ad).T.astype(np.float32), std=np.sin(ad).T.astype(np.float32),
                  ctc=np.cos(ac).T.astype(np.float32), stc=np.sin(ac).T.astype(np.float32))
    consts["tri"] = np.triu(np.ones((T, T), np.float32))
    n_back = -(-max(w for w, _ in DILATED_PATTERNS) // T)
    kk = np.arange(T)[:, None]
    qq = np.arange(T)[None, :]
    bias = np.zeros((n_back + 1, T, T), np.float32)
    for o in range(n_back + 1):
        delta = o * T + qq - kk
        mult = np.zeros((T, T), np.float64)
        for w, dil in DILATED_PATTERNS:
            mult += (delta >= 0) & (delta <= w) & (delta % dil == 0)
        bias[o] = np.where(mult > 0, np.log2(np.maximum(mult, 1.0)), NEG_BIG)
    out = {}
    for name, v in consts.items():
        out[name] = jnp.asarray(v, BF16 if name == "tri" else F32)
    return out, jnp.asarray(bias), n_back


def _proj_weights(w_in, b_f):
    depth, d, _ = w_in.shape
    w = w_in.astype(BF16)
    sizes = [384] * 6 + [N_FOX_HEADS, 256, 256, 256]
    aq, ak, av, fq, fk, fv, fg, cq, ck, cv = jnp.split(w, np.cumsum(sizes)[:-1].tolist(), axis=2)
    rows = np.array([GATE_STRIDE * (h // 2) + h % 2 for h in range(N_FOX_HEADS)])
    fg_cols = jnp.zeros((depth, d, GATE_ROWS), BF16).at[:, :, rows].set(fg)
    bf_pad = jnp.zeros((depth, GATE_ROWS, 1), F32).at[:, rows, 0].set(b_f)
    wn = jnp.concatenate([ak, fk, ck], axis=2)
    wt = jnp.concatenate([fg_cols, aq, fq, cq, av, fv, cv], axis=2).transpose(0, 2, 1)
    return wn, wt, bf_pad


def _forward(x, w_in, b_f, lam_q1, lam_k1, lam_q2, lam_k2, g_sub, w_o, g_ffn1, w1_gate, w1_up, w1_down,
             g_mix, g_ffn2, w2_gate, w2_up, w2_down, g_final, T=512, tm=512):
    b, s, d = x.shape
    depth = w_in.shape[0]
    assert s % T == 0 and (b * s) % tm == 0 and d == HEAD_DIM * (N_DIFF_HEADS + N_FOX_HEADS + N_DIL_HEADS)
    consts, bias, n_back = _constants(s, T)
    wn_all, wt_all, bf_all = _proj_weights(w_in, b_f)
    ffn1 = (g_ffn1[:, None, :], w1_gate.astype(BF16), w1_up.astype(BF16), w1_down.astype(BF16))
    ffn2 = (g_ffn2[:, None, :], w2_gate.astype(BF16), w2_up.astype(BF16), w2_down.astype(BF16))
    g_mix3 = g_mix[:, None, :]
    wo_all = w_o.astype(BF16)
    lam_init = np.array([0.8 - 0.6 * math.exp(-0.3 * l) for l in range(depth)], np.float32)
    lam_all = jnp.zeros((depth, SUBLANES, LANES), F32)
    lam_all = lam_all.at[:, 0:4, 0:DIFF_QK_DIM].set(jnp.stack([lam_q1, lam_k1, lam_q2, lam_k2], axis=1))
    lam_all = lam_all.at[:, 4, :].set(jnp.asarray(lam_init)[:, None])
    x2d = x.reshape(b * s, d)
    for l in range(depth):
        if l == 0:
            x2d = _ffn_call(x2d, l, *ffn1, tm=tm)
        (qtd, kd, vtd, qtf, kf, vtf, cqf, qtc, kc, vtc) = _proj_call(
            x2d.reshape(b, s, d), l, g_mix3, wn_all, wt_all, bf_all, consts, T)
        o_a = _attn_call("diff", qtd, kd, vtd, (lam_all, g_sub[:, :, None]), T, layer=l)
        o_b = _attn_call("fox", qtf, kf, vtf, (cqf.reshape(b, s // T, 3, GATE_STRIDE, T),), T)
        o_c = _attn_call("dil", qtc, kc, vtc, (bias,), T, n_back=n_back)
        attn = ([o_a.reshape(b * s, -1), o_b.reshape(b * s, -1), o_c.reshape(b * s, -1)], wo_all)
        last = l == depth - 1
        x2d = _ffn_call(x2d, l, *ffn2, attn=attn, g_final=g_final if last else None, tm=tm)
        if not last:
            x2d = _ffn_call(x2d, l + 1, *ffn1, tm=tm)
    return x2d.reshape(b, s, d)


def kernel(x, w_in, b_f, lam_q1, lam_k1, lam_q2, lam_k2, g_sub, w_o, g_ffn1, w1_gate, w1_up, w1_down,
           g_mix, g_ffn2, w2_gate, w2_up, w2_down, g_final):
    return _forward(x, w_in, b_f, lam_q1, lam_k1, lam_q2, lam_k2, g_sub, w_o, g_ffn1, w1_gate, w1_up,
                    w1_down, g_mix, g_ffn2, w2_gate, w2_up, w2_down, g_final)
```

```python
import functools
import math

import numpy as np
import jax
import jax.numpy as jnp
from jax import lax
from jax.experimental import pallas as pl
from jax.experimental.pallas import tpu as pltpu

F32 = jnp.float32
BF16 = jnp.bfloat16

HEAD_DIM = 64
N_DIFF_HEADS = 6
N_FOX_HEADS = 6
N_DIL_HEADS = 4
DIFF_QK_DIM = 32
DILATED_PATTERNS = ((128, 1), (512, 4), (2048, 16))
ROPE_THETA = 10000.0
NORM_EPS = 1e-6
SUBLN_EPS = 1e-5
MACARON_SCALE = 0.5
NEG_BIG = -1e30
LOG2E = 1.4426950408889634

LANES = 128
SUBLANES = 8
MXU_WIDTH = 256
BF16_ROWS = 16
V_ROWS = HEAD_DIM + BF16_ROWS
GATE_STRIDE = 8
VMEM_LIMIT_BYTES = 56 * 1024 * 1024
PIPELINE_AHEAD = 2


def _dot(a, b):
    return jnp.dot(a, b, preferred_element_type=F32)


def _dot_nt(a, b):
    return lax.dot_general(a, b, (((1,), (1,)), ((), ())), preferred_element_type=F32)


def _rms(x, g, eps):
    ms = jnp.mean(x * x, axis=-1, keepdims=True)
    return x * lax.rsqrt(ms + eps) * g


def _split3(x):
    hi = x.astype(BF16)
    r1 = x - hi.astype(F32)
    mid = r1.astype(BF16)
    lo = (r1 - mid.astype(F32)).astype(BF16)
    return hi, mid, lo


def _ffn_body(*refs, has_attn, has_final, n_chunks):
    refs = list(refs)
    x_ref = refs.pop(0)
    if has_attn:
        oa_ref, ob_ref, oc_ref, wo_ref = refs[:4]
        refs = refs[4:]
    g_ref, wg_ref, wu_ref, wd_ref = refs[:4]
    refs = refs[4:]
    if has_final:
        gf_ref = refs.pop(0)
    out_ref = refs[0]

    x = x_ref[...]
    if has_attn:
        o = jnp.concatenate([oa_ref[...], ob_ref[...], oc_ref[...]], axis=1)
        x = x + _dot(o, wo_ref[...])
    xn = _rms(x, g_ref[...], NORM_EPS).astype(BF16)
    n_wide = wg_ref.shape[1] // MXU_WIDTH
    bounds = [MXU_WIDTH * ((n_wide * c + n_chunks - 1) // n_chunks) for c in range(n_chunks + 1)]
    acc = None
    for c in range(n_chunks):
        sl = slice(bounds[c], bounds[c + 1])
        gate = _dot(xn, wg_ref[:, sl])
        up = _dot(xn, wu_ref[:, sl])
        h = (gate * jax.nn.sigmoid(gate) * up).astype(BF16)
        d = _dot(h, wd_ref[sl, :])
        acc = d if acc is None else acc + d
    y = x + MACARON_SCALE * acc
    if has_final:
        y = _rms(y, gf_ref[...], NORM_EPS)
    out_ref[...] = y


def _const_spec(shape):
    nd = len(shape)
    return pl.BlockSpec(shape, lambda *_: (0,) * nd, pipeline_mode=pl.Buffered(1))


def _layer_spec(shape, l):
    nd = len(shape)
    return pl.BlockSpec((None,) + tuple(shape[1:]), lambda *_: (l,) + (0,) * (nd - 1),
                        pipeline_mode=pl.Buffered(1))


def _ffn_call(x2d, l, g, wg, wu, wd, attn=None, g_final=None, tm=512):
    n, d = x2d.shape
    dff = wg.shape[2]
    args = [x2d]
    specs = [pl.BlockSpec((tm, d), lambda i: (i, 0))]
    if attn is not None:
        o_list, wo = attn
        for o in o_list:
            args.append(o)
            specs.append(pl.BlockSpec((tm, o.shape[1]), lambda i: (i, 0)))
        args.append(wo)
        specs.append(_layer_spec(wo.shape, l))
    args += [g, wg, wu, wd]
    specs += [_layer_spec(g.shape, l), _layer_spec(wg.shape, l), _layer_spec(wu.shape, l),
              _layer_spec(wd.shape, l)]
    if g_final is not None:
        args.append(g_final.reshape(1, d))
        specs.append(_const_spec((1, d)))
    assert dff % MXU_WIDTH == 0
    n_chunks = 2
    body = functools.partial(_ffn_body, has_attn=attn is not None, has_final=g_final is not None,
                             n_chunks=n_chunks)
    return pl.pallas_call(
        body,
        grid=(n // tm,),
        in_specs=specs,
        out_specs=pl.BlockSpec((tm, d), lambda i: (i, 0)),
        out_shape=jax.ShapeDtypeStruct((n, d), F32),
        compiler_params=pltpu.CompilerParams(dimension_semantics=("arbitrary",),
                                             vmem_limit_bytes=VMEM_LIMIT_BYTES),
        name="ffn",
    )(*args)


W_DIFF = N_DIFF_HEADS * HEAD_DIM
W_FOX = N_FOX_HEADS * HEAD_DIM
W_DIL = N_DIL_HEADS * HEAD_DIM
P_DIFF, P_FOX, P_DIL = W_DIFF // LANES, W_FOX // LANES, W_DIL // LANES
KN_FOX, KN_DIL, N_NAT = W_DIFF, W_DIFF + W_FOX, W_DIFF + W_FOX + W_DIL
QT_FOX, QT_DIL, VT_DIFF = W_DIFF, W_DIFF + W_FOX, N_NAT
VT_FOX, VT_DIL, N_TR = N_NAT + W_DIFF, N_NAT + W_DIFF + W_FOX, 2 * N_NAT
GATE_ROWS = 32


def _proj_body(x_ref, g_ref, wn_ref, wt_ref, bf_ref, tri_ref,
               cnd_ref, snd_ref, cnc_ref, snc_ref, ctd_ref, std_ref, ctc_ref, stc_ref,
               qtd_ref, kd_ref, vtd_ref, qtf_ref, kf_ref, vtf_ref, cq_ref, qtc_ref, kc_ref, vtc_ref,
               carry_ref):
    tm = x_ref.shape[0]

    @pl.when(pl.program_id(1) == 0)
    def _():
        carry_ref[...] = jnp.zeros_like(carry_ref)

    xn = _rms(x_ref[...], g_ref[...], NORM_EPS).astype(BF16)
    rn = _dot(xn, wn_ref[...])
    rt_all = _dot_nt(wt_ref[...], xn)
    gate_t = rt_all[0:GATE_ROWS]
    rt = rt_all[GATE_ROWS:]

    lane = lax.broadcasted_iota(jnp.int32, (tm, LANES), 1)

    def rope_nat(blk, half, cos, sin_signed):
        first = (lane & half) == 0
        rot = jnp.where(first, pltpu.roll(blk, LANES - half, 1), pltpu.roll(blk, half, 1))
        return (blk * cos + rot * sin_signed).astype(BF16)

    for p in range(P_DIFF):
        kd_ref[p] = rope_nat(rn[:, LANES * p:LANES * (p + 1)], DIFF_QK_DIM // 2, cnd_ref[...], snd_ref[...])
    for p in range(P_DIL):
        c0 = KN_DIL + LANES * p
        kc_ref[p] = rope_nat(rn[:, c0:c0 + LANES], HEAD_DIM // 2, cnc_ref[...], snc_ref[...])

    def rope_tr(ref, p, r0, x1, x2, cos, sin, scale):
        half = x1.shape[0]
        ref[p, r0:r0 + half, :] = ((x1 * cos - x2 * sin) * scale).astype(BF16)
        ref[p, r0 + half:r0 + 2 * half, :] = ((x2 * cos + x1 * sin) * scale).astype(BF16)

    sd = DIFF_QK_DIM ** -0.5 * LOG2E
    hd = DIFF_QK_DIM // 2
    for g in range(2 * N_DIFF_HEADS):
        b0 = DIFF_QK_DIM * g
        rope_tr(qtd_ref, g // 4, DIFF_QK_DIM * (g % 4), rt[b0:b0 + hd], rt[b0 + hd:b0 + 2 * hd],
                ctd_ref[...], std_ref[...], sd)
    sf = HEAD_DIM ** -0.5 * LOG2E
    for p in range(P_FOX):
        b0 = QT_FOX + LANES * p
        qtf_ref[p] = (rt[b0:b0 + LANES] * sf).astype(BF16)
    hc = HEAD_DIM // 2
    for h in range(N_DIL_HEADS):
        b0 = QT_DIL + HEAD_DIM * h
        rope_tr(qtc_ref, h // 2, HEAD_DIM * (h % 2), rt[b0:b0 + hc], rt[b0 + hc:b0 + 2 * hc],
                ctc_ref[...], stc_ref[...], sf)

    ones_rows = jnp.ones((BF16_ROWS, tm), BF16)

    def put_vt(ref, base, n_pairs):
        for p in range(n_pairs):
            for h in range(2):
                b0 = base + LANES * p + HEAD_DIM * h
                ref[p, V_ROWS * h:V_ROWS * h + HEAD_DIM, :] = rt[b0:b0 + HEAD_DIM].astype(BF16)
                ref[p, V_ROWS * h + HEAD_DIM:V_ROWS * (h + 1), :] = ones_rows

    put_vt(vtd_ref, VT_DIFF, P_DIFF)
    put_vt(vtf_ref, VT_FOX, P_FOX)
    put_vt(vtc_ref, VT_DIL, P_DIL)

    z = gate_t + bf_ref[...]
    logf = (jnp.minimum(z, 0.0) - jnp.log1p(jnp.exp(-jnp.abs(z)))) * LOG2E
    part = _dot(jnp.concatenate(_split3(logf), axis=0), tri_ref[...])
    c = part[0:GATE_ROWS] + part[GATE_ROWS:2 * GATE_ROWS] + part[2 * GATE_ROWS:3 * GATE_ROWS] + carry_ref[...]
    carry_ref[...] = c[:, tm - 1:tm]
    cq_ref[...] = c[0:P_FOX * GATE_STRIDE]
    row = lax.broadcasted_iota(jnp.int32, (LANES, tm), 0)
    ones_pattern = jnp.where((row < 3 * BF16_ROWS) & ((row & (BF16_ROWS - 1)) == 0), 1.0, 0.0)
    for p in range(P_FOX):
        kf_ref[p, :, 0:LANES] = rn[:, KN_FOX + LANES * p:KN_FOX + LANES * (p + 1)].astype(BF16)
        auxt = ones_pattern
        for h in range(2):
            ch = c[GATE_STRIDE * p + h:GATE_STRIDE * p + h + 1]
            for t, term in enumerate(_split3(ch)):
                auxt = jnp.where(row == (3 + h) * BF16_ROWS + t, -term.astype(F32), auxt)
        kf_ref[p, :, LANES:2 * LANES] = auxt.T.astype(BF16)


def _proj_call(x, l, g_mix, wn, wt, bf_pad, consts, T):
    b, s, d = x.shape
    nt = s // T
    tile5 = lambda npair, r, c: pl.BlockSpec((None, npair, None, r, c), lambda bi, ti: (bi, 0, ti, 0, 0))
    nat_tab = pl.BlockSpec((T, LANES), lambda bi, ti: (ti, 0))
    tr_tab = lambda r: pl.BlockSpec((r, T), lambda bi, ti: (0, ti))
    in_specs = [
        pl.BlockSpec((None, T, d), lambda bi, ti: (bi, ti, 0)),
        _layer_spec(g_mix.shape, l), _layer_spec(wn.shape, l), _layer_spec(wt.shape, l),
        _layer_spec(bf_pad.shape, l), _const_spec((T, T)),
        nat_tab, nat_tab, nat_tab, nat_tab,
        tr_tab(DIFF_QK_DIM // 2), tr_tab(DIFF_QK_DIM // 2), tr_tab(HEAD_DIM // 2), tr_tab(HEAD_DIM // 2),
    ]
    out_shape = [
        jax.ShapeDtypeStruct((b, P_DIFF, nt, LANES, T), BF16),
        jax.ShapeDtypeStruct((b, P_DIFF, nt, T, LANES), BF16),
        jax.ShapeDtypeStruct((b, P_DIFF, nt, 2 * V_ROWS, T), BF16),
        jax.ShapeDtypeStruct((b, P_FOX, nt, LANES, T), BF16),
        jax.ShapeDtypeStruct((b, P_FOX, nt, T, 2 * LANES), BF16),
        jax.ShapeDtypeStruct((b, P_FOX, nt, 2 * V_ROWS, T), BF16),
        jax.ShapeDtypeStruct((b, nt, P_FOX * GATE_STRIDE, T), F32),
        jax.ShapeDtypeStruct((b, P_DIL, nt, LANES, T), BF16),
        jax.ShapeDtypeStruct((b, P_DIL, nt, T, LANES), BF16),
        jax.ShapeDtypeStruct((b, P_DIL, nt, 2 * V_ROWS, T), BF16),
    ]
    out_specs = [
        tile5(P_DIFF, LANES, T), tile5(P_DIFF, T, LANES), tile5(P_DIFF, 2 * V_ROWS, T),
        tile5(P_FOX, LANES, T), tile5(P_FOX, T, 2 * LANES), tile5(P_FOX, 2 * V_ROWS, T),
        pl.BlockSpec((None, None, P_FOX * GATE_STRIDE, T), lambda bi, ti: (bi, ti, 0, 0)),
        tile5(P_DIL, LANES, T), tile5(P_DIL, T, LANES), tile5(P_DIL, 2 * V_ROWS, T),
    ]
    return pl.pallas_call(
        _proj_body,
        grid=(b, nt),
        in_specs=in_specs,
        out_specs=out_specs,
        out_shape=out_shape,
        scratch_shapes=[pltpu.VMEM((GATE_ROWS, 1), F32)],
        compiler_params=pltpu.CompilerParams(dimension_semantics=("arbitrary", "arbitrary"),
                                             vmem_limit_bytes=VMEM_LIMIT_BYTES),
        name="proj",
    )(x, g_mix, wn, wt, bf_pad, consts["tri"], consts["cnd"], consts["snd"], consts["cnc"], consts["snc"],
      consts["ctd"], consts["std"], consts["ctc"], consts["stc"])


def _attn_body(*refs, mode, n_back):
    if mode == "fox":
        qt_ref, k_ref, vt_ref, cq_ref, o_ref, qe_ref, m_ref, acc_ref, s_ref, tmax_ref = refs
    elif mode == "diff":
        qt_ref, k_ref, vt_ref, lam_ref, gcol_ref, o_ref, qe_ref, m_ref, acc_ref, s_ref, tmax_ref = refs
    else:
        qt_ref, k_ref, vt_ref, bias_ref, o_ref, qe_ref, m_ref, acc_ref, s_ref, tmax_ref = refs
    n_pairs, nt, _, T = qt_ref.shape
    n_sc = s_ref.shape[0]
    per_pair = n_sc // n_pairs
    rows = LANES // per_pair
    q_tiles = (pl.program_id(2), nt - 1 - pl.program_id(2))

    def first_tile(i):
        return jnp.maximum(i - n_back, 0) if mode == "dil" else 0

    def setup(st, i):
        for sc in range(n_sc):
            p, r = divmod(sc, per_pair)
            qe_ref[st, sc, rows * r:rows * (r + 1), :] = qt_ref[p, i, rows * r:rows * (r + 1), :]
        if mode == "fox":
            for sc in range(n_sc):
                p, h = divmod(sc, 2)
                terms = _split3(cq_ref[i, p, h:h + 1, :])
                blocks = [jnp.broadcast_to(t.astype(F32), (BF16_ROWS, T)) for t in terms]
                blocks += [jnp.full((BF16_ROWS, T), 1.0 if h == hh else 0.0, F32) for hh in range(2)]
                blocks += [jnp.zeros((LANES - 5 * BF16_ROWS, T), F32)]
                qe_ref[st, sc, LANES:2 * LANES, :] = jnp.concatenate(blocks, axis=0).astype(BF16)
        m_ref[st] = jnp.full(m_ref.shape[1:], NEG_BIG, F32)
        acc_ref[st] = jnp.zeros(acc_ref.shape[1:], F32)

    half = T // 2
    full_blocks = ((slice(0, T), slice(0, T)),)
    diag_blocks = ((slice(0, half), slice(0, half)), (slice(0, T), slice(half, T)))

    def produce(st, sc, j, diag=False):
        for rk, cq in (diag_blocks if diag else full_blocks):
            s = _dot(k_ref[sc // per_pair, j, rk, :], qe_ref[st, sc, :, cq])
            if mode == "dil":
                s = s + bias_ref[q_tiles[st] - j, rk, cq]
            s_ref[sc, rk, cq] = s
            if mode == "dil" or not diag:
                tmax_ref[sc, :, cq] = jnp.broadcast_to(jnp.max(s, axis=0, keepdims=True),
                                                       (SUBLANES, s.shape[1]))

    def consume(st, sc, j, diag):
        p, r = divmod(sc, per_pair)
        h = r * 2 // per_pair
        for rk, cq in (diag_blocks if diag else full_blocks):
            s = s_ref[sc, rk, cq]
            if diag and mode != "dil":
                kk = lax.broadcasted_iota(jnp.int32, s.shape, 0)
                qq = lax.broadcasted_iota(jnp.int32, s.shape, 1) + cq.start
                s = jnp.where(kk <= qq, s, NEG_BIG)
                tmax = jnp.max(s, axis=0, keepdims=True)
            else:
                tmax = tmax_ref[sc, 0:1, cq]
            m_old = m_ref[st, sc, 0:1, cq]
            m_new = jnp.maximum(m_old, tmax)
            alpha = jnp.exp2(m_old - m_new)
            pr = jnp.exp2(s - m_new).astype(BF16)
            pv = _dot(vt_ref[p, j, V_ROWS * h:V_ROWS * (h + 1), rk], pr)
            acc_ref[st, sc, :, cq] = acc_ref[st, sc, :, cq] * alpha + pv
            m_ref[st, sc, :, cq] = jnp.broadcast_to(m_new, (SUBLANES, s.shape[1]))

    ahead = min(PIPELINE_AHEAD, n_sc - 1)

    def run_tile(st, j, diag):
        for sc in range(n_sc):
            if sc + ahead < n_sc:
                produce(st, sc + ahead, j, diag)
            elif not diag:
                produce(st, sc + ahead - n_sc, j + 1)
            elif st == 0:
                produce(1, sc + ahead - n_sc, first_tile(q_tiles[1]))
            consume(st, sc, j, diag)

    def finalize(st, i):
        def normalised(sc):
            a = acc_ref[st, sc]
            return a[0:HEAD_DIM] / a[HEAD_DIM:HEAD_DIM + 1]

        if mode == "diff":
            lam_init = lam_ref[4:5, 0:1]
            lam = (jnp.exp(jnp.sum(lam_ref[0:1] * lam_ref[1:2], axis=1, keepdims=True))
                   - jnp.exp(jnp.sum(lam_ref[2:3] * lam_ref[3:4], axis=1, keepdims=True)) + lam_init)
        for p in range(n_pairs):
            outs = []
            for h in range(2):
                if mode == "diff":
                    sc = per_pair * p + 2 * h
                    d = normalised(sc) - lam * normalised(sc + 1)
                    ms = jnp.mean(d * d, axis=0, keepdims=True)
                    outs.append(d * lax.rsqrt(ms + SUBLN_EPS) * gcol_ref[...] * (1.0 - lam_init))
                else:
                    outs.append(normalised(per_pair * p + h))
            o_ref[pl.ds(pl.multiple_of(i * T, T), T), LANES * p:LANES * (p + 1)] = (
                jnp.concatenate(outs, axis=0).T.astype(BF16))

    @pl.when((pl.program_id(0) == 0) & (pl.program_id(1) == 0) & (pl.program_id(2) == 0))
    def _():
        qe_ref[...] = jnp.zeros(qe_ref.shape, BF16)

    for st in range(2):
        setup(st, q_tiles[st])
    for sc in range(ahead):
        produce(0, sc, first_tile(q_tiles[0]))
    for st in range(2):
        i = q_tiles[st]

        def step(j, carry, st=st):
            run_tile(st, j, False)
            return carry

        lax.fori_loop(first_tile(i), i, step, 0)
        run_tile(st, i, True)
    for st in range(2):
        finalize(st, q_tiles[st])


def _attn_call(mode, qt, k, vt, extra, T, n_back=0, pairs_per_step=None, layer=0):
    b, npair, nt = qt.shape[:3]
    assert nt % 2 == 0
    pps = npair if pairs_per_step is None else pairs_per_step
    r = k.shape[-1]
    n_sc = (4 if mode == "diff" else 2) * pps
    in_specs = [
        pl.BlockSpec((None, pps, nt, LANES, T), lambda bi, p, i: (bi, p, 0, 0, 0)),
        pl.BlockSpec((None, pps, nt, T, r), lambda bi, p, i: (bi, p, 0, 0, 0)),
        pl.BlockSpec((None, pps, nt, 2 * V_ROWS, T), lambda bi, p, i: (bi, p, 0, 0, 0)),
    ]
    if mode == "fox":
        in_specs.append(pl.BlockSpec((None, nt, pps, GATE_STRIDE, T), lambda bi, p, i: (bi, 0, p, 0, 0)))
    elif mode == "diff":
        in_specs += [_layer_spec(extra[0].shape, layer), _layer_spec(extra[1].shape, layer)]
    else:
        in_specs.append(_const_spec(extra[0].shape))
    body = functools.partial(_attn_body, mode=mode, n_back=n_back)
    return pl.pallas_call(
        body,
        grid=(b, npair // pps, nt // 2),
        in_specs=in_specs,
        out_specs=pl.BlockSpec((None, nt * T, pps * LANES), lambda bi, p, i: (bi, 0, p)),
        out_shape=jax.ShapeDtypeStruct((b, nt * T, npair * LANES), BF16),
        scratch_shapes=[pltpu.VMEM((2, n_sc, r, T), BF16),
                        pltpu.VMEM((2, n_sc, SUBLANES, T), F32),
                        pltpu.VMEM((2, n_sc, V_ROWS, T), F32),
                        pltpu.VMEM((n_sc, T, T), F32),
                        pltpu.VMEM((n_sc, SUBLANES, T), F32)],
        compiler_params=pltpu.CompilerParams(dimension_semantics=("arbitrary", "arbitrary", "arbitrary"),
                                             vmem_limit_bytes=VMEM_LIMIT_BYTES),
        name="attn_" + mode,
    )(qt, k, vt, *extra)


def _constants(s, T):
    pos = np.arange(s, dtype=np.float64)[:, None]

    def angles(half):
        inv = ROPE_THETA ** (-np.arange(half, dtype=np.float64) / half)
        return pos * inv[None, :]

    ad, ac = angles(DIFF_QK_DIM // 2), angles(HEAD_DIM // 2)

    def nat(a):
        reps = LANES // (2 * a.shape[1])
        cos = np.tile(np.concatenate([np.cos(a), np.cos(a)], axis=1), (1, reps))
        sin = np.tile(np.concatenate([-np.sin(a), np.sin(a)], axis=1), (1, reps))
        return cos.astype(np.float32), sin.astype(np.float32)

    cnd, snd = nat(ad)
    cnc, snc = nat(ac)
    consts = dict(cnd=cnd, snd=snd, cnc=cnc, snc=snc,
                  ctd=np.cos(ad).T.astype(np.float32), std=np.sin(ad).T.astype(np.float32),
                  ctc=np.cos(ac).T.astype(np.float32), stc=np.sin(ac).T.astype(np.float32))
    consts["tri"] = np.triu(np.ones((T, T), np.float32))
    n_back = -(-max(w for w, _ in DILATED_PATTERNS) // T)
    kk = np.arange(T)[:, None]
    qq = np.arange(T)[None, :]
    bias = np.zeros((n_back + 1, T, T), np.float32)
    for o in range(n_back + 1):
        delta = o * T + qq - kk
        mult = np.zeros((T, T), np.float64)
        for w, dil in DILATED_PATTERNS:
            mult += (delta >= 0) & (delta <= w) & (delta % dil == 0)
        bias[o] = np.where(mult > 0, np.log2(np.maximum(mult, 1.0)), NEG_BIG)
    out = {}
    for name, v in consts.items():
        out[name] = jnp.asarray(v, BF16 if name == "tri" else F32)
    return out, jnp.asarray(bias), n_back


def _proj_weights(w_in, b_f):
    depth, d, _ = w_in.shape
    w = w_in.astype(BF16)
    sizes = [W_DIFF] * 3 + [W_FOX] * 3 + [N_FOX_HEADS] + [W_DIL] * 3
    aq, ak, av, fq, fk, fv, fg, cq, ck, cv = jnp.split(w, np.cumsum(sizes)[:-1].tolist(), axis=2)
    rows = np.array([GATE_STRIDE * (h // 2) + h % 2 for h in range(N_FOX_HEADS)])
    fg_cols = jnp.zeros((depth, d, GATE_ROWS), BF16).at[:, :, rows].set(fg)
    bf_pad = jnp.zeros((depth, GATE_ROWS, 1), F32).at[:, rows, 0].set(b_f)
    wn = jnp.concatenate([ak, fk, ck], axis=2)
    wt = jnp.concatenate([fg_cols, aq, fq, cq, av, fv, cv], axis=2).transpose(0, 2, 1)
    return wn, wt, bf_pad


def _forward(x, w_in, b_f, lam_q1, lam_k1, lam_q2, lam_k2, g_sub, w_o, g_ffn1, w1_gate, w1_up, w1_down,
             g_mix, g_ffn2, w2_gate, w2_up, w2_down, g_final, T=512, tm=512):
    b, s, d = x.shape
    depth = w_in.shape[0]
    assert s % T == 0 and (b * s) % tm == 0 and d == HEAD_DIM * (N_DIFF_HEADS + N_FOX_HEADS + N_DIL_HEADS)
    consts, bias, n_back = _constants(s, T)
    wn_all, wt_all, bf_all = _proj_weights(w_in, b_f)
    ffn1 = (g_ffn1[:, None, :], w1_gate.astype(BF16), w1_up.astype(BF16), w1_down.astype(BF16))
    ffn2 = (g_ffn2[:, None, :], w2_gate.astype(BF16), w2_up.astype(BF16), w2_down.astype(BF16))
    g_mix3 = g_mix[:, None, :]
    wo_all = w_o.astype(BF16)
    lam_init = np.array([0.8 - 0.6 * math.exp(-0.3 * l) for l in range(depth)], np.float32)
    lam_all = jnp.zeros((depth, SUBLANES, LANES), F32)
    lam_all = lam_all.at[:, 0:4, 0:DIFF_QK_DIM].set(jnp.stack([lam_q1, lam_k1, lam_q2, lam_k2], axis=1))
    lam_all = lam_all.at[:, 4, :].set(jnp.asarray(lam_init)[:, None])
    x2d = x.reshape(b * s, d)
    for l in range(depth):
        if l == 0:
            x2d = _ffn_call(x2d, l, *ffn1, tm=tm)
        (qtd, kd, vtd, qtf, kf, vtf, cqf, qtc, kc, vtc) = _proj_call(
            x2d.reshape(b, s, d), l, g_mix3, wn_all, wt_all, bf_all, consts, T)
        o_a = _attn_call("diff", qtd, kd, vtd, (lam_all, g_sub[:, :, None]), T, layer=l)
        o_b = _attn_call("fox", qtf, kf, vtf, (cqf.reshape(b, s // T, P_FOX, GATE_STRIDE, T),), T)
        o_c = _attn_call("dil", qtc, kc, vtc, (bias,), T, n_back=n_back)
        attn = ([o_a.reshape(b * s, -1), o_b.reshape(b * s, -1), o_c.reshape(b * s, -1)], wo_all)
        last = l == depth - 1
        x2d = _ffn_call(x2d, l, *ffn2, attn=attn, g_final=g_final if last else None, tm=tm)
        if not last:
            x2d = _ffn_call(x2d, l + 1, *ffn1, tm=tm)
    return x2d.reshape(b, s, d)


def kernel(x, w_in, b_f, lam_q1, lam_k1, lam_q2, lam_k2, g_sub, w_o, g_ffn1, w1_gate, w1_up, w1_down,
           g_mix, g_ffn2, w2_gate, w2_up, w2_down, g_final):
    return _forward(x, w_in, b_f, lam_q1, lam_k1, lam_q2, lam_k2, g_sub, w_o, g_ffn1, w1_gate, w1_up,
                    w1_down, g_mix, g_ffn2, w2_gate, w2_up, w2_down, g_final)
```

```python
import functools
import math

import numpy as np
import jax
import jax.numpy as jnp
from jax import lax
from jax.experimental import pallas as pl
from jax.experimental.pallas import tpu as pltpu

F32 = jnp.float32
BF16 = jnp.bfloat16

HEAD_DIM = 64
N_DIFF_HEADS = 6
N_FOX_HEADS = 6
N_DIL_HEADS = 4
DIFF_QK_DIM = 32
DILATED_PATTERNS = ((128, 1), (512, 4), (2048, 16))
ROPE_THETA = 10000.0
NORM_EPS = 1e-6
SUBLN_EPS = 1e-5
MACARON_SCALE = 0.5
NEG_BIG = -1e30
LOG2E = 1.4426950408889634

LANES = 128
SUBLANES = 8
MXU_WIDTH = 256
BF16_ROWS = 16
V_ROWS = HEAD_DIM + BF16_ROWS
GATE_STRIDE = 8
VMEM_LIMIT_BYTES = 56 * 1024 * 1024
PIPELINE_AHEAD = 2


def _dot(a, b):
    return jnp.dot(a, b, preferred_element_type=F32)


def _dot_nt(a, b):
    return lax.dot_general(a, b, (((1,), (1,)), ((), ())), preferred_element_type=F32)


def _rms(x, g, eps):
    ms = jnp.mean(x * x, axis=-1, keepdims=True)
    return x * lax.rsqrt(ms + eps) * g


def _split3(x):
    hi = x.astype(BF16)
    r1 = x - hi.astype(F32)
    mid = r1.astype(BF16)
    lo = (r1 - mid.astype(F32)).astype(BF16)
    return hi, mid, lo


def _ffn_body(*refs, has_attn, has_final, n_chunks):
    refs = list(refs)
    x_ref = refs.pop(0)
    if has_attn:
        oa_ref, ob_ref, oc_ref, wo_ref = refs[:4]
        refs = refs[4:]
    g_ref, wg_ref, wu_ref, wd_ref = refs[:4]
    refs = refs[4:]
    if has_final:
        gf_ref = refs.pop(0)
    out_ref = refs[0]

    x = x_ref[...]
    if has_attn:
        o = jnp.concatenate([oa_ref[...], ob_ref[...], oc_ref[...]], axis=1)
        x = x + _dot(o, wo_ref[...])
    xn = _rms(x, g_ref[...], NORM_EPS).astype(BF16)
    n_wide = wg_ref.shape[1] // MXU_WIDTH
    bounds = [MXU_WIDTH * ((n_wide * c + n_chunks - 1) // n_chunks) for c in range(n_chunks + 1)]
    acc = None
    for c in range(n_chunks):
        sl = slice(bounds[c], bounds[c + 1])
        gate = _dot(xn, wg_ref[:, sl])
        up = _dot(xn, wu_ref[:, sl])
        h = (gate * jax.nn.sigmoid(gate) * up).astype(BF16)
        d = _dot(h, wd_ref[sl, :])
        acc = d if acc is None else acc + d
    y = x + MACARON_SCALE * acc
    if has_final:
        y = _rms(y, gf_ref[...], NORM_EPS)
    out_ref[...] = y


def _const_spec(shape):
    nd = len(shape)
    return pl.BlockSpec(shape, lambda *_: (0,) * nd, pipeline_mode=pl.Buffered(1))


def _layer_spec(shape, l):
    nd = len(shape)
    return pl.BlockSpec((None,) + tuple(shape[1:]), lambda *_: (l,) + (0,) * (nd - 1),
                        pipeline_mode=pl.Buffered(1))


def _ffn_call(x2d, l, g, wg, wu, wd, attn=None, g_final=None, tm=512):
    n, d = x2d.shape
    dff = wg.shape[2]
    args = [x2d]
    specs = [pl.BlockSpec((tm, d), lambda i: (i, 0))]
    if attn is not None:
        o_list, wo = attn
        for o in o_list:
            args.append(o)
            specs.append(pl.BlockSpec((tm, o.shape[1]), lambda i: (i, 0)))
        args.append(wo)
        specs.append(_layer_spec(wo.shape, l))
    args += [g, wg, wu, wd]
    specs += [_layer_spec(g.shape, l), _layer_spec(wg.shape, l), _layer_spec(wu.shape, l),
              _layer_spec(wd.shape, l)]
    if g_final is not None:
        args.append(g_final.reshape(1, d))
        specs.append(_const_spec((1, d)))
    assert dff % MXU_WIDTH == 0
    n_chunks = dff // MXU_WIDTH
    body = functools.partial(_ffn_body, has_attn=attn is not None, has_final=g_final is not None,
                             n_chunks=n_chunks)
    return pl.pallas_call(
        body,
        grid=(n // tm,),
        in_specs=specs,
        out_specs=pl.BlockSpec((tm, d), lambda i: (i, 0)),
        out_shape=jax.ShapeDtypeStruct((n, d), F32),
        compiler_params=pltpu.CompilerParams(dimension_semantics=("arbitrary",),
                                             vmem_limit_bytes=VMEM_LIMIT_BYTES),
        name="ffn",
    )(*args)


W_DIFF = N_DIFF_HEADS * HEAD_DIM
W_FOX = N_FOX_HEADS * HEAD_DIM
W_DIL = N_DIL_HEADS * HEAD_DIM
P_DIFF, P_FOX, P_DIL = W_DIFF // LANES, W_FOX // LANES, W_DIL // LANES
KN_FOX, KN_DIL, N_NAT = W_DIFF, W_DIFF + W_FOX, W_DIFF + W_FOX + W_DIL
QT_FOX, QT_DIL, VT_DIFF = W_DIFF, W_DIFF + W_FOX, N_NAT
VT_FOX, VT_DIL, N_TR = N_NAT + W_DIFF, N_NAT + W_DIFF + W_FOX, 2 * N_NAT
GATE_ROWS = 32


def _proj_body(x_ref, g_ref, wn_ref, wt_ref, bf_ref, tri_ref,
               cnd_ref, snd_ref, cnc_ref, snc_ref, ctd_ref, std_ref, ctc_ref, stc_ref,
               qtd_ref, kd_ref, vtd_ref, qtf_ref, kf_ref, vtf_ref, cq_ref, qtc_ref, kc_ref, vtc_ref,
               carry_ref):
    tm = x_ref.shape[0]

    @pl.when(pl.program_id(1) == 0)
    def _():
        carry_ref[...] = jnp.zeros_like(carry_ref)

    xn = _rms(x_ref[...], g_ref[...], NORM_EPS).astype(BF16)
    rn = _dot(xn, wn_ref[...])
    rt_all = _dot_nt(wt_ref[...], xn)
    gate_t = rt_all[0:GATE_ROWS]
    rt = rt_all[GATE_ROWS:]

    lane = lax.broadcasted_iota(jnp.int32, (tm, LANES), 1)

    def rope_nat(blk, half, cos, sin_signed):
        first = (lane & half) == 0
        rot = jnp.where(first, pltpu.roll(blk, LANES - half, 1), pltpu.roll(blk, half, 1))
        return (blk * cos + rot * sin_signed).astype(BF16)

    for p in range(P_DIFF):
        kd_ref[p] = rope_nat(rn[:, LANES * p:LANES * (p + 1)], DIFF_QK_DIM // 2, cnd_ref[...], snd_ref[...])
    for p in range(P_DIL):
        c0 = KN_DIL + LANES * p
        kc_ref[p] = rope_nat(rn[:, c0:c0 + LANES], HEAD_DIM // 2, cnc_ref[...], snc_ref[...])

    def rope_tr(ref, p, r0, x1, x2, cos, sin, scale):
        half = x1.shape[0]
        ref[p, r0:r0 + half, :] = ((x1 * cos - x2 * sin) * scale).astype(BF16)
        ref[p, r0 + half:r0 + 2 * half, :] = ((x2 * cos + x1 * sin) * scale).astype(BF16)

    sd = DIFF_QK_DIM ** -0.5 * LOG2E
    hd = DIFF_QK_DIM // 2
    for g in range(2 * N_DIFF_HEADS):
        b0 = DIFF_QK_DIM * g
        rope_tr(qtd_ref, g // 4, DIFF_QK_DIM * (g % 4), rt[b0:b0 + hd], rt[b0 + hd:b0 + 2 * hd],
                ctd_ref[...], std_ref[...], sd)
    sf = HEAD_DIM ** -0.5 * LOG2E
    for p in range(P_FOX):
        b0 = QT_FOX + LANES * p
        qtf_ref[p] = (rt[b0:b0 + LANES] * sf).astype(BF16)
    hc = HEAD_DIM // 2
    for h in range(N_DIL_HEADS):
        b0 = QT_DIL + HEAD_DIM * h
        rope_tr(qtc_ref, h // 2, HEAD_DIM * (h % 2), rt[b0:b0 + hc], rt[b0 + hc:b0 + 2 * hc],
                ctc_ref[...], stc_ref[...], sf)

    ones_rows = jnp.ones((BF16_ROWS, tm), BF16)

    def put_vt(ref, base, n_pairs):
        for p in range(n_pairs):
            for h in range(2):
                b0 = base + LANES * p + HEAD_DIM * h
                ref[p, V_ROWS * h:V_ROWS * h + HEAD_DIM, :] = rt[b0:b0 + HEAD_DIM].astype(BF16)
                ref[p, V_ROWS * h + HEAD_DIM:V_ROWS * (h + 1), :] = ones_rows

    put_vt(vtd_ref, VT_DIFF, P_DIFF)
    put_vt(vtf_ref, VT_FOX, P_FOX)
    put_vt(vtc_ref, VT_DIL, P_DIL)

    z = gate_t + bf_ref[...]
    logf = (jnp.minimum(z, 0.0) - jnp.log1p(jnp.exp(-jnp.abs(z)))) * LOG2E
    part = _dot(jnp.concatenate(_split3(logf), axis=0), tri_ref[...])
    c = part[0:GATE_ROWS] + part[GATE_ROWS:2 * GATE_ROWS] + part[2 * GATE_ROWS:3 * GATE_ROWS] + carry_ref[...]
    carry_ref[...] = c[:, tm - 1:tm]
    cq_ref[...] = c[0:P_FOX * GATE_STRIDE]
    row = lax.broadcasted_iota(jnp.int32, (LANES, tm), 0)
    ones_pattern = jnp.where((row < 3 * BF16_ROWS) & ((row & (BF16_ROWS - 1)) == 0), 1.0, 0.0)
    for p in range(P_FOX):
        kf_ref[p, :, 0:LANES] = rn[:, KN_FOX + LANES * p:KN_FOX + LANES * (p + 1)].astype(BF16)
        auxt = ones_pattern
        for h in range(2):
            ch = c[GATE_STRIDE * p + h:GATE_STRIDE * p + h + 1]
            for t, term in enumerate(_split3(ch)):
                auxt = jnp.where(row == (3 + h) * BF16_ROWS + t, -term.astype(F32), auxt)
        kf_ref[p, :, LANES:2 * LANES] = auxt.T.astype(BF16)


def _proj_call(x, l, g_mix, wn, wt, bf_pad, consts, T):
    b, s, d = x.shape
    nt = s // T
    tile5 = lambda npair, r, c: pl.BlockSpec((None, npair, None, r, c), lambda bi, ti: (bi, 0, ti, 0, 0))
    nat_tab = pl.BlockSpec((T, LANES), lambda bi, ti: (ti, 0))
    tr_tab = lambda r: pl.BlockSpec((r, T), lambda bi, ti: (0, ti))
    in_specs = [
        pl.BlockSpec((None, T, d), lambda bi, ti: (bi, ti, 0)),
        _layer_spec(g_mix.shape, l), _layer_spec(wn.shape, l), _layer_spec(wt.shape, l),
        _layer_spec(bf_pad.shape, l), _const_spec((T, T)),
        nat_tab, nat_tab, nat_tab, nat_tab,
        tr_tab(DIFF_QK_DIM // 2), tr_tab(DIFF_QK_DIM // 2), tr_tab(HEAD_DIM // 2), tr_tab(HEAD_DIM // 2),
    ]
    out_shape = [
        jax.ShapeDtypeStruct((b, P_DIFF, nt, LANES, T), BF16),
        jax.ShapeDtypeStruct((b, P_DIFF, nt, T, LANES), BF16),
        jax.ShapeDtypeStruct((b, P_DIFF, nt, 2 * V_ROWS, T), BF16),
        jax.ShapeDtypeStruct((b, P_FOX, nt, LANES, T), BF16),
        jax.ShapeDtypeStruct((b, P_FOX, nt, T, 2 * LANES), BF16),
        jax.ShapeDtypeStruct((b, P_FOX, nt, 2 * V_ROWS, T), BF16),
        jax.ShapeDtypeStruct((b, nt, P_FOX * GATE_STRIDE, T), F32),
        jax.ShapeDtypeStruct((b, P_DIL, nt, LANES, T), BF16),
        jax.ShapeDtypeStruct((b, P_DIL, nt, T, LANES), BF16),
        jax.ShapeDtypeStruct((b, P_DIL, nt, 2 * V_ROWS, T), BF16),
    ]
    out_specs = [
        tile5(P_DIFF, LANES, T), tile5(P_DIFF, T, LANES), tile5(P_DIFF, 2 * V_ROWS, T),
        tile5(P_FOX, LANES, T), tile5(P_FOX, T, 2 * LANES), tile5(P_FOX, 2 * V_ROWS, T),
        pl.BlockSpec((None, None, P_FOX * GATE_STRIDE, T), lambda bi, ti: (bi, ti, 0, 0)),
        tile5(P_DIL, LANES, T), tile5(P_DIL, T, LANES), tile5(P_DIL, 2 * V_ROWS, T),
    ]
    return pl.pallas_call(
        _proj_body,
        grid=(b, nt),
        in_specs=in_specs,
        out_specs=out_specs,
        out_shape=out_shape,
        scratch_shapes=[pltpu.VMEM((GATE_ROWS, 1), F32)],
        compiler_params=pltpu.CompilerParams(dimension_semantics=("arbitrary", "arbitrary"),
                                             vmem_limit_bytes=VMEM_LIMIT_BYTES),
        name="proj",
    )(x, g_mix, wn, wt, bf_pad, consts["tri"], consts["cnd"], consts["snd"], consts["cnc"], consts["snc"],
      consts["ctd"], consts["std"], consts["ctc"], consts["stc"])


def _attn_body(*refs, mode, n_back):
    if mode == "fox":
        qt_ref, k_ref, vt_ref, cq_ref, o_ref, qe_ref, m_ref, acc_ref, s_ref, tmax_ref = refs
    elif mode == "diff":
        qt_ref, k_ref, vt_ref, lam_ref, gcol_ref, o_ref, qe_ref, m_ref, acc_ref, s_ref, tmax_ref = refs
    else:
        qt_ref, k_ref, vt_ref, bias_ref, o_ref, qe_ref, m_ref, acc_ref, s_ref, tmax_ref = refs
    n_pairs, nt, _, T = qt_ref.shape
    n_sc = s_ref.shape[0]
    per_pair = n_sc // n_pairs
    rows = LANES // per_pair
    q_tiles = (pl.program_id(2), nt - 1 - pl.program_id(2))

    def first_tile(i):
        return jnp.maximum(i - n_back, 0) if mode == "dil" else 0

    def setup(st, i):
        for sc in range(n_sc):
            p, r = divmod(sc, per_pair)
            qe_ref[st, sc, rows * r:rows * (r + 1), :] = qt_ref[p, i, rows * r:rows * (r + 1), :]
        if mode == "fox":
            for sc in range(n_sc):
                p, h = divmod(sc, 2)
                terms = _split3(cq_ref[i, p, h:h + 1, :])
                blocks = [jnp.broadcast_to(t.astype(F32), (BF16_ROWS, T)) for t in terms]
                blocks += [jnp.full((BF16_ROWS, T), 1.0 if h == hh else 0.0, F32) for hh in range(2)]
                blocks += [jnp.zeros((LANES - 5 * BF16_ROWS, T), F32)]
                qe_ref[st, sc, LANES:2 * LANES, :] = jnp.concatenate(blocks, axis=0).astype(BF16)
        m_ref[st] = jnp.full(m_ref.shape[1:], NEG_BIG, F32)
        acc_ref[st] = jnp.zeros(acc_ref.shape[1:], F32)

    half = T // 2
    full_blocks = ((slice(0, T), slice(0, T)),)
    diag_blocks = ((slice(0, half), slice(0, half)), (slice(0, T), slice(half, T)))

    def produce(st, sc, j, diag=False):
        for rk, cq in (diag_blocks if diag else full_blocks):
            s = _dot(k_ref[sc // per_pair, j, rk, :], qe_ref[st, sc, :, cq])
            if mode == "dil":
                s = s + bias_ref[q_tiles[st] - j, rk, cq]
            s_ref[sc, rk, cq] = s
            if mode == "dil" or not diag:
                tmax_ref[sc, :, cq] = jnp.broadcast_to(jnp.max(s, axis=0, keepdims=True),
                                                       (SUBLANES, s.shape[1]))

    def consume(st, sc, j, diag):
        p, r = divmod(sc, per_pair)
        h = r * 2 // per_pair
        for rk, cq in (diag_blocks if diag else full_blocks):
            s = s_ref[sc, rk, cq]
            if diag and mode != "dil":
                kk = lax.broadcasted_iota(jnp.int32, s.shape, 0)
                qq = lax.broadcasted_iota(jnp.int32, s.shape, 1) + cq.start
                s = jnp.where(kk <= qq, s, NEG_BIG)
                tmax = jnp.max(s, axis=0, keepdims=True)
            else:
                tmax = tmax_ref[sc, 0:1, cq]
            m_old = m_ref[st, sc, 0:1, cq]
            m_new = jnp.maximum(m_old, tmax)
            alpha = jnp.exp2(m_old - m_new)
            pr = jnp.exp2(s - m_new).astype(BF16)
            pv = _dot(vt_ref[p, j, V_ROWS * h:V_ROWS * (h + 1), rk], pr)
            acc_ref[st, sc, :, cq] = acc_ref[st, sc, :, cq] * alpha + pv
            m_ref[st, sc, :, cq] = jnp.broadcast_to(m_new, (SUBLANES, s.shape[1]))

    ahead = min(PIPELINE_AHEAD, n_sc - 1)

    def run_tile(st, j, diag):
        for sc in range(n_sc):
            if sc + ahead < n_sc:
                produce(st, sc + ahead, j, diag)
            elif not diag:
                produce(st, sc + ahead - n_sc, j + 1)
            elif st == 0:
                produce(1, sc + ahead - n_sc, first_tile(q_tiles[1]))
            consume(st, sc, j, diag)

    def finalize(st, i):
        def normalised(sc):
            a = acc_ref[st, sc]
            return a[0:HEAD_DIM] / a[HEAD_DIM:HEAD_DIM + 1]

        if mode == "diff":
            lam_init = lam_ref[4:5, 0:1]
            lam = (jnp.exp(jnp.sum(lam_ref[0:1] * lam_ref[1:2], axis=1, keepdims=True))
                   - jnp.exp(jnp.sum(lam_ref[2:3] * lam_ref[3:4], axis=1, keepdims=True)) + lam_init)
        for p in range(n_pairs):
            outs = []
            for h in range(2):
                if mode == "diff":
                    sc = per_pair * p + 2 * h
                    d = normalised(sc) - lam * normalised(sc + 1)
                    ms = jnp.mean(d * d, axis=0, keepdims=True)
                    outs.append(d * lax.rsqrt(ms + SUBLN_EPS) * gcol_ref[...] * (1.0 - lam_init))
                else:
                    outs.append(normalised(per_pair * p + h))
            o_ref[pl.ds(pl.multiple_of(i * T, T), T), LANES * p:LANES * (p + 1)] = (
                jnp.concatenate(outs, axis=0).T.astype(BF16))

    @pl.when((pl.program_id(0) == 0) & (pl.program_id(1) == 0) & (pl.program_id(2) == 0))
    def _():
        qe_ref[...] = jnp.zeros(qe_ref.shape, BF16)

    for st in range(2):
        setup(st, q_tiles[st])
    for sc in range(ahead):
        produce(0, sc, first_tile(q_tiles[0]))
    for st in range(2):
        i = q_tiles[st]

        def step(j, carry, st=st):
            run_tile(st, j, False)
            return carry

        lax.fori_loop(first_tile(i), i, step, 0)
        run_tile(st, i, True)
    for st in range(2):
        finalize(st, q_tiles[st])


def _attn_call(mode, qt, k, vt, extra, T, n_back=0, pairs_per_step=None, layer=0):
    b, npair, nt = qt.shape[:3]
    assert nt % 2 == 0
    pps = npair if pairs_per_step is None else pairs_per_step
    r = k.shape[-1]
    n_sc = (4 if mode == "diff" else 2) * pps
    in_specs = [
        pl.BlockSpec((None, pps, nt, LANES, T), lambda bi, p, i: (bi, p, 0, 0, 0)),
        pl.BlockSpec((None, pps, nt, T, r), lambda bi, p, i: (bi, p, 0, 0, 0)),
        pl.BlockSpec((None, pps, nt, 2 * V_ROWS, T), lambda bi, p, i: (bi, p, 0, 0, 0)),
    ]
    if mode == "fox":
        in_specs.append(pl.BlockSpec((None, nt, pps, GATE_STRIDE, T), lambda bi, p, i: (bi, 0, p, 0, 0)))
    elif mode == "diff":
        in_specs += [_layer_spec(extra[0].shape, layer), _layer_spec(extra[1].shape, layer)]
    else:
        in_specs.append(_const_spec(extra[0].shape))
    body = functools.partial(_attn_body, mode=mode, n_back=n_back)
    return pl.pallas_call(
        body,
        grid=(b, npair // pps, nt // 2),
        in_specs=in_specs,
        out_specs=pl.BlockSpec((None, nt * T, pps * LANES), lambda bi, p, i: (bi, 0, p)),
        out_shape=jax.ShapeDtypeStruct((b, nt * T, npair * LANES), BF16),
        scratch_shapes=[pltpu.VMEM((2, n_sc, r, T), BF16),
                        pltpu.VMEM((2, n_sc, SUBLANES, T), F32),
                        pltpu.VMEM((2, n_sc, V_ROWS, T), F32),
                        pltpu.VMEM((n_sc, T, T), F32),
                        pltpu.VMEM((n_sc, SUBLANES, T), F32)],
        compiler_params=pltpu.CompilerParams(dimension_semantics=("arbitrary", "arbitrary", "arbitrary"),
                                             vmem_limit_bytes=VMEM_LIMIT_BYTES),
        name="attn_" + mode,
    )(qt, k, vt, *extra)


def _constants(s, T):
    pos = np.arange(s, dtype=np.float64)[:, None]

    def angles(half):
        inv = ROPE_THETA ** (-np.arange(half, dtype=np.float64) / half)
        return pos * inv[None, :]

    ad, ac = angles(DIFF_QK_DIM // 2), angles(HEAD_DIM // 2)

    def nat(a):
        reps = LANES // (2 * a.shape[1])
        cos = np.tile(np.concatenate([np.cos(a), np.cos(a)], axis=1), (1, reps))
        sin = np.tile(np.concatenate([-np.sin(a), np.sin(a)], axis=1), (1, reps))
        return cos.astype(np.float32), sin.astype(np.float32)

    cnd, snd = nat(ad)
    cnc, snc = nat(ac)
    consts = dict(cnd=cnd, snd=snd, cnc=cnc, snc=snc,
                  ctd=np.cos(ad).T.astype(np.float32), std=np.sin(ad).T.astype(np.float32),
                  ctc=np.cos(ac).T.astype(np.float32), stc=np.sin(ac).T.astype(np.float32))
    consts["tri"] = np.triu(np.ones((T, T), np.float32))
    n_back = -(-max(w for w, _ in DILATED_PATTERNS) // T)
    kk = np.arange(T)[:, None]
    qq = np.arange(T)[None, :]
    bias = np.zeros((n_back + 1, T, T), np.float32)
    for o in range(n_back + 1):
        delta = o * T + qq - kk
        mult = np.zeros((T, T), np.float64)
        for w, dil in DILATED_PATTERNS:
            mult += (delta >= 0) & (delta <= w) & (delta % dil == 0)
        bias[o] = np.where(mult > 0, np.log2(np.maximum(mult, 1.0)), NEG_BIG)
    out = {}
    for name, v in consts.items():
        out[name] = jnp.asarray(v, BF16 if name == "tri" else F32)
    return out, jnp.asarray(bias), n_back


def _proj_weights(w_in, b_f):
    depth, d, _ = w_in.shape
    w = w_in.astype(BF16)
    sizes = [W_DIFF] * 3 + [W_FOX] * 3 + [N_FOX_HEADS] + [W_DIL] * 3
    aq, ak, av, fq, fk, fv, fg, cq, ck, cv = jnp.split(w, np.cumsum(sizes)[:-1].tolist(), axis=2)
    rows = np.array([GATE_STRIDE * (h // 2) + h % 2 for h in range(N_FOX_HEADS)])
    fg_cols = jnp.zeros((depth, d, GATE_ROWS), BF16).at[:, :, rows].set(fg)
    bf_pad = jnp.zeros((depth, GATE_ROWS, 1), F32).at[:, rows, 0].set(b_f)
    wn = jnp.concatenate([ak, fk, ck], axis=2)
    wt = jnp.concatenate([fg_cols, aq, fq, cq, av, fv, cv], axis=2).transpose(0, 2, 1)
    return wn, wt, bf_pad


def _forward(x, w_in, b_f, lam_q1, lam_k1, lam_q2, lam_k2, g_sub, w_o, g_ffn1, w1_gate, w1_up, w1_down,
             g_mix, g_ffn2, w2_gate, w2_up, w2_down, g_final, T=512, tm=512):
    b, s, d = x.shape
    depth = w_in.shape[0]
    assert s % T == 0 and (b * s) % tm == 0 and d == HEAD_DIM * (N_DIFF_HEADS + N_FOX_HEADS + N_DIL_HEADS)
    consts, bias, n_back = _constants(s, T)
    wn_all, wt_all, bf_all = _proj_weights(w_in, b_f)
    ffn1 = (g_ffn1[:, None, :], w1_gate.astype(BF16), w1_up.astype(BF16), w1_down.astype(BF16))
    ffn2 = (g_ffn2[:, None, :], w2_gate.astype(BF16), w2_up.astype(BF16), w2_down.astype(BF16))
    g_mix3 = g_mix[:, None, :]
    wo_all = w_o.astype(BF16)
    lam_init = np.array([0.8 - 0.6 * math.exp(-0.3 * l) for l in range(depth)], np.float32)
    lam_all = jnp.zeros((depth, SUBLANES, LANES), F32)
    lam_all = lam_all.at[:, 0:4, 0:DIFF_QK_DIM].set(jnp.stack([lam_q1, lam_k1, lam_q2, lam_k2], axis=1))
    lam_all = lam_all.at[:, 4, :].set(jnp.asarray(lam_init)[:, None])
    x2d = x.reshape(b * s, d)
    for l in range(depth):
        if l == 0:
            x2d = _ffn_call(x2d, l, *ffn1, tm=tm)
        (qtd, kd, vtd, qtf, kf, vtf, cqf, qtc, kc, vtc) = _proj_call(
            x2d.reshape(b, s, d), l, g_mix3, wn_all, wt_all, bf_all, consts, T)
        o_a = _attn_call("diff", qtd, kd, vtd, (lam_all, g_sub[:, :, None]), T, layer=l)
        o_b = _attn_call("fox", qtf, kf, vtf, (cqf.reshape(b, s // T, P_FOX, GATE_STRIDE, T),), T)
        o_c = _attn_call("dil", qtc, kc, vtc, (bias,), T, n_back=n_back)
        attn = ([o_a.reshape(b * s, -1), o_b.reshape(b * s, -1), o_c.reshape(b * s, -1)], wo_all)
        last = l == depth - 1
        x2d = _ffn_call(x2d, l, *ffn2, attn=attn, g_final=g_final if last else None, tm=tm)
        if not last:
            x2d = _ffn_call(x2d, l + 1, *ffn1, tm=tm)
    return x2d.reshape(b, s, d)


def kernel(x, w_in, b_f, lam_q1, lam_k1, lam_q2, lam_k2, g_sub, w_o, g_ffn1, w1_gate, w1_up, w1_down,
           g_mix, g_ffn2, w2_gate, w2_up, w2_down, g_final):
    return _forward(x, w_in, b_f, lam_q1, lam_k1, lam_q2, lam_k2, g_sub, w_o, g_ffn1, w1_gate, w1_up,
                    w1_down, g_mix, g_ffn2, w2_gate, w2_up, w2_down, g_final)
```

```python
import functools
import math

import numpy as np
import jax
import jax.numpy as jnp
from jax import lax
from jax.experimental import pallas as pl
from jax.experimental.pallas import tpu as pltpu

F32 = jnp.float32
BF16 = jnp.bfloat16

HEAD_DIM = 64
N_DIFF_HEADS = 6
N_FOX_HEADS = 6
N_DIL_HEADS = 4
DIFF_QK_DIM = 32
DILATED_PATTERNS = ((128, 1), (512, 4), (2048, 16))
ROPE_THETA = 10000.0
NORM_EPS = 1e-6
SUBLN_EPS = 1e-5
MACARON_SCALE = 0.5
NEG_BIG = -1e30
LOG2E = 1.4426950408889634

LANES = 128
SUBLANES = 8
MXU_WIDTH = 256
BF16_ROWS = 16
V_ROWS = HEAD_DIM + BF16_ROWS
GATE_STRIDE = 8
VMEM_LIMIT_BYTES = 56 * 1024 * 1024
PIPELINE_AHEAD = 2


def _dot(a, b):
    return jnp.dot(a, b, preferred_element_type=F32)


def _dot_nt(a, b):
    return lax.dot_general(a, b, (((1,), (1,)), ((), ())), preferred_element_type=F32)


def _rms(x, g, eps):
    ms = jnp.mean(x * x, axis=-1, keepdims=True)
    return x * lax.rsqrt(ms + eps) * g


def _split3(x):
    hi = x.astype(BF16)
    r1 = x - hi.astype(F32)
    mid = r1.astype(BF16)
    lo = (r1 - mid.astype(F32)).astype(BF16)
    return hi, mid, lo


def _ffn_body(*refs, has_attn, has_final, n_chunks):
    refs = list(refs)
    x_ref = refs.pop(0)
    if has_attn:
        oa_ref, ob_ref, oc_ref, wo_ref = refs[:4]
        refs = refs[4:]
    g_ref, wg_ref, wu_ref, wd_ref = refs[:4]
    refs = refs[4:]
    if has_final:
        gf_ref = refs.pop(0)
    out_ref = refs[0]

    x = x_ref[...]
    if has_attn:
        o = jnp.concatenate([oa_ref[...], ob_ref[...], oc_ref[...]], axis=1)
        x = x + _dot(o, wo_ref[...])
    xn = _rms(x, g_ref[...], NORM_EPS).astype(BF16)
    n_wide = wg_ref.shape[1] // MXU_WIDTH
    bounds = [MXU_WIDTH * ((n_wide * c + n_chunks - 1) // n_chunks) for c in range(n_chunks + 1)]
    acc = None
    for c in range(n_chunks):
        sl = slice(bounds[c], bounds[c + 1])
        gate = _dot(xn, wg_ref[:, sl])
        up = _dot(xn, wu_ref[:, sl])
        h = (gate * jax.nn.sigmoid(gate) * up).astype(BF16)
        d = _dot(h, wd_ref[sl, :])
        acc = d if acc is None else acc + d
    y = x + MACARON_SCALE * acc
    if has_final:
        y = _rms(y, gf_ref[...], NORM_EPS)
    out_ref[...] = y


def _const_spec(shape):
    nd = len(shape)
    return pl.BlockSpec(shape, lambda *_: (0,) * nd, pipeline_mode=pl.Buffered(1))


def _layer_spec(shape, l):
    nd = len(shape)
    return pl.BlockSpec((None,) + tuple(shape[1:]), lambda *_: (l,) + (0,) * (nd - 1),
                        pipeline_mode=pl.Buffered(1))


def _ffn_call(x2d, l, g, wg, wu, wd, attn=None, g_final=None, tm=512, lw=None):
    lw = l if lw is None else lw
    n, d = x2d.shape
    dff = wg.shape[2]
    args = [x2d]
    specs = [pl.BlockSpec((tm, d), lambda i: (i, 0))]
    if attn is not None:
        o_list, wo = attn
        for o in o_list:
            args.append(o)
            specs.append(pl.BlockSpec((tm, o.shape[1]), lambda i: (i, 0)))
        args.append(wo)
        specs.append(_layer_spec(wo.shape, l))
    args += [g, wg, wu, wd]
    specs += [_layer_spec(g.shape, l), _layer_spec(wg.shape, lw), _layer_spec(wu.shape, lw),
              _layer_spec(wd.shape, lw)]
    if g_final is not None:
        args.append(g_final.reshape(1, d))
        specs.append(_const_spec((1, d)))
    assert dff % MXU_WIDTH == 0
    n_chunks = dff // MXU_WIDTH
    body = functools.partial(_ffn_body, has_attn=attn is not None, has_final=g_final is not None,
                             n_chunks=n_chunks)
    return pl.pallas_call(
        body,
        grid=(n // tm,),
        in_specs=specs,
        out_specs=pl.BlockSpec((tm, d), lambda i: (i, 0)),
        out_shape=jax.ShapeDtypeStruct((n, d), F32),
        compiler_params=pltpu.CompilerParams(dimension_semantics=("arbitrary",),
                                             vmem_limit_bytes=VMEM_LIMIT_BYTES),
        name="ffn",
    )(*args)


W_DIFF = N_DIFF_HEADS * HEAD_DIM
W_FOX = N_FOX_HEADS * HEAD_DIM
W_DIL = N_DIL_HEADS * HEAD_DIM
P_DIFF, P_FOX, P_DIL = W_DIFF // LANES, W_FOX // LANES, W_DIL // LANES
KN_FOX, KN_DIL, N_NAT = W_DIFF, W_DIFF + W_FOX, W_DIFF + W_FOX + W_DIL
QT_FOX, QT_DIL, VT_DIFF = W_DIFF, W_DIFF + W_FOX, N_NAT
VT_FOX, VT_DIL, N_TR = N_NAT + W_DIFF, N_NAT + W_DIFF + W_FOX, 2 * N_NAT
GATE_ROWS = 32


def _proj_body(x_ref, g_ref, wn_ref, wt_ref, bf_ref, tri_ref,
               cnd_ref, snd_ref, cnc_ref, snc_ref, ctd_ref, std_ref, ctc_ref, stc_ref,
               qtd_ref, kd_ref, vtd_ref, qtf_ref, kf_ref, vtf_ref, cq_ref, qtc_ref, kc_ref, vtc_ref,
               carry_ref):
    tm = x_ref.shape[0]

    @pl.when(pl.program_id(1) == 0)
    def _():
        carry_ref[...] = jnp.zeros_like(carry_ref)

    xn = _rms(x_ref[...], g_ref[...], NORM_EPS).astype(BF16)
    rn = _dot(xn, wn_ref[...])
    rt_all = _dot_nt(wt_ref[...], xn)
    gate_t = rt_all[0:GATE_ROWS]
    rt = rt_all[GATE_ROWS:]

    lane = lax.broadcasted_iota(jnp.int32, (tm, LANES), 1)

    def rope_nat(blk, half, cos, sin_signed):
        first = (lane & half) == 0
        rot = jnp.where(first, pltpu.roll(blk, LANES - half, 1), pltpu.roll(blk, half, 1))
        return (blk * cos + rot * sin_signed).astype(BF16)

    for p in range(P_DIFF):
        kd_ref[p] = rope_nat(rn[:, LANES * p:LANES * (p + 1)], DIFF_QK_DIM // 2, cnd_ref[...], snd_ref[...])
    for p in range(P_DIL):
        c0 = KN_DIL + LANES * p
        kc_ref[p] = rope_nat(rn[:, c0:c0 + LANES], HEAD_DIM // 2, cnc_ref[...], snc_ref[...])

    def rope_tr(ref, p, r0, x1, x2, cos, sin, scale):
        half = x1.shape[0]
        ref[p, r0:r0 + half, :] = ((x1 * cos - x2 * sin) * scale).astype(BF16)
        ref[p, r0 + half:r0 + 2 * half, :] = ((x2 * cos + x1 * sin) * scale).astype(BF16)

    sd = DIFF_QK_DIM ** -0.5 * LOG2E
    hd = DIFF_QK_DIM // 2
    for g in range(2 * N_DIFF_HEADS):
        b0 = DIFF_QK_DIM * g
        rope_tr(qtd_ref, g // 4, DIFF_QK_DIM * (g % 4), rt[b0:b0 + hd], rt[b0 + hd:b0 + 2 * hd],
                ctd_ref[...], std_ref[...], sd)
    sf = HEAD_DIM ** -0.5 * LOG2E
    for p in range(P_FOX):
        b0 = QT_FOX + LANES * p
        qtf_ref[p] = (rt[b0:b0 + LANES] * sf).astype(BF16)
    hc = HEAD_DIM // 2
    for h in range(N_DIL_HEADS):
        b0 = QT_DIL + HEAD_DIM * h
        rope_tr(qtc_ref, h // 2, HEAD_DIM * (h % 2), rt[b0:b0 + hc], rt[b0 + hc:b0 + 2 * hc],
                ctc_ref[...], stc_ref[...], sf)

    ones_rows = jnp.ones((BF16_ROWS, tm), BF16)

    def put_vt(ref, base, n_pairs):
        for p in range(n_pairs):
            for h in range(2):
                b0 = base + LANES * p + HEAD_DIM * h
                ref[p, V_ROWS * h:V_ROWS * h + HEAD_DIM, :] = rt[b0:b0 + HEAD_DIM].astype(BF16)
                ref[p, V_ROWS * h + HEAD_DIM:V_ROWS * (h + 1), :] = ones_rows

    put_vt(vtd_ref, VT_DIFF, P_DIFF)
    put_vt(vtf_ref, VT_FOX, P_FOX)
    put_vt(vtc_ref, VT_DIL, P_DIL)

    z = gate_t + bf_ref[...]
    logf = (jnp.minimum(z, 0.0) - jnp.log1p(jnp.exp(-jnp.abs(z)))) * LOG2E
    part = _dot(jnp.concatenate(_split3(logf), axis=0), tri_ref[...])
    c = part[0:GATE_ROWS] + part[GATE_ROWS:2 * GATE_ROWS] + part[2 * GATE_ROWS:3 * GATE_ROWS] + carry_ref[...]
    carry_ref[...] = c[:, tm - 1:tm]
    cq_ref[...] = c[0:P_FOX * GATE_STRIDE]
    row = lax.broadcasted_iota(jnp.int32, (LANES, tm), 0)
    ones_pattern = jnp.where((row < 3 * BF16_ROWS) & ((row & (BF16_ROWS - 1)) == 0), 1.0, 0.0)
    for p in range(P_FOX):
        kf_ref[p, :, 0:LANES] = rn[:, KN_FOX + LANES * p:KN_FOX + LANES * (p + 1)].astype(BF16)
        auxt = ones_pattern
        for h in range(2):
            ch = c[GATE_STRIDE * p + h:GATE_STRIDE * p + h + 1]
            for t, term in enumerate(_split3(ch)):
                auxt = jnp.where(row == (3 + h) * BF16_ROWS + t, -term.astype(F32), auxt)
        kf_ref[p, :, LANES:2 * LANES] = auxt.T.astype(BF16)


def _proj_call(x, l, g_mix, wn, wt, bf_pad, consts, T):
    b, s, d = x.shape
    nt = s // T
    tile5 = lambda npair, r, c: pl.BlockSpec((None, npair, None, r, c), lambda bi, ti: (bi, 0, ti, 0, 0))
    nat_tab = pl.BlockSpec((T, LANES), lambda bi, ti: (ti, 0))
    tr_tab = lambda r: pl.BlockSpec((r, T), lambda bi, ti: (0, ti))
    in_specs = [
        pl.BlockSpec((None, T, d), lambda bi, ti: (bi, ti, 0)),
        _layer_spec(g_mix.shape, l), _layer_spec(wn.shape, l), _layer_spec(wt.shape, l),
        _layer_spec(bf_pad.shape, l), _const_spec((T, T)),
        nat_tab, nat_tab, nat_tab, nat_tab,
        tr_tab(DIFF_QK_DIM // 2), tr_tab(DIFF_QK_DIM // 2), tr_tab(HEAD_DIM // 2), tr_tab(HEAD_DIM // 2),
    ]
    out_shape = [
        jax.ShapeDtypeStruct((b, P_DIFF, nt, LANES, T), BF16),
        jax.ShapeDtypeStruct((b, P_DIFF, nt, T, LANES), BF16),
        jax.ShapeDtypeStruct((b, P_DIFF, nt, 2 * V_ROWS, T), BF16),
        jax.ShapeDtypeStruct((b, P_FOX, nt, LANES, T), BF16),
        jax.ShapeDtypeStruct((b, P_FOX, nt, T, 2 * LANES), BF16),
        jax.ShapeDtypeStruct((b, P_FOX, nt, 2 * V_ROWS, T), BF16),
        jax.ShapeDtypeStruct((b, nt, P_FOX * GATE_STRIDE, T), F32),
        jax.ShapeDtypeStruct((b, P_DIL, nt, LANES, T), BF16),
        jax.ShapeDtypeStruct((b, P_DIL, nt, T, LANES), BF16),
        jax.ShapeDtypeStruct((b, P_DIL, nt, 2 * V_ROWS, T), BF16),
    ]
    out_specs = [
        tile5(P_DIFF, LANES, T), tile5(P_DIFF, T, LANES), tile5(P_DIFF, 2 * V_ROWS, T),
        tile5(P_FOX, LANES, T), tile5(P_FOX, T, 2 * LANES), tile5(P_FOX, 2 * V_ROWS, T),
        pl.BlockSpec((None, None, P_FOX * GATE_STRIDE, T), lambda bi, ti: (bi, ti, 0, 0)),
        tile5(P_DIL, LANES, T), tile5(P_DIL, T, LANES), tile5(P_DIL, 2 * V_ROWS, T),
    ]
    return pl.pallas_call(
        _proj_body,
        grid=(b, nt),
        in_specs=in_specs,
        out_specs=out_specs,
        out_shape=out_shape,
        scratch_shapes=[pltpu.VMEM((GATE_ROWS, 1), F32)],
        compiler_params=pltpu.CompilerParams(dimension_semantics=("arbitrary", "arbitrary"),
                                             vmem_limit_bytes=VMEM_LIMIT_BYTES),
        name="proj",
    )(x, g_mix, wn, wt, bf_pad, consts["tri"], consts["cnd"], consts["snd"], consts["cnc"], consts["snc"],
      consts["ctd"], consts["std"], consts["ctc"], consts["stc"])


def _attn_body(*refs, mode, n_back, n_cast):
    refs = list(refs)
    n_extra = {"fox": 1, "diff": 2, "dil": 1}[mode]
    qt_ref, k_ref, vt_ref = refs[:3]
    extra_refs = refs[3:3 + n_extra]
    cast_in = refs[3 + n_extra:3 + n_extra + n_cast]
    o_ref = refs[3 + n_extra + n_cast]
    cast_out = refs[4 + n_extra + n_cast:4 + n_extra + 2 * n_cast]
    qe_ref, m_ref, acc_ref, s_ref, tmax_ref = refs[4 + n_extra + 2 * n_cast:]
    if mode == "fox":
        (cq_ref,) = extra_refs
    elif mode == "diff":
        lam_ref, gcol_ref = extra_refs
    else:
        (bias_ref,) = extra_refs
    for w_in_ref, w_out_ref in zip(cast_in, cast_out):
        w_out_ref[...] = w_in_ref[...].astype(BF16)
    n_pairs, nt, _, T = qt_ref.shape
    n_sc = s_ref.shape[0]
    per_pair = n_sc // n_pairs
    rows = LANES // per_pair
    q_tiles = (pl.program_id(2), nt - 1 - pl.program_id(2))

    def first_tile(i):
        return jnp.maximum(i - n_back, 0) if mode == "dil" else 0

    def setup(st, i):
        for sc in range(n_sc):
            p, r = divmod(sc, per_pair)
            qe_ref[st, sc, rows * r:rows * (r + 1), :] = qt_ref[p, i, rows * r:rows * (r + 1), :]
        if mode == "fox":
            for sc in range(n_sc):
                p, h = divmod(sc, 2)
                terms = _split3(cq_ref[i, p, h:h + 1, :])
                blocks = [jnp.broadcast_to(t.astype(F32), (BF16_ROWS, T)) for t in terms]
                blocks += [jnp.full((BF16_ROWS, T), 1.0 if h == hh else 0.0, F32) for hh in range(2)]
                blocks += [jnp.zeros((LANES - 5 * BF16_ROWS, T), F32)]
                qe_ref[st, sc, LANES:2 * LANES, :] = jnp.concatenate(blocks, axis=0).astype(BF16)
        m_ref[st] = jnp.full(m_ref.shape[1:], NEG_BIG, F32)
        acc_ref[st] = jnp.zeros(acc_ref.shape[1:], F32)

    half = T // 2
    full_blocks = ((slice(0, T), slice(0, T)),)
    diag_blocks = ((slice(0, half), slice(0, half)), (slice(0, T), slice(half, T)))

    def produce(st, sc, j, diag=False):
        for rk, cq in (diag_blocks if diag else full_blocks):
            s = _dot(k_ref[sc // per_pair, j, rk, :], qe_ref[st, sc, :, cq])
            if mode == "dil":
                s = s + bias_ref[q_tiles[st] - j, rk, cq]
            s_ref[sc, rk, cq] = s
            if mode == "dil" or not diag:
                tmax_ref[sc, :, cq] = jnp.broadcast_to(jnp.max(s, axis=0, keepdims=True),
                                                       (SUBLANES, s.shape[1]))

    def consume(st, sc, j, diag):
        p, r = divmod(sc, per_pair)
        h = r * 2 // per_pair
        for rk, cq in (diag_blocks if diag else full_blocks):
            s = s_ref[sc, rk, cq]
            if diag and mode != "dil":
                kk = lax.broadcasted_iota(jnp.int32, s.shape, 0)
                qq = lax.broadcasted_iota(jnp.int32, s.shape, 1) + cq.start
                s = jnp.where(kk <= qq, s, NEG_BIG)
                tmax = jnp.max(s, axis=0, keepdims=True)
            else:
                tmax = tmax_ref[sc, 0:1, cq]
            m_old = m_ref[st, sc, 0:1, cq]
            m_new = jnp.maximum(m_old, tmax)
            alpha = jnp.exp2(m_old - m_new)
            pr = jnp.exp2(s - m_new).astype(BF16)
            pv = _dot(vt_ref[p, j, V_ROWS * h:V_ROWS * (h + 1), rk], pr)
            acc_ref[st, sc, :, cq] = acc_ref[st, sc, :, cq] * alpha + pv
            m_ref[st, sc, :, cq] = jnp.broadcast_to(m_new, (SUBLANES, s.shape[1]))

    ahead = min(PIPELINE_AHEAD, n_sc - 1)

    def run_tile(st, j, diag):
        for sc in range(n_sc):
            if sc + ahead < n_sc:
                produce(st, sc + ahead, j, diag)
            elif not diag:
                produce(st, sc + ahead - n_sc, j + 1)
            elif st == 0:
                produce(1, sc + ahead - n_sc, first_tile(q_tiles[1]))
            consume(st, sc, j, diag)

    def finalize(st, i):
        def normalised(sc):
            a = acc_ref[st, sc]
            return a[0:HEAD_DIM] / a[HEAD_DIM:HEAD_DIM + 1]

        if mode == "diff":
            lam_init = lam_ref[4:5, 0:1]
            lam = (jnp.exp(jnp.sum(lam_ref[0:1] * lam_ref[1:2], axis=1, keepdims=True))
                   - jnp.exp(jnp.sum(lam_ref[2:3] * lam_ref[3:4], axis=1, keepdims=True)) + lam_init)
        for p in range(n_pairs):
            outs = []
            for h in range(2):
                if mode == "diff":
                    sc = per_pair * p + 2 * h
                    d = normalised(sc) - lam * normalised(sc + 1)
                    ms = jnp.mean(d * d, axis=0, keepdims=True)
                    outs.append(d * lax.rsqrt(ms + SUBLN_EPS) * gcol_ref[...] * (1.0 - lam_init))
                else:
                    outs.append(normalised(per_pair * p + h))
            o_ref[pl.ds(pl.multiple_of(i * T, T), T), LANES * p:LANES * (p + 1)] = (
                jnp.concatenate(outs, axis=0).T.astype(BF16))

    @pl.when((pl.program_id(0) == 0) & (pl.program_id(1) == 0) & (pl.program_id(2) == 0))
    def _():
        qe_ref[...] = jnp.zeros(qe_ref.shape, BF16)

    for st in range(2):
        setup(st, q_tiles[st])
    for sc in range(ahead):
        produce(0, sc, first_tile(q_tiles[0]))
    for st in range(2):
        i = q_tiles[st]

        def step(j, carry, st=st):
            run_tile(st, j, False)
            return carry

        lax.fori_loop(first_tile(i), i, step, 0)
        run_tile(st, i, True)
    for st in range(2):
        finalize(st, q_tiles[st])


def _attn_call(mode, qt, k, vt, extra, T, n_back=0, pairs_per_step=None, layer=0, cast=()):
    b, npair, nt = qt.shape[:3]
    assert nt % 2 == 0
    pps = npair if pairs_per_step is None else pairs_per_step
    r = k.shape[-1]
    n_sc = (4 if mode == "diff" else 2) * pps
    in_specs = [
        pl.BlockSpec((None, pps, nt, LANES, T), lambda bi, p, i: (bi, p, 0, 0, 0)),
        pl.BlockSpec((None, pps, nt, T, r), lambda bi, p, i: (bi, p, 0, 0, 0)),
        pl.BlockSpec((None, pps, nt, 2 * V_ROWS, T), lambda bi, p, i: (bi, p, 0, 0, 0)),
    ]
    if mode == "fox":
        in_specs.append(pl.BlockSpec((None, nt, pps, GATE_STRIDE, T), lambda bi, p, i: (bi, 0, p, 0, 0)))
    elif mode == "diff":
        in_specs += [_layer_spec(extra[0].shape, layer), _layer_spec(extra[1].shape, layer)]
    else:
        in_specs.append(_const_spec(extra[0].shape))
    grid = (b, npair // pps, nt // 2)
    n_steps = grid[0] * grid[1] * grid[2]
    out_specs = [pl.BlockSpec((None, nt * T, pps * LANES), lambda bi, p, i: (bi, 0, p))]
    out_shape = [jax.ShapeDtypeStruct((b, nt * T, npair * LANES), BF16)]
    step_of = lambda bi, p, i: (bi * grid[1] + p) * grid[2] + i
    for w, lw in cast:
        _, rows_w, cols_w = w.shape
        assert rows_w % (n_steps * BF16_ROWS) == 0
        in_specs.append(pl.BlockSpec((None, rows_w // n_steps, cols_w),
                                     lambda bi, p, i, lw=lw: (lw, step_of(bi, p, i), 0)))
        out_specs.append(pl.BlockSpec((rows_w // n_steps, cols_w), lambda bi, p, i: (step_of(bi, p, i), 0)))
        out_shape.append(jax.ShapeDtypeStruct((rows_w, cols_w), BF16))
    body = functools.partial(_attn_body, mode=mode, n_back=n_back, n_cast=len(cast))
    return pl.pallas_call(
        body,
        grid=grid,
        in_specs=in_specs,
        out_specs=out_specs,
        out_shape=out_shape,
        scratch_shapes=[pltpu.VMEM((2, n_sc, r, T), BF16),
                        pltpu.VMEM((2, n_sc, SUBLANES, T), F32),
                        pltpu.VMEM((2, n_sc, V_ROWS, T), F32),
                        pltpu.VMEM((n_sc, T, T), F32),
                        pltpu.VMEM((n_sc, SUBLANES, T), F32)],
        compiler_params=pltpu.CompilerParams(dimension_semantics=("arbitrary", "arbitrary", "arbitrary"),
                                             vmem_limit_bytes=VMEM_LIMIT_BYTES),
        name="attn_" + mode,
    )(qt, k, vt, *extra, *[w for w, _ in cast])


def _constants(s, T):
    pos = np.arange(s, dtype=np.float64)[:, None]

    def angles(half):
        inv = ROPE_THETA ** (-np.arange(half, dtype=np.float64) / half)
        return pos * inv[None, :]

    ad, ac = angles(DIFF_QK_DIM // 2), angles(HEAD_DIM // 2)

    def nat(a):
        reps = LANES // (2 * a.shape[1])
        cos = np.tile(np.concatenate([np.cos(a), np.cos(a)], axis=1), (1, reps))
        sin = np.tile(np.concatenate([-np.sin(a), np.sin(a)], axis=1), (1, reps))
        return cos.astype(np.float32), sin.astype(np.float32)

    cnd, snd = nat(ad)
    cnc, snc = nat(ac)
    consts = dict(cnd=cnd, snd=snd, cnc=cnc, snc=snc,
                  ctd=np.cos(ad).T.astype(np.float32), std=np.sin(ad).T.astype(np.float32),
                  ctc=np.cos(ac).T.astype(np.float32), stc=np.sin(ac).T.astype(np.float32))
    consts["tri"] = np.triu(np.ones((T, T), np.float32))
    n_back = -(-max(w for w, _ in DILATED_PATTERNS) // T)
    kk = np.arange(T)[:, None]
    qq = np.arange(T)[None, :]
    bias = np.zeros((n_back + 1, T, T), np.float32)
    for o in range(n_back + 1):
        delta = o * T + qq - kk
        mult = np.zeros((T, T), np.float64)
        for w, dil in DILATED_PATTERNS:
            mult += (delta >= 0) & (delta <= w) & (delta % dil == 0)
        bias[o] = np.where(mult > 0, np.log2(np.maximum(mult, 1.0)), NEG_BIG)
    out = {}
    for name, v in consts.items():
        out[name] = jnp.asarray(v, BF16 if name == "tri" else F32)
    return out, jnp.asarray(bias), n_back


def _proj_weights(w_in, b_f):
    depth, d, _ = w_in.shape
    w = w_in.astype(BF16)
    sizes = [W_DIFF] * 3 + [W_FOX] * 3 + [N_FOX_HEADS] + [W_DIL] * 3
    aq, ak, av, fq, fk, fv, fg, cq, ck, cv = jnp.split(w, np.cumsum(sizes)[:-1].tolist(), axis=2)
    rows = np.array([GATE_STRIDE * (h // 2) + h % 2 for h in range(N_FOX_HEADS)])
    fg_cols = jnp.zeros((depth, d, GATE_ROWS), BF16).at[:, :, rows].set(fg)
    bf_pad = jnp.zeros((depth, GATE_ROWS, 1), F32).at[:, rows, 0].set(b_f)
    wn = jnp.concatenate([ak, fk, ck], axis=2)
    wt = jnp.concatenate([fg_cols, aq, fq, cq, av, fv, cv], axis=2).transpose(0, 2, 1)
    return wn, wt, bf_pad


def _forward(x, w_in, b_f, lam_q1, lam_k1, lam_q2, lam_k2, g_sub, w_o, g_ffn1, w1_gate, w1_up, w1_down,
             g_mix, g_ffn2, w2_gate, w2_up, w2_down, g_final, T=512, tm=512):
    b, s, d = x.shape
    depth = w_in.shape[0]
    assert s % T == 0 and (b * s) % tm == 0 and d == HEAD_DIM * (N_DIFF_HEADS + N_FOX_HEADS + N_DIL_HEADS)
    consts, bias, n_back = _constants(s, T)
    wn_all, wt_all, bf_all = _proj_weights(w_in, b_f)
    dff = w1_gate.shape[2]
    g1, g2 = g_ffn1[:, None, :], g_ffn2[:, None, :]
    w1_f32 = (w1_gate, w1_up, w1_down.reshape(depth, d, dff))
    w2_f32 = (w2_gate, w2_up, w2_down.reshape(depth, d, dff))

    def ffn_weights(gate, up, down_view):
        return gate[None], up[None], down_view.reshape(dff, d)[None]

    ffn1 = tuple(w[0:1].astype(BF16) for w in (w1_gate, w1_up, w1_down))
    g_mix3 = g_mix[:, None, :]
    wo_all = w_o.astype(BF16)
    lam_init = np.array([0.8 - 0.6 * math.exp(-0.3 * l) for l in range(depth)], np.float32)
    lam_all = jnp.zeros((depth, SUBLANES, LANES), F32)
    lam_all = lam_all.at[:, 0:4, 0:DIFF_QK_DIM].set(jnp.stack([lam_q1, lam_k1, lam_q2, lam_k2], axis=1))
    lam_all = lam_all.at[:, 4, :].set(jnp.asarray(lam_init)[:, None])
    x2d = x.reshape(b * s, d)
    for l in range(depth):
        x2d = _ffn_call(x2d, l, g1, *ffn1, tm=tm, lw=0)
        (qtd, kd, vtd, qtf, kf, vtf, cqf, qtc, kc, vtc) = _proj_call(
            x2d.reshape(b, s, d), l, g_mix3, wn_all, wt_all, bf_all, consts, T)
        nxt = l + 1 < depth
        o_a, w2g, w2u = _attn_call("diff", qtd, kd, vtd, (lam_all, g_sub[:, :, None]), T, layer=l,
                                   cast=((w2_f32[0], l), (w2_f32[1], l)))
        o_b, w2d, *w1a = _attn_call("fox", qtf, kf, vtf, (cqf.reshape(b, s // T, P_FOX, GATE_STRIDE, T),), T,
                                    cast=((w2_f32[2], l),) + (((w1_f32[0], l + 1),) if nxt else ()))
        o_c, *w1b = _attn_call("dil", qtc, kc, vtc, (bias,), T, n_back=n_back,
                               cast=((w1_f32[1], l + 1), (w1_f32[2], l + 1)) if nxt else ())
        attn = ([o_a.reshape(b * s, -1), o_b.reshape(b * s, -1), o_c.reshape(b * s, -1)], wo_all)
        x2d = _ffn_call(x2d, l, g2, *ffn_weights(w2g, w2u, w2d), attn=attn,
                        g_final=None if nxt else g_final, tm=tm, lw=0)
        if nxt:
            ffn1 = ffn_weights(w1a[0], w1b[0], w1b[1])
    return x2d.reshape(b, s, d)


def kernel(x, w_in, b_f, lam_q1, lam_k1, lam_q2, lam_k2, g_sub, w_o, g_ffn1, w1_gate, w1_up, w1_down,
           g_mix, g_ffn2, w2_gate, w2_up, w2_down, g_final):
    return _forward(x, w_in, b_f, lam_q1, lam_k1, lam_q2, lam_k2, g_sub, w_o, g_ffn1, w1_gate, w1_up,
                    w1_down, g_mix, g_ffn2, w2_gate, w2_up, w2_down, g_final)
```

```python
import functools
import math

import numpy as np
import jax
import jax.numpy as jnp
from jax import lax
from jax.experimental import pallas as pl
from jax.experimental.pallas import tpu as pltpu

F32 = jnp.float32
BF16 = jnp.bfloat16

HEAD_DIM = 64
N_DIFF_HEADS = 6
N_FOX_HEADS = 6
N_DIL_HEADS = 4
DIFF_QK_DIM = 32
DILATED_PATTERNS = ((128, 1), (512, 4), (2048, 16))
ROPE_THETA = 10000.0
NORM_EPS = 1e-6
SUBLN_EPS = 1e-5
MACARON_SCALE = 0.5
NEG_BIG = -1e30
LOG2E = 1.4426950408889634

LANES = 128
SUBLANES = 8
MXU_WIDTH = 256
BF16_ROWS = 16
V_ROWS = HEAD_DIM + BF16_ROWS
GATE_STRIDE = 8
VMEM_LIMIT_BYTES = 56 * 1024 * 1024
PIPELINE_AHEAD = 2


def _dot(a, b):
    return jnp.dot(a, b, preferred_element_type=F32)


def _dot_nt(a, b):
    return lax.dot_general(a, b, (((1,), (1,)), ((), ())), preferred_element_type=F32)


def _rms(x, g, eps):
    ms = jnp.mean(x * x, axis=-1, keepdims=True)
    return x * lax.rsqrt(ms + eps) * g


def _split3(x):
    hi = x.astype(BF16)
    r1 = x - hi.astype(F32)
    mid = r1.astype(BF16)
    lo = (r1 - mid.astype(F32)).astype(BF16)
    return hi, mid, lo


def _ffn_body(*refs, has_attn, has_final, n_chunks):
    refs = list(refs)
    x_ref = refs.pop(0)
    if has_attn:
        oa_ref, ob_ref, oc_ref, wo_ref = refs[:4]
        refs = refs[4:]
    g_ref, wg_ref, wu_ref, wd_ref = refs[:4]
    refs = refs[4:]
    if has_final:
        gf_ref = refs.pop(0)
    out_ref = refs[0]

    x = x_ref[...]
    if has_attn:
        o = jnp.concatenate([oa_ref[...], ob_ref[...], oc_ref[...]], axis=1)
        x = x + _dot(o, wo_ref[...])
    xn = _rms(x, g_ref[...], NORM_EPS).astype(BF16)
    n_wide = wg_ref.shape[1] // MXU_WIDTH
    bounds = [MXU_WIDTH * ((n_wide * c + n_chunks - 1) // n_chunks) for c in range(n_chunks + 1)]
    acc = None
    for c in range(n_chunks):
        sl = slice(bounds[c], bounds[c + 1])
        gate = _dot(xn, wg_ref[:, sl])
        up = _dot(xn, wu_ref[:, sl])
        h = (gate * jax.nn.sigmoid(gate) * up).astype(BF16)
        d = _dot(h, wd_ref[sl, :])
        acc = d if acc is None else acc + d
    y = x + MACARON_SCALE * acc
    if has_final:
        y = _rms(y, gf_ref[...], NORM_EPS)
    out_ref[...] = y


def _const_spec(shape):
    nd = len(shape)
    return pl.BlockSpec(shape, lambda *_: (0,) * nd, pipeline_mode=pl.Buffered(1))


def _layer_spec(shape, l):
    nd = len(shape)
    return pl.BlockSpec((None,) + tuple(shape[1:]), lambda *_: (l,) + (0,) * (nd - 1),
                        pipeline_mode=pl.Buffered(1))


def _ffn_call(x2d, l, g, wg, wu, wd, attn=None, g_final=None, tm=512):
    n, d = x2d.shape
    dff = wg.shape[2]
    args = [x2d]
    specs = [pl.BlockSpec((tm, d), lambda i: (i, 0))]
    if attn is not None:
        o_list, wo = attn
        for o in o_list:
            args.append(o)
            specs.append(pl.BlockSpec((tm, o.shape[1]), lambda i: (i, 0)))
        args.append(wo)
        specs.append(_layer_spec(wo.shape, l))
    args += [g, wg, wu, wd]
    specs += [_layer_spec(g.shape, l), _layer_spec(wg.shape, l), _layer_spec(wu.shape, l),
              _layer_spec(wd.shape, l)]
    if g_final is not None:
        args.append(g_final.reshape(1, d))
        specs.append(_const_spec((1, d)))
    assert dff % MXU_WIDTH == 0
    n_chunks = dff // MXU_WIDTH
    body = functools.partial(_ffn_body, has_attn=attn is not None, has_final=g_final is not None,
                             n_chunks=n_chunks)
    return pl.pallas_call(
        body,
        grid=(n // tm,),
        in_specs=specs,
        out_specs=pl.BlockSpec((tm, d), lambda i: (i, 0)),
        out_shape=jax.ShapeDtypeStruct((n, d), F32),
        compiler_params=pltpu.CompilerParams(dimension_semantics=("arbitrary",),
                                             vmem_limit_bytes=VMEM_LIMIT_BYTES),
        name="ffn",
    )(*args)


W_DIFF = N_DIFF_HEADS * HEAD_DIM
W_FOX = N_FOX_HEADS * HEAD_DIM
W_DIL = N_DIL_HEADS * HEAD_DIM
P_DIFF, P_FOX, P_DIL = W_DIFF // LANES, W_FOX // LANES, W_DIL // LANES
KN_FOX, KN_DIL, N_NAT = W_DIFF, W_DIFF + W_FOX, W_DIFF + W_FOX + W_DIL
QT_FOX, QT_DIL, VT_DIFF = W_DIFF, W_DIFF + W_FOX, N_NAT
VT_FOX, VT_DIL, N_TR = N_NAT + W_DIFF, N_NAT + W_DIFF + W_FOX, 2 * N_NAT
GATE_ROWS = 32


def _proj_body(x_ref, g_ref, wn_ref, wt_ref, bf_ref, tri_ref,
               cnd_ref, snd_ref, cnc_ref, snc_ref, ctd_ref, std_ref, ctc_ref, stc_ref,
               qtd_ref, kd_ref, vtd_ref, qtf_ref, kf_ref, vtf_ref, cq_ref, qtc_ref, kc_ref, vtc_ref,
               carry_ref):
    tm = x_ref.shape[0]

    @pl.when(pl.program_id(1) == 0)
    def _():
        carry_ref[...] = jnp.zeros_like(carry_ref)

    xn = _rms(x_ref[...], g_ref[...], NORM_EPS).astype(BF16)
    rn = _dot(xn, wn_ref[...])
    rt_all = _dot_nt(wt_ref[...], xn)
    gate_t = rt_all[0:GATE_ROWS]
    rt = rt_all[GATE_ROWS:]

    lane = lax.broadcasted_iota(jnp.int32, (tm, LANES), 1)

    def rope_nat(blk, half, cos, sin_signed):
        first = (lane & half) == 0
        rot = jnp.where(first, pltpu.roll(blk, LANES - half, 1), pltpu.roll(blk, half, 1))
        return (blk * cos + rot * sin_signed).astype(BF16)

    for p in range(P_DIFF):
        kd_ref[p] = rope_nat(rn[:, LANES * p:LANES * (p + 1)], DIFF_QK_DIM // 2, cnd_ref[...], snd_ref[...])
    for p in range(P_DIL):
        c0 = KN_DIL + LANES * p
        kc_ref[p] = rope_nat(rn[:, c0:c0 + LANES], HEAD_DIM // 2, cnc_ref[...], snc_ref[...])

    def rope_tr(ref, p, r0, x1, x2, cos, sin, scale):
        half = x1.shape[0]
        ref[p, r0:r0 + half, :] = ((x1 * cos - x2 * sin) * scale).astype(BF16)
        ref[p, r0 + half:r0 + 2 * half, :] = ((x2 * cos + x1 * sin) * scale).astype(BF16)

    sd = DIFF_QK_DIM ** -0.5 * LOG2E
    hd = DIFF_QK_DIM // 2
    for g in range(2 * N_DIFF_HEADS):
        b0 = DIFF_QK_DIM * g
        rope_tr(qtd_ref, g // 4, DIFF_QK_DIM * (g % 4), rt[b0:b0 + hd], rt[b0 + hd:b0 + 2 * hd],
                ctd_ref[...], std_ref[...], sd)
    sf = HEAD_DIM ** -0.5 * LOG2E
    for p in range(P_FOX):
        b0 = QT_FOX + LANES * p
        qtf_ref[p] = (rt[b0:b0 + LANES] * sf).astype(BF16)
    hc = HEAD_DIM // 2
    for h in range(N_DIL_HEADS):
        b0 = QT_DIL + HEAD_DIM * h
        rope_tr(qtc_ref, h // 2, HEAD_DIM * (h % 2), rt[b0:b0 + hc], rt[b0 + hc:b0 + 2 * hc],
                ctc_ref[...], stc_ref[...], sf)

    ones_rows = jnp.ones((BF16_ROWS, tm), BF16)

    def put_vt(ref, base, n_pairs):
        for p in range(n_pairs):
            for h in range(2):
                b0 = base + LANES * p + HEAD_DIM * h
                ref[p, V_ROWS * h:V_ROWS * h + HEAD_DIM, :] = rt[b0:b0 + HEAD_DIM].astype(BF16)
                ref[p, V_ROWS * h + HEAD_DIM:V_ROWS * (h + 1), :] = ones_rows

    put_vt(vtd_ref, VT_DIFF, P_DIFF)
    put_vt(vtf_ref, VT_FOX, P_FOX)
    put_vt(vtc_ref, VT_DIL, P_DIL)

    z = gate_t + bf_ref[...]
    logf = (jnp.minimum(z, 0.0) - jnp.log1p(jnp.exp(-jnp.abs(z)))) * LOG2E
    part = _dot(jnp.concatenate(_split3(logf), axis=0), tri_ref[...])
    c = part[0:GATE_ROWS] + part[GATE_ROWS:2 * GATE_ROWS] + part[2 * GATE_ROWS:3 * GATE_ROWS] + carry_ref[...]
    carry_ref[...] = c[:, tm - 1:tm]
    cq_ref[...] = c[0:P_FOX * GATE_STRIDE]
    row = lax.broadcasted_iota(jnp.int32, (LANES, tm), 0)
    ones_pattern = jnp.where((row < 3 * BF16_ROWS) & ((row & (BF16_ROWS - 1)) == 0), 1.0, 0.0)
    for p in range(P_FOX):
        kf_ref[p, :, 0:LANES] = rn[:, KN_FOX + LANES * p:KN_FOX + LANES * (p + 1)].astype(BF16)
        auxt = ones_pattern
        for h in range(2):
            ch = c[GATE_STRIDE * p + h:GATE_STRIDE * p + h + 1]
            for t, term in enumerate(_split3(ch)):
                auxt = jnp.where(row == (3 + h) * BF16_ROWS + t, -term.astype(F32), auxt)
        kf_ref[p, :, LANES:2 * LANES] = auxt.T.astype(BF16)


def _proj_call(x, l, g_mix, wn, wt, bf_pad, consts, T):
    b, s, d = x.shape
    nt = s // T
    tile5 = lambda npair, r, c: pl.BlockSpec((None, npair, None, r, c), lambda bi, ti: (bi, 0, ti, 0, 0))
    nat_tab = pl.BlockSpec((T, LANES), lambda bi, ti: (ti, 0))
    tr_tab = lambda r: pl.BlockSpec((r, T), lambda bi, ti: (0, ti))
    in_specs = [
        pl.BlockSpec((None, T, d), lambda bi, ti: (bi, ti, 0)),
        _layer_spec(g_mix.shape, l), _layer_spec(wn.shape, l), _layer_spec(wt.shape, l),
        _layer_spec(bf_pad.shape, l), _const_spec((T, T)),
        nat_tab, nat_tab, nat_tab, nat_tab,
        tr_tab(DIFF_QK_DIM // 2), tr_tab(DIFF_QK_DIM // 2), tr_tab(HEAD_DIM // 2), tr_tab(HEAD_DIM // 2),
    ]
    out_shape = [
        jax.ShapeDtypeStruct((b, P_DIFF, nt, LANES, T), BF16),
        jax.ShapeDtypeStruct((b, P_DIFF, nt, T, LANES), BF16),
        jax.ShapeDtypeStruct((b, P_DIFF, nt, 2 * V_ROWS, T), BF16),
        jax.ShapeDtypeStruct((b, P_FOX, nt, LANES, T), BF16),
        jax.ShapeDtypeStruct((b, P_FOX, nt, T, 2 * LANES), BF16),
        jax.ShapeDtypeStruct((b, P_FOX, nt, 2 * V_ROWS, T), BF16),
        jax.ShapeDtypeStruct((b, nt, P_FOX * GATE_STRIDE, T), F32),
        jax.ShapeDtypeStruct((b, P_DIL, nt, LANES, T), BF16),
        jax.ShapeDtypeStruct((b, P_DIL, nt, T, LANES), BF16),
        jax.ShapeDtypeStruct((b, P_DIL, nt, 2 * V_ROWS, T), BF16),
    ]
    out_specs = [
        tile5(P_DIFF, LANES, T), tile5(P_DIFF, T, LANES), tile5(P_DIFF, 2 * V_ROWS, T),
        tile5(P_FOX, LANES, T), tile5(P_FOX, T, 2 * LANES), tile5(P_FOX, 2 * V_ROWS, T),
        pl.BlockSpec((None, None, P_FOX * GATE_STRIDE, T), lambda bi, ti: (bi, ti, 0, 0)),
        tile5(P_DIL, LANES, T), tile5(P_DIL, T, LANES), tile5(P_DIL, 2 * V_ROWS, T),
    ]
    return pl.pallas_call(
        _proj_body,
        grid=(b, nt),
        in_specs=in_specs,
        out_specs=out_specs,
        out_shape=out_shape,
        scratch_shapes=[pltpu.VMEM((GATE_ROWS, 1), F32)],
        compiler_params=pltpu.CompilerParams(dimension_semantics=("arbitrary", "arbitrary"),
                                             vmem_limit_bytes=VMEM_LIMIT_BYTES),
        name="proj",
    )(x, g_mix, wn, wt, bf_pad, consts["tri"], consts["cnd"], consts["snd"], consts["cnc"], consts["snc"],
      consts["ctd"], consts["std"], consts["ctc"], consts["stc"])


def _attn_body(*refs, mode, n_back):
    if mode == "fox":
        qt_ref, k_ref, vt_ref, cq_ref, o_ref, qe_ref, m_ref, acc_ref, s_ref, tmax_ref = refs
    elif mode == "diff":
        qt_ref, k_ref, vt_ref, lam_ref, gcol_ref, o_ref, qe_ref, m_ref, acc_ref, s_ref, tmax_ref = refs
    else:
        qt_ref, k_ref, vt_ref, bias_ref, o_ref, qe_ref, m_ref, acc_ref, s_ref, tmax_ref = refs
    n_pairs, nt, _, T = qt_ref.shape
    n_sc = s_ref.shape[0]
    per_pair = n_sc // n_pairs
    rows = LANES // per_pair
    q_tiles = (pl.program_id(2), nt - 1 - pl.program_id(2))

    def first_tile(i):
        return jnp.maximum(i - n_back, 0) if mode == "dil" else 0

    def setup(st, i):
        for sc in range(n_sc):
            p, r = divmod(sc, per_pair)
            qe_ref[st, sc, rows * r:rows * (r + 1), :] = qt_ref[p, i, rows * r:rows * (r + 1), :]
        if mode == "fox":
            for sc in range(n_sc):
                p, h = divmod(sc, 2)
                terms = _split3(cq_ref[i, p, h:h + 1, :])
                blocks = [jnp.broadcast_to(t.astype(F32), (BF16_ROWS, T)) for t in terms]
                blocks += [jnp.full((BF16_ROWS, T), 1.0 if h == hh else 0.0, F32) for hh in range(2)]
                blocks += [jnp.zeros((LANES - 5 * BF16_ROWS, T), F32)]
                qe_ref[st, sc, LANES:2 * LANES, :] = jnp.concatenate(blocks, axis=0).astype(BF16)
        m_ref[st] = jnp.full(m_ref.shape[1:], NEG_BIG, F32)
        acc_ref[st] = jnp.zeros(acc_ref.shape[1:], F32)

    half = T // 2
    full_blocks = ((slice(0, T), slice(0, T)),)
    diag_blocks = ((slice(0, half), slice(0, half)), (slice(0, T), slice(half, T)))

    def produce(st, sc, j, diag=False):
        for rk, cq in (diag_blocks if diag else full_blocks):
            s = _dot(k_ref[sc // per_pair, j, rk, :], qe_ref[st, sc, :, cq])
            if mode == "dil":
                s = s + bias_ref[q_tiles[st] - j, rk, cq]
            s_ref[sc, rk, cq] = s
            if mode == "dil" or not diag:
                tmax_ref[sc, :, cq] = jnp.broadcast_to(jnp.max(s, axis=0, keepdims=True),
                                                       (SUBLANES, s.shape[1]))

    def consume(st, sc, j, diag):
        p, r = divmod(sc, per_pair)
        h = r * 2 // per_pair
        for rk, cq in (diag_blocks if diag else full_blocks):
            s = s_ref[sc, rk, cq]
            if diag and mode != "dil":
                kk = lax.broadcasted_iota(jnp.int32, s.shape, 0)
                qq = lax.broadcasted_iota(jnp.int32, s.shape, 1) + cq.start
                s = jnp.where(kk <= qq, s, NEG_BIG)
                tmax = jnp.max(s, axis=0, keepdims=True)
            else:
                tmax = tmax_ref[sc, 0:1, cq]
            m_old = m_ref[st, sc, 0:1, cq]
            m_new = jnp.maximum(m_old, tmax)
            alpha = jnp.exp2(m_old - m_new)
            pr = jnp.exp2(s - m_new).astype(BF16)
            pv = _dot(vt_ref[p, j, V_ROWS * h:V_ROWS * (h + 1), rk], pr)
            acc_ref[st, sc, :, cq] = acc_ref[st, sc, :, cq] * alpha + pv
            m_ref[st, sc, :, cq] = jnp.broadcast_to(m_new, (SUBLANES, s.shape[1]))

    ahead = min(PIPELINE_AHEAD, n_sc - 1)

    def run_tile(st, j, diag):
        for sc in range(n_sc):
            if sc + ahead < n_sc:
                produce(st, sc + ahead, j, diag)
            elif not diag:
                produce(st, sc + ahead - n_sc, j + 1)
            elif st == 0:
                produce(1, sc + ahead - n_sc, first_tile(q_tiles[1]))
            consume(st, sc, j, diag)

    def finalize(st, i):
        def normalised(sc):
            a = acc_ref[st, sc]
            return a[0:HEAD_DIM] / a[HEAD_DIM:HEAD_DIM + 1]

        if mode == "diff":
            lam_init = lam_ref[4:5, 0:1]
            lam = (jnp.exp(jnp.sum(lam_ref[0:1] * lam_ref[1:2], axis=1, keepdims=True))
                   - jnp.exp(jnp.sum(lam_ref[2:3] * lam_ref[3:4], axis=1, keepdims=True)) + lam_init)
        for p in range(n_pairs):
            outs = []
            for h in range(2):
                if mode == "diff":
                    sc = per_pair * p + 2 * h
                    d = normalised(sc) - lam * normalised(sc + 1)
                    ms = jnp.mean(d * d, axis=0, keepdims=True)
                    outs.append(d * lax.rsqrt(ms + SUBLN_EPS) * gcol_ref[...] * (1.0 - lam_init))
                else:
                    outs.append(normalised(per_pair * p + h))
            o_ref[pl.ds(pl.multiple_of(i * T, T), T), LANES * p:LANES * (p + 1)] = (
                jnp.concatenate(outs, axis=0).T.astype(BF16))

    @pl.when((pl.program_id(0) == 0) & (pl.program_id(1) == 0) & (pl.program_id(2) == 0))
    def _():
        qe_ref[...] = jnp.zeros(qe_ref.shape, BF16)

    for st in range(2):
        setup(st, q_tiles[st])
    for sc in range(ahead):
        produce(0, sc, first_tile(q_tiles[0]))
    for st in range(2):
        i = q_tiles[st]

        lo = first_tile(i)
        odd = jnp.bitwise_and(i - lo, 1)

        def step1(j, carry, st=st):
            run_tile(st, j, False)
            return carry

        def step2(jj, carry, st=st, lo=lo, odd=odd):
            j = lo + odd + 2 * jj
            run_tile(st, j, False)
            run_tile(st, j + 1, False)
            return carry

        lax.fori_loop(lo, lo + odd, step1, 0)
        lax.fori_loop(0, lax.shift_right_logical(i - lo - odd, 1), step2, 0)
        run_tile(st, i, True)
    for st in range(2):
        finalize(st, q_tiles[st])


def _attn_call(mode, qt, k, vt, extra, T, n_back=0, pairs_per_step=None, layer=0):
    b, npair, nt = qt.shape[:3]
    assert nt % 2 == 0
    pps = npair if pairs_per_step is None else pairs_per_step
    r = k.shape[-1]
    n_sc = (4 if mode == "diff" else 2) * pps
    in_specs = [
        pl.BlockSpec((None, pps, nt, LANES, T), lambda bi, p, i: (bi, p, 0, 0, 0)),
        pl.BlockSpec((None, pps, nt, T, r), lambda bi, p, i: (bi, p, 0, 0, 0)),
        pl.BlockSpec((None, pps, nt, 2 * V_ROWS, T), lambda bi, p, i: (bi, p, 0, 0, 0)),
    ]
    if mode == "fox":
        in_specs.append(pl.BlockSpec((None, nt, pps, GATE_STRIDE, T), lambda bi, p, i: (bi, 0, p, 0, 0)))
    elif mode == "diff":
        in_specs += [_layer_spec(extra[0].shape, layer), _layer_spec(extra[1].shape, layer)]
    else:
        in_specs.append(_const_spec(extra[0].shape))
    body = functools.partial(_attn_body, mode=mode, n_back=n_back)
    return pl.pallas_call(
        body,
        grid=(b, npair // pps, nt // 2),
        in_specs=in_specs,
        out_specs=pl.BlockSpec((None, nt * T, pps * LANES), lambda bi, p, i: (bi, 0, p)),
        out_shape=jax.ShapeDtypeStruct((b, nt * T, npair * LANES), BF16),
        scratch_shapes=[pltpu.VMEM((2, n_sc, r, T), BF16),
                        pltpu.VMEM((2, n_sc, SUBLANES, T), F32),
                        pltpu.VMEM((2, n_sc, V_ROWS, T), F32),
                        pltpu.VMEM((n_sc, T, T), F32),
                        pltpu.VMEM((n_sc, SUBLANES, T), F32)],
        compiler_params=pltpu.CompilerParams(dimension_semantics=("arbitrary", "arbitrary", "arbitrary"),
                                             vmem_limit_bytes=VMEM_LIMIT_BYTES),
        name="attn_" + mode,
    )(qt, k, vt, *extra)


def _constants(s, T):
    pos = np.arange(s, dtype=np.float64)[:, None]

    def angles(half):
        inv = ROPE_THETA ** (-np.arange(half, dtype=np.float64) / half)
        return pos * inv[None, :]

    ad, ac = angles(DIFF_QK_DIM // 2), angles(HEAD_DIM // 2)

    def nat(a):
        reps = LANES // (2 * a.shape[1])
        cos = np.tile(np.concatenate([np.cos(a), np.cos(a)], axis=1), (1, reps))
        sin = np.tile(np.concatenate([-np.sin(a), np.sin(a)], axis=1), (1, reps))
        return cos.astype(np.float32), sin.astype(np.float32)

    cnd, snd = nat(ad)
    cnc, snc = nat(ac)
    consts = dict(cnd=cnd, snd=snd, cnc=cnc, snc=snc,
                  ctd=np.cos(ad).T.astype(np.float32), std=np.sin(ad).T.astype(np.float32),
                  ctc=np.cos(ac).T.astype(np.float32), stc=np.sin(ac).T.astype(np.float32))
    consts["tri"] = np.triu(np.ones((T, T), np.float32))
    n_back = -(-max(w for w, _ in DILATED_PATTERNS) // T)
    kk = np.arange(T)[:, None]
    qq = np.arange(T)[None, :]
    bias = np.zeros((n_back + 1, T, T), np.float32)
    for o in range(n_back + 1):
        delta = o * T + qq - kk
        mult = np.zeros((T, T), np.float64)
        for w, dil in DILATED_PATTERNS:
            mult += (delta >= 0) & (delta <= w) & (delta % dil == 0)
        bias[o] = np.where(mult > 0, np.log2(np.maximum(mult, 1.0)), NEG_BIG)
    out = {}
    for name, v in consts.items():
        out[name] = jnp.asarray(v, BF16 if name == "tri" else F32)
    return out, jnp.asarray(bias), n_back


def _proj_weights(w_in, b_f):
    depth, d, _ = w_in.shape
    w = w_in.astype(BF16)
    sizes = [W_DIFF] * 3 + [W_FOX] * 3 + [N_FOX_HEADS] + [W_DIL] * 3
    aq, ak, av, fq, fk, fv, fg, cq, ck, cv = jnp.split(w, np.cumsum(sizes)[:-1].tolist(), axis=2)
    rows = np.array([GATE_STRIDE * (h // 2) + h % 2 for h in range(N_FOX_HEADS)])
    fg_cols = jnp.zeros((depth, d, GATE_ROWS), BF16).at[:, :, rows].set(fg)
    bf_pad = jnp.zeros((depth, GATE_ROWS, 1), F32).at[:, rows, 0].set(b_f)
    wn = jnp.concatenate([ak, fk, ck], axis=2)
    wt = jnp.concatenate([fg_cols, aq, fq, cq, av, fv, cv], axis=2).transpose(0, 2, 1)
    return wn, wt, bf_pad


def _forward(x, w_in, b_f, lam_q1, lam_k1, lam_q2, lam_k2, g_sub, w_o, g_ffn1, w1_gate, w1_up, w1_down,
             g_mix, g_ffn2, w2_gate, w2_up, w2_down, g_final, T=512, tm=512):
    b, s, d = x.shape
    depth = w_in.shape[0]
    assert s % T == 0 and (b * s) % tm == 0 and d == HEAD_DIM * (N_DIFF_HEADS + N_FOX_HEADS + N_DIL_HEADS)
    consts, bias, n_back = _constants(s, T)
    wn_all, wt_all, bf_all = _proj_weights(w_in, b_f)
    ffn1 = (g_ffn1[:, None, :], w1_gate.astype(BF16), w1_up.astype(BF16), w1_down.astype(BF16))
    ffn2 = (g_ffn2[:, None, :], w2_gate.astype(BF16), w2_up.astype(BF16), w2_down.astype(BF16))
    g_mix3 = g_mix[:, None, :]
    wo_all = w_o.astype(BF16)
    lam_init = np.array([0.8 - 0.6 * math.exp(-0.3 * l) for l in range(depth)], np.float32)
    lam_all = jnp.zeros((depth, SUBLANES, LANES), F32)
    lam_all = lam_all.at[:, 0:4, 0:DIFF_QK_DIM].set(jnp.stack([lam_q1, lam_k1, lam_q2, lam_k2], axis=1))
    lam_all = lam_all.at[:, 4, :].set(jnp.asarray(lam_init)[:, None])
    x2d = x.reshape(b * s, d)
    for l in range(depth):
        if l == 0:
            x2d = _ffn_call(x2d, l, *ffn1, tm=tm)
        (qtd, kd, vtd, qtf, kf, vtf, cqf, qtc, kc, vtc) = _proj_call(
            x2d.reshape(b, s, d), l, g_mix3, wn_all, wt_all, bf_all, consts, T)
        o_a = _attn_call("diff", qtd, kd, vtd, (lam_all, g_sub[:, :, None]), T, layer=l)
        o_b = _attn_call("fox", qtf, kf, vtf, (cqf.reshape(b, s // T, P_FOX, GATE_STRIDE, T),), T)
        o_c = _attn_call("dil", qtc, kc, vtc, (bias,), T, n_back=n_back)
        attn = ([o_a.reshape(b * s, -1), o_b.reshape(b * s, -1), o_c.reshape(b * s, -1)], wo_all)
        last = l == depth - 1
        x2d = _ffn_call(x2d, l, *ffn2, attn=attn, g_final=g_final if last else None, tm=tm)
        if not last:
            x2d = _ffn_call(x2d, l + 1, *ffn1, tm=tm)
    return x2d.reshape(b, s, d)


def kernel(x, w_in, b_f, lam_q1, lam_k1, lam_q2, lam_k2, g_sub, w_o, g_ffn1, w1_gate, w1_up, w1_down,
           g_mix, g_ffn2, w2_gate, w2_up, w2_down, g_final):
    return _forward(x, w_in, b_f, lam_q1, lam_k1, lam_q2, lam_k2, g_sub, w_o, g_ffn1, w1_gate, w1_up,
                    w1_down, g_mix, g_ffn2, w2_gate, w2_up, w2_down, g_final)
```

```python
import functools
import math

import numpy as np
import jax
import jax.numpy as jnp
from jax import lax
from jax.experimental import pallas as pl
from jax.experimental.pallas import tpu as pltpu

F32 = jnp.float32
BF16 = jnp.bfloat16

HEAD_DIM = 64
N_DIFF_HEADS = 6
N_FOX_HEADS = 6
N_DIL_HEADS = 4
DIFF_QK_DIM = 32
DILATED_PATTERNS = ((128, 1), (512, 4), (2048, 16))
ROPE_THETA = 10000.0
NORM_EPS = 1e-6
SUBLN_EPS = 1e-5
MACARON_SCALE = 0.5
NEG_BIG = -1e30
LOG2E = 1.4426950408889634

LANES = 128
SUBLANES = 8
MXU_WIDTH = 256
BF16_ROWS = 16
V_ROWS = HEAD_DIM + BF16_ROWS
GATE_STRIDE = 8
VMEM_LIMIT_BYTES = 56 * 1024 * 1024
PIPELINE_AHEAD = 2


def _dot(a, b):
    return jnp.dot(a, b, preferred_element_type=F32)


def _dot_nt(a, b):
    return lax.dot_general(a, b, (((1,), (1,)), ((), ())), preferred_element_type=F32)


def _rms(x, g, eps):
    ms = jnp.mean(x * x, axis=-1, keepdims=True)
    return x * lax.rsqrt(ms + eps) * g


def _split3(x):
    hi = x.astype(BF16)
    r1 = x - hi.astype(F32)
    mid = r1.astype(BF16)
    lo = (r1 - mid.astype(F32)).astype(BF16)
    return hi, mid, lo


def _ffn_body(*refs, has_attn, has_final, n_chunks):
    refs = list(refs)
    x_ref = refs.pop(0)
    if has_attn:
        oa_ref, ob_ref, oc_ref, wo_ref = refs[:4]
        refs = refs[4:]
    g_ref, wg_ref, wu_ref, wd_ref = refs[:4]
    refs = refs[4:]
    if has_final:
        gf_ref = refs.pop(0)
    out_ref = refs[0]

    x = x_ref[...]
    if has_attn:
        o = jnp.concatenate([oa_ref[...], ob_ref[...], oc_ref[...]], axis=1)
        x = x + _dot(o, wo_ref[...])
    xn = _rms(x, g_ref[...], NORM_EPS).astype(BF16)
    n_wide = wg_ref.shape[1] // MXU_WIDTH
    bounds = [MXU_WIDTH * ((n_wide * c + n_chunks - 1) // n_chunks) for c in range(n_chunks + 1)]
    acc = None
    for c in range(n_chunks):
        sl = slice(bounds[c], bounds[c + 1])
        gate = _dot(xn, wg_ref[:, sl])
        up = _dot(xn, wu_ref[:, sl])
        h = (gate * jax.nn.sigmoid(gate) * up).astype(BF16)
        d = _dot(h, wd_ref[sl, :])
        acc = d if acc is None else acc + d
    y = x + MACARON_SCALE * acc
    if has_final:
        y = _rms(y, gf_ref[...], NORM_EPS)
    out_ref[...] = y


def _const_spec(shape):
    nd = len(shape)
    return pl.BlockSpec(shape, lambda *_: (0,) * nd, pipeline_mode=pl.Buffered(1))


def _layer_spec(shape, l):
    nd = len(shape)
    return pl.BlockSpec((None,) + tuple(shape[1:]), lambda *_: (l,) + (0,) * (nd - 1),
                        pipeline_mode=pl.Buffered(1))


def _ffn_call(x2d, l, g, wg, wu, wd, attn=None, g_final=None, tm=512):
    n, d = x2d.shape
    dff = wg.shape[2]
    args = [x2d]
    specs = [pl.BlockSpec((tm, d), lambda i: (i, 0))]
    if attn is not None:
        o_list, wo = attn
        for o in o_list:
            args.append(o)
            specs.append(pl.BlockSpec((tm, o.shape[1]), lambda i: (i, 0)))
        args.append(wo)
        specs.append(_layer_spec(wo.shape, l))
    args += [g, wg, wu, wd]
    specs += [_layer_spec(g.shape, l), _layer_spec(wg.shape, l), _layer_spec(wu.shape, l),
              _layer_spec(wd.shape, l)]
    if g_final is not None:
        args.append(g_final.reshape(1, d))
        specs.append(_const_spec((1, d)))
    assert dff % MXU_WIDTH == 0
    n_chunks = dff // MXU_WIDTH
    body = functools.partial(_ffn_body, has_attn=attn is not None, has_final=g_final is not None,
                             n_chunks=n_chunks)
    return pl.pallas_call(
        body,
        grid=(n // tm,),
        in_specs=specs,
        out_specs=pl.BlockSpec((tm, d), lambda i: (i, 0)),
        out_shape=jax.ShapeDtypeStruct((n, d), F32),
        compiler_params=pltpu.CompilerParams(dimension_semantics=("arbitrary",),
                                             vmem_limit_bytes=VMEM_LIMIT_BYTES),
        name="ffn",
    )(*args)


W_DIFF = N_DIFF_HEADS * HEAD_DIM
W_FOX = N_FOX_HEADS * HEAD_DIM
W_DIL = N_DIL_HEADS * HEAD_DIM
P_DIFF, P_FOX, P_DIL = W_DIFF // LANES, W_FOX // LANES, W_DIL // LANES
KN_FOX, KN_DIL, N_NAT = W_DIFF, W_DIFF + W_FOX, W_DIFF + W_FOX + W_DIL
QT_FOX, QT_DIL, VT_DIFF = W_DIFF, W_DIFF + W_FOX, N_NAT
VT_FOX, VT_DIL, N_TR = N_NAT + W_DIFF, N_NAT + W_DIFF + W_FOX, 2 * N_NAT
GATE_ROWS = 32


def _proj_body(x_ref, g_ref, wn_ref, wt_ref, bf_ref, tri_ref,
               cnd_ref, snd_ref, cnc_ref, snc_ref, ctd_ref, std_ref, ctc_ref, stc_ref,
               qtd_ref, kd_ref, vtd_ref, qtf_ref, kf_ref, vtf_ref, cq_ref, qtc_ref, kc_ref, vtc_ref,
               carry_ref):
    tm = x_ref.shape[0]

    @pl.when(pl.program_id(1) == 0)
    def _():
        carry_ref[...] = jnp.zeros_like(carry_ref)

    xn = _rms(x_ref[...], g_ref[...], NORM_EPS).astype(BF16)
    rn = _dot(xn, wn_ref[...])
    rt_all = _dot_nt(wt_ref[...], xn)
    gate_t = rt_all[0:GATE_ROWS]
    rt = rt_all[GATE_ROWS:]

    lane = lax.broadcasted_iota(jnp.int32, (tm, LANES), 1)

    def rope_nat(blk, half, cos, sin_signed):
        first = (lane & half) == 0
        rot = jnp.where(first, pltpu.roll(blk, LANES - half, 1), pltpu.roll(blk, half, 1))
        return (blk * cos + rot * sin_signed).astype(BF16)

    for p in range(P_DIFF):
        kd_ref[p] = rope_nat(rn[:, LANES * p:LANES * (p + 1)], DIFF_QK_DIM // 2, cnd_ref[...], snd_ref[...])
    for p in range(P_DIL):
        c0 = KN_DIL + LANES * p
        kc_ref[p] = rope_nat(rn[:, c0:c0 + LANES], HEAD_DIM // 2, cnc_ref[...], snc_ref[...])

    def rope_tr(ref, p, r0, x1, x2, cos, sin, scale):
        half = x1.shape[0]
        ref[p, r0:r0 + half, :] = ((x1 * cos - x2 * sin) * scale).astype(BF16)
        ref[p, r0 + half:r0 + 2 * half, :] = ((x2 * cos + x1 * sin) * scale).astype(BF16)

    sd = DIFF_QK_DIM ** -0.5 * LOG2E
    hd = DIFF_QK_DIM // 2
    for g in range(2 * N_DIFF_HEADS):
        b0 = DIFF_QK_DIM * g
        rope_tr(qtd_ref, g // 4, DIFF_QK_DIM * (g % 4), rt[b0:b0 + hd], rt[b0 + hd:b0 + 2 * hd],
                ctd_ref[...], std_ref[...], sd)
    sf = HEAD_DIM ** -0.5 * LOG2E
    for p in range(P_FOX):
        b0 = QT_FOX + LANES * p
        qtf_ref[p] = (rt[b0:b0 + LANES] * sf).astype(BF16)
    hc = HEAD_DIM // 2
    for h in range(N_DIL_HEADS):
        b0 = QT_DIL + HEAD_DIM * h
        rope_tr(qtc_ref, h // 2, HEAD_DIM * (h % 2), rt[b0:b0 + hc], rt[b0 + hc:b0 + 2 * hc],
                ctc_ref[...], stc_ref[...], sf)

    ones_rows = jnp.ones((BF16_ROWS, tm), BF16)

    def put_vt(ref, base, n_pairs):
        for p in range(n_pairs):
            for h in range(2):
                b0 = base + LANES * p + HEAD_DIM * h
                ref[p, V_ROWS * h:V_ROWS * h + HEAD_DIM, :] = rt[b0:b0 + HEAD_DIM].astype(BF16)
                ref[p, V_ROWS * h + HEAD_DIM:V_ROWS * (h + 1), :] = ones_rows

    put_vt(vtd_ref, VT_DIFF, P_DIFF)
    put_vt(vtf_ref, VT_FOX, P_FOX)
    put_vt(vtc_ref, VT_DIL, P_DIL)

    z = gate_t + bf_ref[...]
    logf = (jnp.minimum(z, 0.0) - jnp.log1p(jnp.exp(-jnp.abs(z)))) * LOG2E
    part = _dot(jnp.concatenate(_split3(logf), axis=0), tri_ref[...])
    c = part[0:GATE_ROWS] + part[GATE_ROWS:2 * GATE_ROWS] + part[2 * GATE_ROWS:3 * GATE_ROWS] + carry_ref[...]
    carry_ref[...] = c[:, tm - 1:tm]
    cq_ref[...] = c[0:P_FOX * GATE_STRIDE]
    row = lax.broadcasted_iota(jnp.int32, (LANES, tm), 0)
    ones_pattern = jnp.where((row < 3 * BF16_ROWS) & ((row & (BF16_ROWS - 1)) == 0), 1.0, 0.0)
    for p in range(P_FOX):
        kf_ref[p, :, 0:LANES] = rn[:, KN_FOX + LANES * p:KN_FOX + LANES * (p + 1)].astype(BF16)
        auxt = ones_pattern
        for h in range(2):
            ch = c[GATE_STRIDE * p + h:GATE_STRIDE * p + h + 1]
            for t, term in enumerate(_split3(ch)):
                auxt = jnp.where(row == (3 + h) * BF16_ROWS + t, -term.astype(F32), auxt)
        kf_ref[p, :, LANES:2 * LANES] = auxt.T.astype(BF16)


def _proj_call(x, l, g_mix, wn, wt, bf_pad, consts, T):
    b, s, d = x.shape
    nt = s // T
    tile5 = lambda npair, r, c: pl.BlockSpec((None, npair, None, r, c), lambda bi, ti: (bi, 0, ti, 0, 0))
    nat_tab = pl.BlockSpec((T, LANES), lambda bi, ti: (ti, 0))
    tr_tab = lambda r: pl.BlockSpec((r, T), lambda bi, ti: (0, ti))
    in_specs = [
        pl.BlockSpec((None, T, d), lambda bi, ti: (bi, ti, 0)),
        _layer_spec(g_mix.shape, l), _layer_spec(wn.shape, l), _layer_spec(wt.shape, l),
        _layer_spec(bf_pad.shape, l), _const_spec((T, T)),
        nat_tab, nat_tab, nat_tab, nat_tab,
        tr_tab(DIFF_QK_DIM // 2), tr_tab(DIFF_QK_DIM // 2), tr_tab(HEAD_DIM // 2), tr_tab(HEAD_DIM // 2),
    ]
    out_shape = [
        jax.ShapeDtypeStruct((b, P_DIFF, nt, LANES, T), BF16),
        jax.ShapeDtypeStruct((b, P_DIFF, nt, T, LANES), BF16),
        jax.ShapeDtypeStruct((b, P_DIFF, nt, 2 * V_ROWS, T), BF16),
        jax.ShapeDtypeStruct((b, P_FOX, nt, LANES, T), BF16),
        jax.ShapeDtypeStruct((b, P_FOX, nt, T, 2 * LANES), BF16),
        jax.ShapeDtypeStruct((b, P_FOX, nt, 2 * V_ROWS, T), BF16),
        jax.ShapeDtypeStruct((b, nt, P_FOX * GATE_STRIDE, T), F32),
        jax.ShapeDtypeStruct((b, P_DIL, nt, LANES, T), BF16),
        jax.ShapeDtypeStruct((b, P_DIL, nt, T, LANES), BF16),
        jax.ShapeDtypeStruct((b, P_DIL, nt, 2 * V_ROWS, T), BF16),
    ]
    out_specs = [
        tile5(P_DIFF, LANES, T), tile5(P_DIFF, T, LANES), tile5(P_DIFF, 2 * V_ROWS, T),
        tile5(P_FOX, LANES, T), tile5(P_FOX, T, 2 * LANES), tile5(P_FOX, 2 * V_ROWS, T),
        pl.BlockSpec((None, None, P_FOX * GATE_STRIDE, T), lambda bi, ti: (bi, ti, 0, 0)),
        tile5(P_DIL, LANES, T), tile5(P_DIL, T, LANES), tile5(P_DIL, 2 * V_ROWS, T),
    ]
    return pl.pallas_call(
        _proj_body,
        grid=(b, nt),
        in_specs=in_specs,
        out_specs=out_specs,
        out_shape=out_shape,
        scratch_shapes=[pltpu.VMEM((GATE_ROWS, 1), F32)],
        compiler_params=pltpu.CompilerParams(dimension_semantics=("arbitrary", "arbitrary"),
                                             vmem_limit_bytes=VMEM_LIMIT_BYTES),
        name="proj",
    )(x, g_mix, wn, wt, bf_pad, consts["tri"], consts["cnd"], consts["snd"], consts["cnc"], consts["snc"],
      consts["ctd"], consts["std"], consts["ctc"], consts["stc"])


def _attn_body(*refs, mode, n_back):
    if mode == "fox":
        qt_ref, k_ref, vt_ref, cq_ref, o_ref, qe_ref, m_ref, acc_ref, s_ref, tmax_ref = refs
    elif mode == "diff":
        qt_ref, k_ref, vt_ref, lam_ref, gcol_ref, o_ref, qe_ref, m_ref, acc_ref, s_ref, tmax_ref = refs
    else:
        qt_ref, k_ref, vt_ref, bias_ref, o_ref, qe_ref, m_ref, acc_ref, s_ref, tmax_ref = refs
    n_pairs, nt, _, T = qt_ref.shape
    n_sc = s_ref.shape[0]
    per_pair = n_sc // n_pairs
    rows = LANES // per_pair
    q_tiles = (pl.program_id(2), nt - 1 - pl.program_id(2))

    def first_tile(i):
        return jnp.maximum(i - n_back, 0) if mode == "dil" else 0

    def setup(st, i):
        for sc in range(n_sc):
            p, r = divmod(sc, per_pair)
            qe_ref[st, sc, rows * r:rows * (r + 1), :] = qt_ref[p, i, rows * r:rows * (r + 1), :]
        if mode == "fox":
            for sc in range(n_sc):
                p, h = divmod(sc, 2)
                terms = _split3(cq_ref[i, p, h:h + 1, :])
                blocks = [jnp.broadcast_to(t.astype(F32), (BF16_ROWS, T)) for t in terms]
                blocks += [jnp.full((BF16_ROWS, T), 1.0 if h == hh else 0.0, F32) for hh in range(2)]
                blocks += [jnp.zeros((LANES - 5 * BF16_ROWS, T), F32)]
                qe_ref[st, sc, LANES:2 * LANES, :] = jnp.concatenate(blocks, axis=0).astype(BF16)
        m_ref[st] = jnp.full(m_ref.shape[1:], NEG_BIG, F32)
        acc_ref[st] = jnp.zeros(acc_ref.shape[1:], F32)

    half = T // 2
    full_blocks = ((slice(0, T), slice(0, T)),)
    diag_blocks = ((slice(0, half), slice(0, half)), (slice(0, T), slice(half, T)))

    def produce(st, sc, j, diag=False):
        for rk, cq in (diag_blocks if diag else full_blocks):
            s = _dot(k_ref[sc // per_pair, j, rk, :], qe_ref[st, sc, :, cq])
            if mode == "dil":
                s = s + bias_ref[q_tiles[st] - j, rk, cq]
            s_ref[sc, rk, cq] = s
            if mode == "dil" or not diag:
                tmax_ref[sc, :, cq] = jnp.broadcast_to(jnp.max(s, axis=0, keepdims=True),
                                                       (SUBLANES, s.shape[1]))

    def consume(st, sc, j, diag):
        p, r = divmod(sc, per_pair)
        h = r * 2 // per_pair
        for rk, cq in (diag_blocks if diag else full_blocks):
            s = s_ref[sc, rk, cq]
            if diag and mode != "dil":
                kk = lax.broadcasted_iota(jnp.int32, s.shape, 0)
                qq = lax.broadcasted_iota(jnp.int32, s.shape, 1) + cq.start
                s = jnp.where(kk <= qq, s, NEG_BIG)
                tmax = jnp.max(s, axis=0, keepdims=True)
            else:
                tmax = tmax_ref[sc, 0:1, cq]
            m_old = m_ref[st, sc, 0:1, cq]
            m_new = jnp.maximum(m_old, tmax)
            alpha = jnp.exp2(m_old - m_new)
            pr = jnp.exp2(s - m_new).astype(BF16)
            pv = _dot(vt_ref[p, j, V_ROWS * h:V_ROWS * (h + 1), rk], pr)
            acc_ref[st, sc, :, cq] = acc_ref[st, sc, :, cq] * alpha + pv
            m_ref[st, sc, :, cq] = jnp.broadcast_to(m_new, (SUBLANES, s.shape[1]))

    ahead = min(PIPELINE_AHEAD, n_sc - 1)

    def run_tile(st, j, diag):
        for sc in range(n_sc):
            if sc + ahead < n_sc:
                produce(st, sc + ahead, j, diag)
            elif not diag:
                produce(st, sc + ahead - n_sc, j + 1)
            elif st == 0:
                produce(1, sc + ahead - n_sc, first_tile(q_tiles[1]))
            consume(st, sc, j, diag)

    def finalize(st, i):
        def normalised(sc):
            a = acc_ref[st, sc]
            return a[0:HEAD_DIM] / a[HEAD_DIM:HEAD_DIM + 1]

        if mode == "diff":
            lam_init = lam_ref[4:5, 0:1]
            lam = (jnp.exp(jnp.sum(lam_ref[0:1] * lam_ref[1:2], axis=1, keepdims=True))
                   - jnp.exp(jnp.sum(lam_ref[2:3] * lam_ref[3:4], axis=1, keepdims=True)) + lam_init)
        for p in range(n_pairs):
            outs = []
            for h in range(2):
                if mode == "diff":
                    sc = per_pair * p + 2 * h
                    d = normalised(sc) - lam * normalised(sc + 1)
                    ms = jnp.mean(d * d, axis=0, keepdims=True)
                    outs.append(d * lax.rsqrt(ms + SUBLN_EPS) * gcol_ref[...] * (1.0 - lam_init))
                else:
                    outs.append(normalised(per_pair * p + h))
            o_ref[pl.ds(pl.multiple_of(i * T, T), T), LANES * p:LANES * (p + 1)] = (
                jnp.concatenate(outs, axis=0).T.astype(BF16))

    @pl.when((pl.program_id(0) == 0) & (pl.program_id(1) == 0) & (pl.program_id(2) == 0))
    def _():
        qe_ref[...] = jnp.zeros(qe_ref.shape, BF16)

    for st in range(2):
        setup(st, q_tiles[st])
    for sc in range(ahead):
        produce(0, sc, first_tile(q_tiles[0]))
    for st in range(2):
        i = q_tiles[st]

        lo = first_tile(i)
        odd = jnp.bitwise_and(i - lo, 1)

        def step1(j, carry, st=st):
            run_tile(st, j, False)
            return carry

        def step2(jj, carry, st=st, lo=lo, odd=odd):
            j = lo + odd + 2 * jj
            run_tile(st, j, False)
            run_tile(st, j + 1, False)
            return carry

        two = jnp.bitwise_and(lax.shift_right_logical(i - lo, 1), 1)

        def step4(jj, carry, st=st, lo=lo, odd=odd, two=two):
            j = lo + odd + 2 * two + 4 * jj
            for t in range(4):
                run_tile(st, j + t, False)
            return carry

        lax.fori_loop(lo, lo + odd, step1, 0)
        lax.fori_loop(0, two, step2, 0)
        lax.fori_loop(0, lax.shift_right_logical(i - lo, 2), step4, 0)
        run_tile(st, i, True)
    for st in range(2):
        finalize(st, q_tiles[st])


def _attn_call(mode, qt, k, vt, extra, T, n_back=0, pairs_per_step=None, layer=0):
    b, npair, nt = qt.shape[:3]
    assert nt % 2 == 0
    pps = npair if pairs_per_step is None else pairs_per_step
    r = k.shape[-1]
    n_sc = (4 if mode == "diff" else 2) * pps
    in_specs = [
        pl.BlockSpec((None, pps, nt, LANES, T), lambda bi, p, i: (bi, p, 0, 0, 0)),
        pl.BlockSpec((None, pps, nt, T, r), lambda bi, p, i: (bi, p, 0, 0, 0)),
        pl.BlockSpec((None, pps, nt, 2 * V_ROWS, T), lambda bi, p, i: (bi, p, 0, 0, 0)),
    ]
    if mode == "fox":
        in_specs.append(pl.BlockSpec((None, nt, pps, GATE_STRIDE, T), lambda bi, p, i: (bi, 0, p, 0, 0)))
    elif mode == "diff":
        in_specs += [_layer_spec(extra[0].shape, layer), _layer_spec(extra[1].shape, layer)]
    else:
        in_specs.append(_const_spec(extra[0].shape))
    body = functools.partial(_attn_body, mode=mode, n_back=n_back)
    return pl.pallas_call(
        body,
        grid=(b, npair // pps, nt // 2),
        in_specs=in_specs,
        out_specs=pl.BlockSpec((None, nt * T, pps * LANES), lambda bi, p, i: (bi, 0, p)),
        out_shape=jax.ShapeDtypeStruct((b, nt * T, npair * LANES), BF16),
        scratch_shapes=[pltpu.VMEM((2, n_sc, r, T), BF16),
                        pltpu.VMEM((2, n_sc, SUBLANES, T), F32),
                        pltpu.VMEM((2, n_sc, V_ROWS, T), F32),
                        pltpu.VMEM((n_sc, T, T), F32),
                        pltpu.VMEM((n_sc, SUBLANES, T), F32)],
        compiler_params=pltpu.CompilerParams(dimension_semantics=("arbitrary", "arbitrary", "arbitrary"),
                                             vmem_limit_bytes=VMEM_LIMIT_BYTES),
        name="attn_" + mode,
    )(qt, k, vt, *extra)


def _constants(s, T):
    pos = np.arange(s, dtype=np.float64)[:, None]

    def angles(half):
        inv = ROPE_THETA ** (-np.arange(half, dtype=np.float64) / half)
        return pos * inv[None, :]

    ad, ac = angles(DIFF_QK_DIM // 2), angles(HEAD_DIM // 2)

    def nat(a):
        reps = LANES // (2 * a.shape[1])
        cos = np.tile(np.concatenate([np.cos(a), np.cos(a)], axis=1), (1, reps))
        sin = np.tile(np.concatenate([-np.sin(a), np.sin(a)], axis=1), (1, reps))
        return cos.astype(np.float32), sin.astype(np.float32)

    cnd, snd = nat(ad)
    cnc, snc = nat(ac)
    consts = dict(cnd=cnd, snd=snd, cnc=cnc, snc=snc,
                  ctd=np.cos(ad).T.astype(np.float32), std=np.sin(ad).T.astype(np.float32),
                  ctc=np.cos(ac).T.astype(np.float32), stc=np.sin(ac).T.astype(np.float32))
    consts["tri"] = np.triu(np.ones((T, T), np.float32))
    n_back = -(-max(w for w, _ in DILATED_PATTERNS) // T)
    kk = np.arange(T)[:, None]
    qq = np.arange(T)[None, :]
    bias = np.zeros((n_back + 1, T, T), np.float32)
    for o in range(n_back + 1):
        delta = o * T + qq - kk
        mult = np.zeros((T, T), np.float64)
        for w, dil in DILATED_PATTERNS:
            mult += (delta >= 0) & (delta <= w) & (delta % dil == 0)
        bias[o] = np.where(mult > 0, np.log2(np.maximum(mult, 1.0)), NEG_BIG)
    out = {}
    for name, v in consts.items():
        out[name] = jnp.asarray(v, BF16 if name == "tri" else F32)
    return out, jnp.asarray(bias), n_back


def _proj_weights(w_in, b_f):
    depth, d, _ = w_in.shape
    w = w_in.astype(BF16)
    sizes = [W_DIFF] * 3 + [W_FOX] * 3 + [N_FOX_HEADS] + [W_DIL] * 3
    aq, ak, av, fq, fk, fv, fg, cq, ck, cv = jnp.split(w, np.cumsum(sizes)[:-1].tolist(), axis=2)
    rows = np.array([GATE_STRIDE * (h // 2) + h % 2 for h in range(N_FOX_HEADS)])
    fg_cols = jnp.zeros((depth, d, GATE_ROWS), BF16).at[:, :, rows].set(fg)
    bf_pad = jnp.zeros((depth, GATE_ROWS, 1), F32).at[:, rows, 0].set(b_f)
    wn = jnp.concatenate([ak, fk, ck], axis=2)
    wt = jnp.concatenate([fg_cols, aq, fq, cq, av, fv, cv], axis=2).transpose(0, 2, 1)
    return wn, wt, bf_pad


def _forward(x, w_in, b_f, lam_q1, lam_k1, lam_q2, lam_k2, g_sub, w_o, g_ffn1, w1_gate, w1_up, w1_down,
             g_mix, g_ffn2, w2_gate, w2_up, w2_down, g_final, T=512, tm=512):
    b, s, d = x.shape
    depth = w_in.shape[0]
    assert s % T == 0 and (b * s) % tm == 0 and d == HEAD_DIM * (N_DIFF_HEADS + N_FOX_HEADS + N_DIL_HEADS)
    consts, bias, n_back = _constants(s, T)
    wn_all, wt_all, bf_all = _proj_weights(w_in, b_f)
    ffn1 = (g_ffn1[:, None, :], w1_gate.astype(BF16), w1_up.astype(BF16), w1_down.astype(BF16))
    ffn2 = (g_ffn2[:, None, :], w2_gate.astype(BF16), w2_up.astype(BF16), w2_down.astype(BF16))
    g_mix3 = g_mix[:, None, :]
    wo_all = w_o.astype(BF16)
    lam_init = np.array([0.8 - 0.6 * math.exp(-0.3 * l) for l in range(depth)], np.float32)
    lam_all = jnp.zeros((depth, SUBLANES, LANES), F32)
    lam_all = lam_all.at[:, 0:4, 0:DIFF_QK_DIM].set(jnp.stack([lam_q1, lam_k1, lam_q2, lam_k2], axis=1))
    lam_all = lam_all.at[:, 4, :].set(jnp.asarray(lam_init)[:, None])
    x2d = x.reshape(b * s, d)
    for l in range(depth):
        if l == 0:
            x2d = _ffn_call(x2d, l, *ffn1, tm=tm)
        (qtd, kd, vtd, qtf, kf, vtf, cqf, qtc, kc, vtc) = _proj_call(
            x2d.reshape(b, s, d), l, g_mix3, wn_all, wt_all, bf_all, consts, T)
        o_a = _attn_call("diff", qtd, kd, vtd, (lam_all, g_sub[:, :, None]), T, layer=l)
        o_b = _attn_call("fox", qtf, kf, vtf, (cqf.reshape(b, s // T, P_FOX, GATE_STRIDE, T),), T)
        o_c = _attn_call("dil", qtc, kc, vtc, (bias,), T, n_back=n_back)
        attn = ([o_a.reshape(b * s, -1), o_b.reshape(b * s, -1), o_c.reshape(b * s, -1)], wo_all)
        last = l == depth - 1
        x2d = _ffn_call(x2d, l, *ffn2, attn=attn, g_final=g_final if last else None, tm=tm)
        if not last:
            x2d = _ffn_call(x2d, l + 1, *ffn1, tm=tm)
    return x2d.reshape(b, s, d)


def kernel(x, w_in, b_f, lam_q1, lam_k1, lam_q2, lam_k2, g_sub, w_o, g_ffn1, w1_gate, w1_up, w1_down,
           g_mix, g_ffn2, w2_gate, w2_up, w2_down, g_final):
    return _forward(x, w_in, b_f, lam_q1, lam_k1, lam_q2, lam_k2, g_sub, w_o, g_ffn1, w1_gate, w1_up,
                    w1_down, g_mix, g_ffn2, w2_gate, w2_up, w2_down, g_final)
```
